```python
import math
import jax, jax.numpy as jnp
from jax import lax
import numpy as np

D_MODEL = 2048
BATCH = 4
SEQ = 2048
DEPTH = 1
DEC_BATCH = 128
DEC_SEQ = 8
PAST_LEN = 16384
PAGE_SIZE = 128

CHUNK = 128
A_GROUPS = 4
A_GROUP_DIM = 256
A_WIDTH = A_GROUPS * A_GROUP_DIM
R_HEADS = 8
R_QK_DIM = 128
R_V_DIM = 256
R_QK_WIDTH = R_HEADS * R_QK_DIM
R_V_WIDTH = R_HEADS * R_V_DIM
R_CHUNK = 128
N_MEM = 256
M_HEADS = 4
M_HEAD_DIM = 256
M_WIDTH = M_HEADS * M_HEAD_DIM
N_BRANCH = 3
D_FF = -(-8 * D_MODEL // (3 * 256)) * 256
ALPHA = (2.0 * DEPTH) ** 0.25
BETA = (8.0 * DEPTH) ** -0.25
ROPE_BASE = 10000.0
LN_EPS = 1e-5

OFF_AU = N_BRANCH * D_MODEL
OFF_AV = OFF_AU + A_WIDTH
OFF_RQ = OFF_AV + A_WIDTH
OFF_RK = OFF_RQ + R_QK_WIDTH
OFF_RV = OFF_RK + R_QK_WIDTH
OFF_RG = OFF_RV + R_V_WIDTH
OFF_MQ = OFF_RG + R_V_WIDTH
IN_WIDTH = OFF_MQ + M_WIDTH

kernel_name = "gated_hybrid_sgu_retention_memory_step"


def _standardize(x, eps=LN_EPS):
    xf = x.astype(jnp.float32)
    mu = jnp.mean(xf, axis=-1, keepdims=True)
    xc = xf - mu
    var = jnp.mean(xc * xc, axis=-1, keepdims=True)
    return xc * lax.rsqrt(var + eps)


def _rotary(x, pos):
    half = x.shape[-1] // 2
    inv = ROPE_BASE ** (-jnp.arange(half, dtype=jnp.float32) / half)
    ang = pos[:, None] * inv[None, :]
    cos = jnp.cos(ang)[None, :, None, :]
    sin = jnp.sin(ang)[None, :, None, :]
    xf = x.astype(jnp.float32)
    x1, x2 = xf[..., :half], xf[..., half:]
    return jnp.concatenate([x1 * cos - x2 * sin, x1 * sin + x2 * cos], axis=-1)


def _retention(q, k, v, s0):
    B, T, H, dk = q.shape
    dv = v.shape[-1]
    C = min(T, R_CHUNK)
    n = T // C
    log_g = jnp.log1p(-jnp.exp2(-5.0 - jnp.arange(H, dtype=jnp.float32)))
    idx = jnp.arange(C, dtype=jnp.float32)
    diff = idx[:, None] - idx[None, :]
    causal = diff >= 0
    dmask = jnp.where(causal[None], jnp.exp(jnp.where(causal, diff, 0.0)[None] * log_g[:, None, None]), 0.0)
    q_decay = jnp.exp((idx[:, None] + 1.0) * log_g[None, :])
    k_decay = jnp.exp((C - 1.0 - idx)[:, None] * log_g[None, :])
    chunk_decay = jnp.exp(C * log_g)

    def to_chunks(a):
        return a.astype(jnp.float32).reshape(B, n, C, H, a.shape[-1]).transpose(1, 0, 2, 3, 4)

    def step(s, inp):
        qc, kc, vc = inp
        scores = jnp.einsum('bihd,bjhd->bhij', qc, kc) * dmask[None]
        o = (jnp.einsum('bhij,bjhe->bihe', scores, vc)
             + jnp.einsum('bihd,bhde->bihe', qc, s) * q_decay[None, :, :, None])
        s_new = (s * chunk_decay[None, :, None, None]
                 + jnp.einsum('bjhd,bjhe->bhde', kc * k_decay[None, :, :, None], vc))
        return s_new, o

    s_fin, o = lax.scan(step, s0.astype(jnp.float32), (to_chunks(q), to_chunks(k), to_chunks(v)))
    o = o.transpose(1, 0, 2, 3, 4).reshape(B, T, H, dv)
    return o, s_fin


def _spatial_gate(u, v, w_s, b_s):
    B, T, G, dg = v.shape
    C = min(T, CHUNK)
    n = T // C
    w = jnp.tril(w_s[:, :C, :C])
    vc = v.reshape(B, n, C, G, dg)
    z = jnp.einsum('gij,bnjgd->bnigd', w, vc) + b_s[:, :C].T[None, None, :, :, None]
    return u * z.reshape(B, T, G, dg)


def _mem_attend(q, mk, mv):
    s = jnp.einsum('bthd,bmhd->bhtm', q.astype(jnp.float32), mk.astype(jnp.float32)) * (M_HEAD_DIM ** -0.5)
    p = jax.nn.softmax(s, axis=-1)
    return jnp.einsum('bhtm,bmhe->bthe', p, mv.astype(jnp.float32))


def _layer(x, pos_start, ret_s0, mem_k, mem_v, p):
    B, T, _ = x.shape
    pos = pos_start + jnp.arange(T, dtype=jnp.float32)
    h = x @ p['w_in']
    gates = jax.nn.sigmoid(h[..., :OFF_AU].astype(jnp.float32)).reshape(B, T, N_BRANCH, D_MODEL)
    u = jax.nn.gelu(h[..., OFF_AU:OFF_AV]).reshape(B, T, A_GROUPS, A_GROUP_DIM)
    va = jax.nn.gelu(h[..., OFF_AV:OFF_RQ]).reshape(B, T, A_GROUPS, A_GROUP_DIM)
    va = _standardize(va) * p['sgu_ln_g'] + p['sgu_ln_b']
    ya = _spatial_gate(u, va, p['sgu_w'], p['sgu_b']).reshape(B, T, A_WIDTH) @ p['w_proj_a']
    q = _rotary(h[..., OFF_RQ:OFF_RK].reshape(B, T, R_HEADS, R_QK_DIM), pos)
    k = _rotary(h[..., OFF_RK:OFF_RV].reshape(B, T, R_HEADS, R_QK_DIM), pos) * (R_QK_DIM ** -0.5)
    vr = h[..., OFF_RV:OFF_RG].reshape(B, T, R_HEADS, R_V_DIM)
    o, s_fin = _retention(q, k, vr, ret_s0)
    o = (_standardize(o) * p['ret_gn_g'].reshape(R_HEADS, R_V_DIM)).reshape(B, T, R_V_WIDTH)
    yb = (jax.nn.silu(h[..., OFF_RG:OFF_MQ].astype(jnp.float32)) * o) @ p['w_proj_b']
    qm = h[..., OFF_MQ:].reshape(B, T, M_HEADS, M_HEAD_DIM)
    yc = _mem_attend(qm, mem_k, mem_v).reshape(B, T, M_WIDTH) @ p['w_proj_c']
    merged = gates[:, :, 0] * ya + gates[:, :, 1] * yb + gates[:, :, 2] * yc
    x1 = _standardize(ALPHA * x + merged @ p['w_out']) * p['ln1_g'] + p['ln1_b']
    f = (jax.nn.silu(x1 @ p['w_ffn_gate']) * (x1 @ p['w_ffn_up'])) @ p['w_ffn_down']
    y = _standardize(ALPHA * x1 + f) * p['ln2_g'] + p['ln2_b']
    return y, s_fin, va


def setup_inputs(seed: int = 0) -> dict:
    key = jax.random.key(seed)
    ks = jax.random.split(key, 32)
    f32 = jnp.float32

    def nrm(k, shape, scale):
        return jax.random.normal(k, shape, f32) * scale

    L = DEPTH
    return {
        'x_prompt': nrm(ks[0], (BATCH, SEQ, D_MODEL), 1.0),
        'x_sample': nrm(ks[1], (DEC_BATCH, DEC_SEQ, D_MODEL), 1.0),
        'mem_prompt': nrm(ks[2], (BATCH, N_MEM, D_MODEL), 1.0),
        'state_ret': nrm(ks[3], (L, DEC_BATCH, R_HEADS, R_QK_DIM, R_V_DIM), 0.1),
        'cache_mem_k': nrm(ks[4], (L, DEC_BATCH, N_MEM, M_HEADS, M_HEAD_DIM), 1.0),
        'cache_mem_v': nrm(ks[5], (L, DEC_BATCH, N_MEM, M_HEADS, M_HEAD_DIM), 1.0),
        'w_in': nrm(ks[6], (L, D_MODEL, IN_WIDTH), D_MODEL ** -0.5),
        'sgu_ln_g': 1.0 + nrm(ks[7], (L, A_GROUPS, A_GROUP_DIM), 0.02),
        'sgu_ln_b': nrm(ks[8], (L, A_GROUPS, A_GROUP_DIM), 0.02),
        'sgu_w': nrm(ks[9], (L, A_GROUPS, CHUNK, CHUNK), CHUNK ** -0.5),
        'sgu_b': 1.0 + nrm(ks[10], (L, A_GROUPS, CHUNK), 0.02),
        'w_proj_a': nrm(ks[11], (L, A_WIDTH, D_MODEL), BETA * A_WIDTH ** -0.5),
        'ret_gn_g': 1.0 + nrm(ks[12], (L, R_V_WIDTH), 0.02),
        'w_proj_b': nrm(ks[13], (L, R_V_WIDTH, D_MODEL), BETA * R_V_WIDTH ** -0.5),
        'w_mem_k': nrm(ks[14], (L, D_MODEL, M_WIDTH), D_MODEL ** -0.5),
        'w_mem_v': nrm(ks[15], (L, D_MODEL, M_WIDTH), D_MODEL ** -0.5),
        'w_proj_c': nrm(ks[16], (L, M_WIDTH, D_MODEL), BETA * M_WIDTH ** -0.5),
        'w_out': nrm(ks[17], (L, D_MODEL, D_MODEL), BETA * D_MODEL ** -0.5),
        'ln1_g': 1.0 + nrm(ks[18], (L, D_MODEL), 0.02),
        'ln1_b': nrm(ks[19], (L, D_MODEL), 0.02),
        'w_ffn_gate': nrm(ks[20], (L, D_MODEL, D_FF), D_MODEL ** -0.5),
        'w_ffn_up': nrm(ks[21], (L, D_MODEL, D_FF), D_MODEL ** -0.5),
        'w_ffn_down': nrm(ks[22], (L, D_FF, D_MODEL), BETA * D_FF ** -0.5),
        'ln2_g': 1.0 + nrm(ks[23], (L, D_MODEL), 0.02),
        'ln2_b': nrm(ks[24], (L, D_MODEL), 0.02),
    }


def reference(x_prompt, x_sample, mem_prompt, state_ret, cache_mem_k, cache_mem_v,
              w_in, sgu_ln_g, sgu_ln_b, sgu_w, sgu_b, w_proj_a, ret_gn_g, w_proj_b,
              w_mem_k, w_mem_v, w_proj_c, w_out, ln1_g, ln1_b,
              w_ffn_gate, w_ffn_up, w_ffn_down, ln2_g, ln2_b):
    bp = x_prompt.shape[0]
    h_p, h_s = x_prompt, x_sample
    ret_p, mk_p_all, mv_p_all, ret_s, cv_s = [], [], [], [], []
    for l in range(DEPTH):
        p = {
            'w_in': w_in[l], 'sgu_ln_g': sgu_ln_g[l], 'sgu_ln_b': sgu_ln_b[l],
            'sgu_w': sgu_w[l], 'sgu_b': sgu_b[l], 'w_proj_a': w_proj_a[l],
            'ret_gn_g': ret_gn_g[l], 'w_proj_b': w_proj_b[l], 'w_proj_c': w_proj_c[l],
            'w_out': w_out[l], 'ln1_g': ln1_g[l], 'ln1_b': ln1_b[l],
            'w_ffn_gate': w_ffn_gate[l], 'w_ffn_up': w_ffn_up[l], 'w_ffn_down': w_ffn_down[l],
            'ln2_g': ln2_g[l], 'ln2_b': ln2_b[l],
        }
        mk_p = (mem_prompt @ w_mem_k[l]).reshape(bp, N_MEM, M_HEADS, M_HEAD_DIM)
        mv_p = (mem_prompt @ w_mem_v[l]).reshape(bp, N_MEM, M_HEADS, M_HEAD_DIM)
        s0 = jnp.zeros((bp, R_HEADS, R_QK_DIM, R_V_DIM), jnp.float32)
        h_p, s_p, _ = _layer(h_p, 0.0, s0, mk_p, mv_p, p)
        h_s, s_s, v_s = _layer(h_s, float(PAST_LEN), state_ret[l], cache_mem_k[l], cache_mem_v[l], p)
        ret_p.append(s_p)
        mk_p_all.append(mk_p)
        mv_p_all.append(mv_p)
        ret_s.append(s_s)
        cv_s.append(v_s)
    return (h_p, h_s, jnp.stack(ret_p), jnp.stack(mk_p_all), jnp.stack(mv_p_all), jnp.stack(ret_s), jnp.stack(cv_s))
```

```python
import functools
import math

import jax
import jax.numpy as jnp
from jax import lax
from jax.experimental import pallas as pl
from jax.experimental.pallas import tpu as pltpu

F32 = jnp.float32
BF16 = jnp.bfloat16

D_MODEL = 2048
DEPTH = 1
PAST_LEN = 16384
CHUNK = 128
A_GROUPS = 4
A_GROUP_DIM = 256
A_WIDTH = A_GROUPS * A_GROUP_DIM
R_HEADS = 8
R_QK_DIM = 128
R_V_DIM = 256
R_QK_WIDTH = R_HEADS * R_QK_DIM
R_V_WIDTH = R_HEADS * R_V_DIM
R_CHUNK = 128
N_MEM = 256
M_HEADS = 4
M_HEAD_DIM = 256
M_WIDTH = M_HEADS * M_HEAD_DIM
N_BRANCH = 3
D_FF = -(-8 * D_MODEL // (3 * 256)) * 256
ALPHA = (2.0 * DEPTH) ** 0.25
ROPE_BASE = 10000.0
LN_EPS = 1e-5

OFF_AU = N_BRANCH * D_MODEL
OFF_AV = OFF_AU + A_WIDTH
OFF_RQ = OFF_AV + A_WIDTH
OFF_RK = OFF_RQ + R_QK_WIDTH
OFF_RV = OFF_RK + R_QK_WIDTH
OFF_RG = OFF_RV + R_V_WIDTH
OFF_MQ = OFF_RG + R_V_WIDTH
IN_WIDTH = OFF_MQ + M_WIDTH

VMEM_LIMIT_BYTES = 56 * 1024 * 1024
LANES = 128


def _params(*sem):
    return pltpu.CompilerParams(dimension_semantics=sem, vmem_limit_bytes=VMEM_LIMIT_BYTES)


def _standardize(x):
    mu = jnp.mean(x, axis=-1, keepdims=True)
    xc = x - mu
    var = jnp.mean(xc * xc, axis=-1, keepdims=True)
    return xc * lax.rsqrt(var + LN_EPS)


IN_TN = 512
_J_AU = OFF_AU // IN_TN
_J_AV = OFF_AV // IN_TN
_J_RQ = OFF_RQ // IN_TN
_J_RK = OFF_RK // IN_TN
_J_RV = OFF_RV // IN_TN
_J_RG = OFF_RG // IN_TN
_J_MQ = OFF_MQ // IN_TN
_J_END = IN_WIDTH // IN_TN


def _in_proj_kernel(x_ref, w_ref, cos_ref, sin_ref, lng_ref, lnb_ref, o_ref, *rest, with_va):
    if with_va:
        va_ref, xb_ref = rest
    else:
        (xb_ref,) = rest
    j = pl.program_id(1)

    @pl.when(j == 0)
    def _():
        xb_ref[...] = x_ref[...].astype(BF16)

    acc = jnp.dot(xb_ref[...], w_ref[...], preferred_element_type=F32)

    @pl.when(j < _J_AU)
    def _():
        o_ref[...] = jax.nn.sigmoid(acc).astype(BF16)

    @pl.when((j >= _J_AU) & (j < _J_AV))
    def _():
        o_ref[...] = jax.nn.gelu(acc).astype(BF16)

    @pl.when((j >= _J_AV) & (j < _J_RQ))
    def _():
        v = jax.nn.gelu(acc)
        for g in range(IN_TN // A_GROUP_DIM):
            cols = slice(g * A_GROUP_DIM, (g + 1) * A_GROUP_DIM)
            vn = _standardize(v[:, cols]) * lng_ref[:, cols] + lnb_ref[:, cols]
            o_ref[:, cols] = vn.astype(BF16)
            if with_va:
                va_ref[:, cols] = vn

    def rotary(scale):
        for hh in range(IN_TN // R_QK_DIM):
            cols = slice(hh * R_QK_DIM, (hh + 1) * R_QK_DIM)
            xh = acc[:, cols]
            r = xh * cos_ref[...] + pltpu.roll(xh, R_QK_DIM // 2, 1) * sin_ref[...]
            if scale != 1.0:
                r = r * scale
            o_ref[:, cols] = r.astype(BF16)

    @pl.when((j >= _J_RQ) & (j < _J_RK))
    def _():
        rotary(1.0)

    @pl.when((j >= _J_RK) & (j < _J_RV))
    def _():
        rotary(R_QK_DIM ** -0.5)

    @pl.when((j >= _J_RV) & (j < _J_RG))
    def _():
        o_ref[...] = acc.astype(BF16)

    @pl.when((j >= _J_RG) & (j < _J_MQ))
    def _():
        o_ref[...] = jax.nn.silu(acc).astype(BF16)

    @pl.when(j >= _J_MQ)
    def _():
        o_ref[...] = (acc * (M_HEAD_DIM ** -0.5)).astype(BF16)


def _in_proj(x, w_in, cos_t, sin_t, ln_g, ln_b, *, tm, with_va):
    m = x.shape[0]
    n_pos_blocks = cos_t.shape[0] // tm
    n_av = A_WIDTH // IN_TN

    def av_idx(j):
        return jnp.clip(j - _J_AV, 0, n_av - 1)

    out_shape = [jax.ShapeDtypeStruct((m, IN_WIDTH), BF16)]
    out_specs = [pl.BlockSpec((tm, IN_TN), lambda i, j: (i, j))]
    if with_va:
        out_shape.append(jax.ShapeDtypeStruct((m, A_WIDTH), F32))
        out_specs.append(pl.BlockSpec((tm, IN_TN), lambda i, j: (i, av_idx(j))))
    return pl.pallas_call(
        functools.partial(_in_proj_kernel, with_va=with_va),
        grid=(m // tm, _J_END),
        in_specs=[
            pl.BlockSpec((tm, D_MODEL), lambda i, j: (i, 0)),
            pl.BlockSpec((D_MODEL, IN_TN), lambda i, j: (0, j)),
            pl.BlockSpec((tm, R_QK_DIM), lambda i, j: (i % n_pos_blocks, 0)),
            pl.BlockSpec((tm, R_QK_DIM), lambda i, j: (i % n_pos_blocks, 0)),
            pl.BlockSpec((1, IN_TN), lambda i, j: (0, av_idx(j))),
            pl.BlockSpec((1, IN_TN), lambda i, j: (0, av_idx(j))),
        ],
        out_specs=out_specs,
        out_shape=out_shape,
        scratch_shapes=[pltpu.VMEM((tm, D_MODEL), BF16)],
        compiler_params=_params("arbitrary", "arbitrary"),
        name="in_proj",
    )(x, w_in, cos_t, sin_t, ln_g, ln_b)


def _matmul_kernel(x_ref, w_ref, o_ref, ob_ref):
    acc = jnp.dot(x_ref[...].astype(BF16), w_ref[...], preferred_element_type=F32)
    o_ref[...] = acc
    ob_ref[...] = acc.astype(BF16)


def _mem_kv(x, w):
    m, k = x.shape
    n = w.shape[1]
    tn = 512
    return pl.pallas_call(
        _matmul_kernel,
        grid=(n // tn,),
        in_specs=[pl.BlockSpec((m, k), lambda j: (0, 0)), pl.BlockSpec((k, tn), lambda j: (0, j))],
        out_specs=[pl.BlockSpec((m, tn), lambda j: (0, j)), pl.BlockSpec((m, tn), lambda j: (0, j))],
        out_shape=[jax.ShapeDtypeStruct((m, n), F32), jax.ShapeDtypeStruct((m, n), BF16)],
        compiler_params=_params("arbitrary"),
        name="mem_kv",
    )(x, w)


def _gated_groupnorm(o, gn, g):
    return (g.astype(F32) * (_standardize(o) * gn)).astype(BF16)


def _ret_prompt_kernel(cd_ref, q_ref, k_ref, v_ref, g_ref, dmask_ref, qd_ref, kd_ref, gn_ref,
                       o_ref, sfin_ref, s_scr, *, n_chunks):
    t = pl.program_id(1)

    @pl.when(t == 0)
    def _():
        s_scr[...] = jnp.zeros_like(s_scr)

    for c in range(n_chunks):
        rows = slice(c * R_CHUNK, (c + 1) * R_CHUNK)
        for h in range(R_HEADS):
            qk = slice(h * R_QK_DIM, (h + 1) * R_QK_DIM)
            vv = slice(h * R_V_DIM, (h + 1) * R_V_DIM)
            q = q_ref[rows, qk]
            k = k_ref[rows, qk]
            v = v_ref[rows, vv]
            s = s_scr[h]
            sc = lax.dot_general(q, k, (((1,), (1,)), ((), ())), preferred_element_type=F32) * dmask_ref[h]
            o = (jnp.dot(sc.astype(BF16), v, preferred_element_type=F32)
                 + jnp.dot(q, s.astype(BF16), preferred_element_type=F32) * qd_ref[h])
            kd = (k.astype(F32) * kd_ref[h]).astype(BF16)
            s_scr[h] = s * cd_ref[h] + lax.dot_general(kd, v, (((0,), (0,)), ((), ())),
                                                       preferred_element_type=F32)
            o_ref[rows, vv] = _gated_groupnorm(o, gn_ref[:, vv], g_ref[rows, vv])

    @pl.when(t == pl.num_programs(1) - 1)
    def _():
        sfin_ref[0] = s_scr[...]


def _retention_prompt(h_act, tabs, gn, *, batch, seq):
    tr = 512
    n_t = seq // tr
    m = batch * seq
    row = lambda b, t: b * n_t + t
    const3 = lambda b, t: (0, 0, 0)
    return pl.pallas_call(
        functools.partial(_ret_prompt_kernel, n_chunks=tr // R_CHUNK),
        grid=(batch, n_t),
        in_specs=[
            pl.BlockSpec(memory_space=pltpu.SMEM),
            pl.BlockSpec((tr, R_QK_WIDTH), lambda b, t: (row(b, t), OFF_RQ // R_QK_WIDTH)),
            pl.BlockSpec((tr, R_QK_WIDTH), lambda b, t: (row(b, t), OFF_RK // R_QK_WIDTH)),
            pl.BlockSpec((tr, R_V_WIDTH), lambda b, t: (row(b, t), OFF_RV // R_V_WIDTH)),
            pl.BlockSpec((tr, R_V_WIDTH), lambda b, t: (row(b, t), OFF_RG // R_V_WIDTH)),
            pl.BlockSpec((R_HEADS, R_CHUNK, R_CHUNK), const3),
            pl.BlockSpec((R_HEADS, R_CHUNK, 1), const3),
            pl.BlockSpec((R_HEADS, R_CHUNK, 1), const3),
            pl.BlockSpec((1, R_V_WIDTH), lambda b, t: (0, 0)),
        ],
        out_specs=[
            pl.BlockSpec((tr, R_V_WIDTH), lambda b, t: (row(b, t), 0)),
            pl.BlockSpec((1, R_HEADS, R_QK_DIM, R_V_DIM), lambda b, t: (b, 0, 0, 0)),
        ],
        out_shape=[
            jax.ShapeDtypeStruct((m, R_V_WIDTH), BF16),
            jax.ShapeDtypeStruct((batch, R_HEADS, R_QK_DIM, R_V_DIM), F32),
        ],
        scratch_shapes=[pltpu.VMEM((R_HEADS, R_QK_DIM, R_V_DIM), F32)],
        compiler_params=_params("arbitrary", "arbitrary"),
        name="retention_prompt",
    )(tabs["chunk_decay"], h_act, h_act, h_act, h_act, tabs["dmask"], tabs["q_decay"], tabs["k_decay"], gn)


def _ret_sample_kernel(cd_ref, q_ref, k_ref, v_ref, g_ref, s_ref, dmask_ref, qd_ref, kd_ref, gn_ref,
                       o_ref, snew_ref, *, bb, seq):
    q_all = q_ref[...].astype(F32)
    k_all = k_ref[...].astype(F32)
    v_all = v_ref[...].astype(F32)
    g_all = g_ref[...].astype(F32)
    out_rows = []
    for b in range(bb):
        rows = slice(b * seq, (b + 1) * seq)
        out_heads = []
        for h in range(R_HEADS):
            qk = slice(h * R_QK_DIM, (h + 1) * R_QK_DIM)
            vv = slice(h * R_V_DIM, (h + 1) * R_V_DIM)
            q = q_all[rows, qk].astype(BF16)
            kf = k_all[rows, qk]
            v = v_all[rows, vv].astype(BF16)
            s = s_ref[b, h]
            sc = lax.dot_general(q, kf.astype(BF16), (((1,), (1,)), ((), ())),
                                 preferred_element_type=F32) * dmask_ref[h]
            o = (jnp.dot(sc.astype(BF16), v, preferred_element_type=F32)
                 + jnp.dot(q, s.astype(BF16), preferred_element_type=F32) * qd_ref[h])
            kd = (kf * kd_ref[h]).astype(BF16)
            snew_ref[b, h] = s * cd_ref[h] + lax.dot_general(kd, v, (((0,), (0,)), ((), ())),
                                                             preferred_element_type=F32)
            out_heads.append(g_all[rows, vv] * (_standardize(o) * gn_ref[:, vv]))
        out_rows.append(jnp.concatenate(out_heads, axis=1))
    o_ref[...] = jnp.concatenate(out_rows, axis=0).astype(BF16)


def _retention_sample(h_act, state, tabs, gn, *, batch, seq):
    bb = 4
    m = batch * seq
    tr = bb * seq
    const3 = lambda i: (0, 0, 0)
    return pl.pallas_call(
        functools.partial(_ret_sample_kernel, bb=bb, seq=seq),
        grid=(batch // bb,),
        in_specs=[
            pl.BlockSpec(memory_space=pltpu.SMEM),
            pl.BlockSpec((tr, R_QK_WIDTH), lambda i: (i, OFF_RQ // R_QK_WIDTH)),
            pl.BlockSpec((tr, R_QK_WIDTH), lambda i: (i, OFF_RK // R_QK_WIDTH)),
            pl.BlockSpec((tr, R_V_WIDTH), lambda i: (i, OFF_RV // R_V_WIDTH)),
            pl.BlockSpec((tr, R_V_WIDTH), lambda i: (i, OFF_RG // R_V_WIDTH)),
            pl.BlockSpec((bb, R_HEADS, R_QK_DIM, R_V_DIM), lambda i: (i, 0, 0, 0)),
            pl.BlockSpec((R_HEADS, seq, seq), const3),
            pl.BlockSpec((R_HEADS, seq, 1), const3),
            pl.BlockSpec((R_HEADS, seq, 1), const3),
            pl.BlockSpec((1, R_V_WIDTH), lambda i: (0, 0)),
        ],
        out_specs=[
            pl.BlockSpec((tr, R_V_WIDTH), lambda i: (i, 0)),
            pl.BlockSpec((bb, R_HEADS, R_QK_DIM, R_V_DIM), lambda i: (i, 0, 0, 0)),
        ],
        out_shape=[
            jax.ShapeDtypeStruct((m, R_V_WIDTH), BF16),
            jax.ShapeDtypeStruct((batch, R_HEADS, R_QK_DIM, R_V_DIM), F32),
        ],
        compiler_params=_params("arbitrary"),
        name="retention_sample",
    )(tabs["chunk_decay"], h_act, h_act, h_act, h_act, state, tabs["dmask"], tabs["q_decay"],
      tabs["k_decay"], gn)


def _retention_tables(c):
    log_g = jnp.log1p(-jnp.exp2(-5.0 - jnp.arange(R_HEADS, dtype=F32)))
    idx = jnp.arange(c, dtype=F32)
    diff = idx[:, None] - idx[None, :]
    causal = diff >= 0
    dmask = jnp.where(causal[None], jnp.exp(jnp.where(causal, diff, 0.0)[None] * log_g[:, None, None]), 0.0)
    q_decay = jnp.exp((idx[:, None] + 1.0) * log_g[None, :])
    k_decay = jnp.exp((c - 1.0 - idx)[:, None] * log_g[None, :])
    chunk_decay = jnp.exp(c * log_g)
    return {
        "dmask": dmask,
        "q_decay": q_decay.T[:, :, None],
        "k_decay": k_decay.T[:, :, None],
        "chunk_decay": chunk_decay,
    }


def _softmax_rows(s):
    e = jnp.exp(s - jnp.max(s, axis=-1, keepdims=True))
    return e, jnp.sum(e, axis=-1, keepdims=True)


def _mem_prompt_kernel(q_ref, mk_ref, mv_ref, o_ref):
    for h in range(M_HEADS):
        cols = slice(h * M_HEAD_DIM, (h + 1) * M_HEAD_DIM)
        s = lax.dot_general(q_ref[:, cols], mk_ref[:, cols], (((1,), (1,)), ((), ())),
                            preferred_element_type=F32)
        e, l = _softmax_rows(s)
        o = jnp.dot(e.astype(BF16), mv_ref[:, cols], preferred_element_type=F32) / l
        o_ref[:, cols] = o.astype(BF16)


def _mem_attn_prompt(h_act, mk, mv, *, batch, seq):
    tq = 512
    n_t = seq // tq
    return pl.pallas_call(
        _mem_prompt_kernel,
        grid=(batch, n_t),
        in_specs=[
            pl.BlockSpec((tq, M_WIDTH), lambda b, t: (b * n_t + t, OFF_MQ // M_WIDTH)),
            pl.BlockSpec((N_MEM, M_WIDTH), lambda b, t: (b, 0)),
            pl.BlockSpec((N_MEM, M_WIDTH), lambda b, t: (b, 0)),
        ],
        out_specs=pl.BlockSpec((tq, M_WIDTH), lambda b, t: (b * n_t + t, 0)),
        out_shape=jax.ShapeDtypeStruct((batch * seq, M_WIDTH), BF16),
        compiler_params=_params("arbitrary", "arbitrary"),
        name="mem_attn_prompt",
    )(h_act, mk, mv)


def _mem_sample_kernel(q_ref, mk_ref, mv_ref, o_ref, *, bb, seq):
    q_all = q_ref[...].astype(F32)
    out_rows = []
    for b in range(bb):
        rows = slice(b * seq, (b + 1) * seq)
        out_heads = []
        for h in range(M_HEADS):
            cols = slice(h * M_HEAD_DIM, (h + 1) * M_HEAD_DIM)
            q = q_all[rows, cols].astype(BF16)
            s = lax.dot_general(q, mk_ref[b, :, cols].astype(BF16), (((1,), (1,)), ((), ())),
                                preferred_element_type=F32)
            e, l = _softmax_rows(s)
            out_heads.append(
                jnp.dot(e.astype(BF16), mv_ref[b, :, cols].astype(BF16), preferred_element_type=F32) / l)
        out_rows.append(jnp.concatenate(out_heads, axis=1))
    o_ref[...] = jnp.concatenate(out_rows, axis=0).astype(BF16)


def _mem_attn_sample(h_act, mk, mv, *, batch, seq):
    bb = 4
    tr = bb * seq
    return pl.pallas_call(
        functools.partial(_mem_sample_kernel, bb=bb, seq=seq),
        grid=(batch // bb,),
        in_specs=[
            pl.BlockSpec((tr, M_WIDTH), lambda i: (i, OFF_MQ // M_WIDTH)),
            pl.BlockSpec((bb, N_MEM, M_WIDTH), lambda i: (i, 0, 0)),
            pl.BlockSpec((bb, N_MEM, M_WIDTH), lambda i: (i, 0, 0)),
        ],
        out_specs=pl.BlockSpec((tr, M_WIDTH), lambda i: (i, 0)),
        out_shape=jax.ShapeDtypeStruct((batch * seq, M_WIDTH), BF16),
        compiler_params=_params("arbitrary"),
        name="mem_attn_sample",
    )(h_act, mk, mv)


def _merge_kernel(ga_ref, gb_ref, gc_ref, u_ref, va_ref, sw_ref, sb_ref, b_ref, c_ref,
                  wa_ref, wb_ref, wc_ref, o_ref, a_scr, *, tm):
    @pl.when(pl.program_id(1) == 0)
    def _():
        for r in range(tm // CHUNK):
            rows = slice(r * CHUNK, (r + 1) * CHUNK)
            for g in range(A_GROUPS):
                cols = slice(g * A_GROUP_DIM, (g + 1) * A_GROUP_DIM)
                z = jnp.dot(sw_ref[g], va_ref[rows, cols], preferred_element_type=F32) + sb_ref[g]
                a_scr[rows, cols] = (u_ref[rows, cols].astype(F32) * z).astype(BF16)

    ya = jnp.dot(a_scr[...], wa_ref[...], preferred_element_type=F32)
    yb = jnp.dot(b_ref[...], wb_ref[...], preferred_element_type=F32)
    yc = jnp.dot(c_ref[...], wc_ref[...], preferred_element_type=F32)
    merged = (ga_ref[...].astype(F32) * ya + gb_ref[...].astype(F32) * yb
              + gc_ref[...].astype(F32) * yc)
    o_ref[...] = merged.astype(BF16)


def _merge(h_act, va, b_act, c_act, sgu_w, sgu_b, wa, wb, wc, *, va_col_block):
    m = h_act.shape[0]
    tm, tn = 512, 512
    n_j = D_MODEL // tn
    return pl.pallas_call(
        functools.partial(_merge_kernel, tm=tm),
        grid=(m // tm, n_j),
        in_specs=[
            pl.BlockSpec((tm, tn), lambda i, j: (i, j)),
            pl.BlockSpec((tm, tn), lambda i, j: (i, n_j + j)),
            pl.BlockSpec((tm, tn), lambda i, j: (i, 2 * n_j + j)),
            pl.BlockSpec((tm, A_WIDTH), lambda i, j: (i, OFF_AU // A_WIDTH)),
            pl.BlockSpec((tm, A_WIDTH), lambda i, j: (i, va_col_block)),
            pl.BlockSpec((A_GROUPS, CHUNK, CHUNK), lambda i, j: (0, 0, 0)),
            pl.BlockSpec((A_GROUPS, CHUNK, 1), lambda i, j: (0, 0, 0)),
            pl.BlockSpec((tm, R_V_WIDTH), lambda i, j: (i, 0)),
            pl.BlockSpec((tm, M_WIDTH), lambda i, j: (i, 0)),
            pl.BlockSpec((A_WIDTH, tn), lambda i, j: (0, j)),
            pl.BlockSpec((R_V_WIDTH, tn), lambda i, j: (0, j)),
            pl.BlockSpec((M_WIDTH, tn), lambda i, j: (0, j)),
        ],
        out_specs=pl.BlockSpec((tm, tn), lambda i, j: (i, j)),
        out_shape=jax.ShapeDtypeStruct((m, D_MODEL), BF16),
        scratch_shapes=[pltpu.VMEM((tm, A_WIDTH), BF16)],
        compiler_params=_params("arbitrary", "arbitrary"),
        name="merge",
    )(h_act, h_act, h_act, h_act, va, sgu_w, sgu_b, b_act, c_act, wa, wb, wc)


def _out_ln_kernel(m_ref, w_ref, x_ref, g_ref, b_ref, o_ref):
    y = jnp.dot(m_ref[...], w_ref[...], preferred_element_type=F32)
    o_ref[...] = _standardize(ALPHA * x_ref[...] + y) * g_ref[...] + b_ref[...]


def _out_ln(merged, w_out, x, ln_g, ln_b):
    m = x.shape[0]
    tm = 512
    return pl.pallas_call(
        _out_ln_kernel,
        grid=(m // tm,),
        in_specs=[
            pl.BlockSpec((tm, D_MODEL), lambda i: (i, 0)),
            pl.BlockSpec((D_MODEL, D_MODEL), lambda i: (0, 0)),
            pl.BlockSpec((tm, D_MODEL), lambda i: (i, 0)),
            pl.BlockSpec((1, D_MODEL), lambda i: (0, 0)),
            pl.BlockSpec((1, D_MODEL), lambda i: (0, 0)),
        ],
        out_specs=pl.BlockSpec((tm, D_MODEL), lambda i: (i, 0)),
        out_shape=jax.ShapeDtypeStruct((m, D_MODEL), F32),
        compiler_params=_params("arbitrary"),
        name="out_ln",
    )(merged, w_out, x, ln_g, ln_b)


def _ffn_kernel(x_ref, wg_ref, wu_ref, wd_ref, g_ref, b_ref, o_ref, xb_ref, acc_ref):
    f = pl.program_id(1)

    @pl.when(f == 0)
    def _():
        xb_ref[...] = x_ref[...].astype(BF16)
        acc_ref[...] = jnp.zeros_like(acc_ref)

    xb = xb_ref[...]
    gate = jnp.dot(xb, wg_ref[...], preferred_element_type=F32)
    up = jnp.dot(xb, wu_ref[...], preferred_element_type=F32)
    act = (jax.nn.silu(gate) * up).astype(BF16)
    acc_ref[...] += jnp.dot(act, wd_ref[...], preferred_element_type=F32)

    @pl.when(f == pl.num_programs(1) - 1)
    def _():
        o_ref[...] = _standardize(ALPHA * x_ref[...] + acc_ref[...]) * g_ref[...] + b_ref[...]


def _ffn(x1, wg, wu, wd, ln_g, ln_b):
    m = x1.shape[0]
    tm, tf = 512, 512
    return pl.pallas_call(
        _ffn_kernel,
        grid=(m // tm, D_FF // tf),
        in_specs=[
            pl.BlockSpec((tm, D_MODEL), lambda i, f: (i, 0)),
            pl.BlockSpec((D_MODEL, tf), lambda i, f: (0, f)),
            pl.BlockSpec((D_MODEL, tf), lambda i, f: (0, f)),
            pl.BlockSpec((tf, D_MODEL), lambda i, f: (f, 0)),
            pl.BlockSpec((1, D_MODEL), lambda i, f: (0, 0)),
            pl.BlockSpec((1, D_MODEL), lambda i, f: (0, 0)),
        ],
        out_specs=pl.BlockSpec((tm, D_MODEL), lambda i, f: (i, 0)),
        out_shape=jax.ShapeDtypeStruct((m, D_MODEL), F32),
        scratch_shapes=[pltpu.VMEM((tm, D_MODEL), BF16), pltpu.VMEM((tm, D_MODEL), F32)],
        compiler_params=_params("arbitrary", "arbitrary"),
        name="ffn",
    )(x1, wg, wu, wd, ln_g, ln_b)


def _rotary_tables(pos):
    half = R_QK_DIM // 2
    inv = ROPE_BASE ** (-jnp.arange(half, dtype=F32) / half)
    ang = pos[:, None] * inv[None, :]
    cos = jnp.cos(ang)
    sin = jnp.sin(ang)
    return jnp.concatenate([cos, cos], axis=-1), jnp.concatenate([-sin, sin], axis=-1)


def _sgu_tables(sgu_w, sgu_b, c):
    w = jnp.tril(sgu_w[:, :c, :c])
    b = sgu_b[:, :c]
    reps = CHUNK // c
    if reps > 1:
        blk = jnp.arange(CHUNK) // c
        w = jnp.where((blk[:, None] == blk[None, :])[None], jnp.tile(w, (1, reps, reps)), 0.0)
        b = jnp.tile(b, (1, reps))
    return w.astype(BF16), b[:, :, None]


def kernel(x_prompt, x_sample, mem_prompt, state_ret, cache_mem_k, cache_mem_v, w_in, sgu_ln_g, sgu_ln_b, sgu_w, sgu_b, w_proj_a, ret_gn_g, w_proj_b, w_mem_k, w_mem_v, w_proj_c, w_out, ln1_g, ln1_b, w_ffn_gate, w_ffn_up, w_ffn_down, ln2_g, ln2_b):
    assert w_in.shape[0] == DEPTH == 1
    bp, seq, _ = x_prompt.shape
    bs, dseq, _ = x_sample.shape
    l = 0
    bf = lambda w: w[l].astype(BF16)
    w_in_b, wa_b, wb_b, wc_b, wo_b = bf(w_in), bf(w_proj_a), bf(w_proj_b), bf(w_proj_c), bf(w_out)
    wmk_b, wmv_b = bf(w_mem_k), bf(w_mem_v)
    wg_b, wu_b, wd_b = bf(w_ffn_gate), bf(w_ffn_up), bf(w_ffn_down)
    ln_g = sgu_ln_g[l].reshape(1, A_WIDTH)
    ln_b = sgu_ln_b[l].reshape(1, A_WIDTH)
    gn = ret_gn_g[l].reshape(1, R_V_WIDTH)
    ln1g, ln1b = ln1_g[l].reshape(1, D_MODEL), ln1_b[l].reshape(1, D_MODEL)
    ln2g, ln2b = ln2_g[l].reshape(1, D_MODEL), ln2_b[l].reshape(1, D_MODEL)

    tm_in = 1024

    def tail(x2, h_act, va, b_act, c_act, sgu_tabs, va_col_block):
        merged = _merge(h_act, va, b_act, c_act, sgu_tabs[0], sgu_tabs[1], wa_b, wb_b, wc_b,
                        va_col_block=va_col_block)
        x1 = _out_ln(merged, wo_b, x2, ln1g, ln1b)
        return _ffn(x1, wg_b, wu_b, wd_b, ln2g, ln2b)

    xp = x_prompt.reshape(bp * seq, D_MODEL)
    cos_p, sin_p = _rotary_tables(0.0 + jnp.arange(seq, dtype=F32))
    (h_p,) = _in_proj(xp, w_in_b, cos_p, sin_p, ln_g, ln_b, tm=tm_in, with_va=False)
    memp = mem_prompt.reshape(bp * N_MEM, D_MODEL)
    mk_p, mk_pb = _mem_kv(memp, wmk_b)
    mv_p, mv_pb = _mem_kv(memp, wmv_b)
    b_p, s_p = _retention_prompt(h_p, _retention_tables(min(seq, R_CHUNK)), gn, batch=bp, seq=seq)
    c_p = _mem_attn_prompt(h_p, mk_pb, mv_pb, batch=bp, seq=seq)
    y_p = tail(xp, h_p, h_p, b_p, c_p, _sgu_tables(sgu_w[l], sgu_b[l], min(seq, CHUNK)), OFF_AV // A_WIDTH)

    xs = x_sample.reshape(bs * dseq, D_MODEL)
    cos_s, sin_s = _rotary_tables(float(PAST_LEN) + jnp.arange(dseq, dtype=F32))
    reps = (bs * dseq) // dseq
    cos_s, sin_s = jnp.tile(cos_s, (reps, 1)), jnp.tile(sin_s, (reps, 1))
    h_s, va_s = _in_proj(xs, w_in_b, cos_s, sin_s, ln_g, ln_b, tm=bs * dseq, with_va=True)
    b_s, s_s = _retention_sample(h_s, state_ret[l], _retention_tables(min(dseq, R_CHUNK)), gn,
                                 batch=bs, seq=dseq)
    c_s = _mem_attn_sample(h_s, cache_mem_k[l].reshape(bs, N_MEM, M_WIDTH),
                           cache_mem_v[l].reshape(bs, N_MEM, M_WIDTH), batch=bs, seq=dseq)
    y_s = tail(xs, h_s, h_s, b_s, c_s, _sgu_tables(sgu_w[l], sgu_b[l], min(dseq, CHUNK)), OFF_AV // A_WIDTH)

    return (
        y_p.reshape(bp, seq, D_MODEL),
        y_s.reshape(bs, dseq, D_MODEL),
        s_p[None],
        mk_p.reshape(1, bp, N_MEM, M_HEADS, M_HEAD_DIM),
        mv_p.reshape(1, bp, N_MEM, M_HEADS, M_HEAD_DIM),
        s_s[None],
        va_s.reshape(1, bs, dseq, A_GROUPS, A_GROUP_DIM),
    )
```

```python
import functools
import math

import jax
import jax.numpy as jnp
from jax import lax
from jax.experimental import pallas as pl
from jax.experimental.pallas import tpu as pltpu

F32 = jnp.float32
BF16 = jnp.bfloat16

D_MODEL = 2048
DEPTH = 1
PAST_LEN = 16384
CHUNK = 128
A_GROUPS = 4
A_GROUP_DIM = 256
A_WIDTH = A_GROUPS * A_GROUP_DIM
R_HEADS = 8
R_QK_DIM = 128
R_V_DIM = 256
R_QK_WIDTH = R_HEADS * R_QK_DIM
R_V_WIDTH = R_HEADS * R_V_DIM
R_CHUNK = 128
N_MEM = 256
M_HEADS = 4
M_HEAD_DIM = 256
M_WIDTH = M_HEADS * M_HEAD_DIM
N_BRANCH = 3
D_FF = -(-8 * D_MODEL // (3 * 256)) * 256
ALPHA = (2.0 * DEPTH) ** 0.25
ROPE_BASE = 10000.0
LN_EPS = 1e-5

OFF_AU = N_BRANCH * D_MODEL
OFF_AV = OFF_AU + A_WIDTH
OFF_RQ = OFF_AV + A_WIDTH
OFF_RK = OFF_RQ + R_QK_WIDTH
OFF_RV = OFF_RK + R_QK_WIDTH
OFF_RG = OFF_RV + R_V_WIDTH
OFF_MQ = OFF_RG + R_V_WIDTH
IN_WIDTH = OFF_MQ + M_WIDTH

VMEM_LIMIT_BYTES = 56 * 1024 * 1024
LANES = 128


def _params(*sem):
    return pltpu.CompilerParams(dimension_semantics=sem, vmem_limit_bytes=VMEM_LIMIT_BYTES)


def _standardize(x):
    mu = jnp.mean(x, axis=-1, keepdims=True)
    xc = x - mu
    var = jnp.mean(xc * xc, axis=-1, keepdims=True)
    return xc * lax.rsqrt(var + LN_EPS)


IN_TN = 512
IN_ROW_CHUNK = 256
_J_AU = OFF_AU // IN_TN
_J_AV = OFF_AV // IN_TN
_J_RQ = OFF_RQ // IN_TN
_J_RK = OFF_RK // IN_TN
_J_RV = OFF_RV // IN_TN
_J_RG = OFF_RG // IN_TN
_J_MQ = OFF_MQ // IN_TN
_J_END = IN_WIDTH // IN_TN


def _in_proj_kernel(x_ref, w_ref, cos_ref, sin_ref, lng_ref, lnb_ref, o_ref, *rest, with_va):
    tm = x_ref.shape[0]
    if with_va:
        va_ref, xb_ref = rest
    else:
        (xb_ref,) = rest
    j = pl.program_id(1)

    @pl.when(j == 0)
    def _():
        xb_ref[...] = x_ref[...].astype(BF16)

    def column_range(lo, hi, epilogue):
        @pl.when((j >= lo) & (j < hi))
        def _():
            for r in range(tm // IN_ROW_CHUNK):
                rows = slice(r * IN_ROW_CHUNK, (r + 1) * IN_ROW_CHUNK)
                acc = jnp.dot(xb_ref[rows, :], w_ref[...], preferred_element_type=F32)
                epilogue(rows, acc)

    def store(fn):
        def epilogue(rows, acc):
            o_ref[rows, :] = fn(acc).astype(BF16)
        return epilogue

    def gelu_layernorm(rows, acc):
        v = jax.nn.gelu(acc)
        for g in range(IN_TN // A_GROUP_DIM):
            cols = slice(g * A_GROUP_DIM, (g + 1) * A_GROUP_DIM)
            vn = _standardize(v[:, cols]) * lng_ref[:, cols] + lnb_ref[:, cols]
            o_ref[rows, cols] = vn.astype(BF16)
            if with_va:
                va_ref[rows, cols] = vn

    def rotary(scale):
        def epilogue(rows, acc):
            for hh in range(IN_TN // R_QK_DIM):
                cols = slice(hh * R_QK_DIM, (hh + 1) * R_QK_DIM)
                xh = acc[:, cols]
                r = xh * cos_ref[rows, :] + pltpu.roll(xh, R_QK_DIM // 2, 1) * sin_ref[rows, :]
                if scale != 1.0:
                    r = r * scale
                o_ref[rows, cols] = r.astype(BF16)
        return epilogue

    column_range(0, _J_AU, store(jax.nn.sigmoid))
    column_range(_J_AU, _J_AV, store(jax.nn.gelu))
    column_range(_J_AV, _J_RQ, gelu_layernorm)
    column_range(_J_RQ, _J_RK, rotary(1.0))
    column_range(_J_RK, _J_RV, rotary(R_QK_DIM ** -0.5))
    column_range(_J_RV, _J_RG, store(lambda acc: acc))
    column_range(_J_RG, _J_MQ, store(jax.nn.silu))
    column_range(_J_MQ, _J_END, store(lambda acc: acc * (M_HEAD_DIM ** -0.5)))


def _in_proj(x, w_in, cos_t, sin_t, ln_g, ln_b, *, tm, with_va):
    m = x.shape[0]
    n_pos_blocks = cos_t.shape[0] // tm
    n_av = A_WIDTH // IN_TN

    def av_idx(j):
        return jnp.clip(j - _J_AV, 0, n_av - 1)

    out_shape = [jax.ShapeDtypeStruct((m, IN_WIDTH), BF16)]
    out_specs = [pl.BlockSpec((tm, IN_TN), lambda i, j: (i, j))]
    if with_va:
        out_shape.append(jax.ShapeDtypeStruct((m, A_WIDTH), F32))
        out_specs.append(pl.BlockSpec((tm, IN_TN), lambda i, j: (i, av_idx(j))))
    return pl.pallas_call(
        functools.partial(_in_proj_kernel, with_va=with_va),
        grid=(m // tm, _J_END),
        in_specs=[
            pl.BlockSpec((tm, D_MODEL), lambda i, j: (i, 0)),
            pl.BlockSpec((D_MODEL, IN_TN), lambda i, j: (0, j)),
            pl.BlockSpec((tm, R_QK_DIM), lambda i, j: (i % n_pos_blocks, 0)),
            pl.BlockSpec((tm, R_QK_DIM), lambda i, j: (i % n_pos_blocks, 0)),
            pl.BlockSpec((1, IN_TN), lambda i, j: (0, av_idx(j))),
            pl.BlockSpec((1, IN_TN), lambda i, j: (0, av_idx(j))),
        ],
        out_specs=out_specs,
        out_shape=out_shape,
        scratch_shapes=[pltpu.VMEM((tm, D_MODEL), BF16)],
        compiler_params=_params("arbitrary", "arbitrary"),
        name="in_proj",
    )(x, w_in, cos_t, sin_t, ln_g, ln_b)


def _matmul_kernel(x_ref, w_ref, o_ref, ob_ref):
    acc = jnp.dot(x_ref[...].astype(BF16), w_ref[...], preferred_element_type=F32)
    o_ref[...] = acc
    ob_ref[...] = acc.astype(BF16)


def _mem_kv(x, w):
    m, k = x.shape
    n = w.shape[1]
    tn = 512
    return pl.pallas_call(
        _matmul_kernel,
        grid=(n // tn,),
        in_specs=[pl.BlockSpec((m, k), lambda j: (0, 0)), pl.BlockSpec((k, tn), lambda j: (0, j))],
        out_specs=[pl.BlockSpec((m, tn), lambda j: (0, j)), pl.BlockSpec((m, tn), lambda j: (0, j))],
        out_shape=[jax.ShapeDtypeStruct((m, n), F32), jax.ShapeDtypeStruct((m, n), BF16)],
        compiler_params=_params("arbitrary"),
        name="mem_kv",
    )(x, w)


def _gated_groupnorm(o, gn, g):
    return (g.astype(F32) * (_standardize(o) * gn)).astype(BF16)


def _ret_prompt_kernel(cd_ref, q_ref, k_ref, v_ref, g_ref, dmask_ref, qd_ref, kd_ref, gn_ref,
                       o_ref, sfin_ref, s_scr, *, n_chunks):
    t = pl.program_id(1)

    @pl.when(t == 0)
    def _():
        s_scr[...] = jnp.zeros_like(s_scr)

    for c in range(n_chunks):
        rows = slice(c * R_CHUNK, (c + 1) * R_CHUNK)
        for h in range(R_HEADS):
            qk = slice(h * R_QK_DIM, (h + 1) * R_QK_DIM)
            vv = slice(h * R_V_DIM, (h + 1) * R_V_DIM)
            q = q_ref[rows, qk]
            k = k_ref[rows, qk]
            v = v_ref[rows, vv]
            s = s_scr[h]
            sc = lax.dot_general(q, k, (((1,), (1,)), ((), ())), preferred_element_type=F32) * dmask_ref[h]
            o = (jnp.dot(sc.astype(BF16), v, preferred_element_type=F32)
                 + jnp.dot(q, s.astype(BF16), preferred_element_type=F32) * qd_ref[h])
            kd = (k.astype(F32) * kd_ref[h]).astype(BF16)
            s_scr[h] = s * cd_ref[h] + lax.dot_general(kd, v, (((0,), (0,)), ((), ())),
                                                       preferred_element_type=F32)
            o_ref[rows, vv] = _gated_groupnorm(o, gn_ref[:, vv], g_ref[rows, vv])

    @pl.when(t == pl.num_programs(1) - 1)
    def _():
        sfin_ref[0] = s_scr[...]


def _retention_prompt(h_act, tabs, gn, *, batch, seq):
    tr = 512
    n_t = seq // tr
    m = batch * seq
    row = lambda b, t: b * n_t + t
    const3 = lambda b, t: (0, 0, 0)
    return pl.pallas_call(
        functools.partial(_ret_prompt_kernel, n_chunks=tr // R_CHUNK),
        grid=(batch, n_t),
        in_specs=[
            pl.BlockSpec(memory_space=pltpu.SMEM),
            pl.BlockSpec((tr, R_QK_WIDTH), lambda b, t: (row(b, t), OFF_RQ // R_QK_WIDTH)),
            pl.BlockSpec((tr, R_QK_WIDTH), lambda b, t: (row(b, t), OFF_RK // R_QK_WIDTH)),
            pl.BlockSpec((tr, R_V_WIDTH), lambda b, t: (row(b, t), OFF_RV // R_V_WIDTH)),
            pl.BlockSpec((tr, R_V_WIDTH), lambda b, t: (row(b, t), OFF_RG // R_V_WIDTH)),
            pl.BlockSpec((R_HEADS, R_CHUNK, R_CHUNK), const3),
            pl.BlockSpec((R_HEADS, R_CHUNK, 1), const3),
            pl.BlockSpec((R_HEADS, R_CHUNK, 1), const3),
            pl.BlockSpec((1, R_V_WIDTH), lambda b, t: (0, 0)),
        ],
        out_specs=[
            pl.BlockSpec((tr, R_V_WIDTH), lambda b, t: (row(b, t), 0)),
            pl.BlockSpec((1, R_HEADS, R_QK_DIM, R_V_DIM), lambda b, t: (b, 0, 0, 0)),
        ],
        out_shape=[
            jax.ShapeDtypeStruct((m, R_V_WIDTH), BF16),
            jax.ShapeDtypeStruct((batch, R_HEADS, R_QK_DIM, R_V_DIM), F32),
        ],
        scratch_shapes=[pltpu.VMEM((R_HEADS, R_QK_DIM, R_V_DIM), F32)],
        compiler_params=_params("arbitrary", "arbitrary"),
        name="retention_prompt",
    )(tabs["chunk_decay"], h_act, h_act, h_act, h_act, tabs["dmask"], tabs["q_decay"], tabs["k_decay"], gn)


def _ret_sample_kernel(cd_ref, q_ref, k_ref, v_ref, g_ref, s_ref, dmask_ref, qd_ref, kd_ref, gn_ref,
                       o_ref, snew_ref, *, bb, seq):
    q_all = q_ref[...].astype(F32)
    k_all = k_ref[...].astype(F32)
    v_all = v_ref[...].astype(F32)
    g_all = g_ref[...].astype(F32)
    out_rows = []
    for b in range(bb):
        rows = slice(b * seq, (b + 1) * seq)
        out_heads = []
        for h in range(R_HEADS):
            qk = slice(h * R_QK_DIM, (h + 1) * R_QK_DIM)
            vv = slice(h * R_V_DIM, (h + 1) * R_V_DIM)
            q = q_all[rows, qk].astype(BF16)
            kf = k_all[rows, qk]
            v = v_all[rows, vv].astype(BF16)
            s = s_ref[b, h]
            sc = lax.dot_general(q, kf.astype(BF16), (((1,), (1,)), ((), ())),
                                 preferred_element_type=F32) * dmask_ref[h]
            o = (jnp.dot(sc.astype(BF16), v, preferred_element_type=F32)
                 + jnp.dot(q, s.astype(BF16), preferred_element_type=F32) * qd_ref[h])
            kd = (kf * kd_ref[h]).astype(BF16)
            snew_ref[b, h] = s * cd_ref[h] + lax.dot_general(kd, v, (((0,), (0,)), ((), ())),
                                                             preferred_element_type=F32)
            out_heads.append(g_all[rows, vv] * (_standardize(o) * gn_ref[:, vv]))
        out_rows.append(jnp.concatenate(out_heads, axis=1))
    o_ref[...] = jnp.concatenate(out_rows, axis=0).astype(BF16)


def _retention_sample(h_act, state, tabs, gn, *, batch, seq):
    bb = 4
    m = batch * seq
    tr = bb * seq
    const3 = lambda i: (0, 0, 0)
    return pl.pallas_call(
        functools.partial(_ret_sample_kernel, bb=bb, seq=seq),
        grid=(batch // bb,),
        in_specs=[
            pl.BlockSpec(memory_space=pltpu.SMEM),
            pl.BlockSpec((tr, R_QK_WIDTH), lambda i: (i, OFF_RQ // R_QK_WIDTH)),
            pl.BlockSpec((tr, R_QK_WIDTH), lambda i: (i, OFF_RK // R_QK_WIDTH)),
            pl.BlockSpec((tr, R_V_WIDTH), lambda i: (i, OFF_RV // R_V_WIDTH)),
            pl.BlockSpec((tr, R_V_WIDTH), lambda i: (i, OFF_RG // R_V_WIDTH)),
            pl.BlockSpec((bb, R_HEADS, R_QK_DIM, R_V_DIM), lambda i: (i, 0, 0, 0)),
            pl.BlockSpec((R_HEADS, seq, seq), const3),
            pl.BlockSpec((R_HEADS, seq, 1), const3),
            pl.BlockSpec((R_HEADS, seq, 1), const3),
            pl.BlockSpec((1, R_V_WIDTH), lambda i: (0, 0)),
        ],
        out_specs=[
            pl.BlockSpec((tr, R_V_WIDTH), lambda i: (i, 0)),
            pl.BlockSpec((bb, R_HEADS, R_QK_DIM, R_V_DIM), lambda i: (i, 0, 0, 0)),
        ],
        out_shape=[
            jax.ShapeDtypeStruct((m, R_V_WIDTH), BF16),
            jax.ShapeDtypeStruct((batch, R_HEADS, R_QK_DIM, R_V_DIM), F32),
        ],
        compiler_params=_params("arbitrary"),
        name="retention_sample",
    )(tabs["chunk_decay"], h_act, h_act, h_act, h_act, state, tabs["dmask"], tabs["q_decay"],
      tabs["k_decay"], gn)


def _retention_tables(c):
    log_g = jnp.log1p(-jnp.exp2(-5.0 - jnp.arange(R_HEADS, dtype=F32)))
    idx = jnp.arange(c, dtype=F32)
    diff = idx[:, None] - idx[None, :]
    causal = diff >= 0
    dmask = jnp.where(causal[None], jnp.exp(jnp.where(causal, diff, 0.0)[None] * log_g[:, None, None]), 0.0)
    q_decay = jnp.exp((idx[:, None] + 1.0) * log_g[None, :])
    k_decay = jnp.exp((c - 1.0 - idx)[:, None] * log_g[None, :])
    chunk_decay = jnp.exp(c * log_g)
    return {
        "dmask": dmask,
        "q_decay": q_decay.T[:, :, None],
        "k_decay": k_decay.T[:, :, None],
        "chunk_decay": chunk_decay,
    }


def _softmax_rows(s):
    e = jnp.exp(s - jnp.max(s, axis=-1, keepdims=True))
    return e, jnp.sum(e, axis=-1, keepdims=True)


def _mem_prompt_kernel(q_ref, mk_ref, mv_ref, o_ref):
    for h in range(M_HEADS):
        cols = slice(h * M_HEAD_DIM, (h + 1) * M_HEAD_DIM)
        s = lax.dot_general(q_ref[:, cols], mk_ref[:, cols], (((1,), (1,)), ((), ())),
                            preferred_element_type=F32)
        e, l = _softmax_rows(s)
        o = jnp.dot(e.astype(BF16), mv_ref[:, cols], preferred_element_type=F32) / l
        o_ref[:, cols] = o.astype(BF16)


def _mem_attn_prompt(h_act, mk, mv, *, batch, seq):
    tq = 512
    n_t = seq // tq
    return pl.pallas_call(
        _mem_prompt_kernel,
        grid=(batch, n_t),
        in_specs=[
            pl.BlockSpec((tq, M_WIDTH), lambda b, t: (b * n_t + t, OFF_MQ // M_WIDTH)),
            pl.BlockSpec((N_MEM, M_WIDTH), lambda b, t: (b, 0)),
            pl.BlockSpec((N_MEM, M_WIDTH), lambda b, t: (b, 0)),
        ],
        out_specs=pl.BlockSpec((tq, M_WIDTH), lambda b, t: (b * n_t + t, 0)),
        out_shape=jax.ShapeDtypeStruct((batch * seq, M_WIDTH), BF16),
        compiler_params=_params("arbitrary", "arbitrary"),
        name="mem_attn_prompt",
    )(h_act, mk, mv)


def _mem_sample_kernel(q_ref, mk_hbm, mv_hbm, o_ref, kbuf, vbuf, sem, *, bb, seq):
    i = pl.program_id(0)
    n_steps = pl.num_programs(0)
    pairs = [(b, h) for b in range(bb) for h in range(M_HEADS)]

    def copies(step, slot):
        out = []
        for b, h in pairs:
            out.append(pltpu.make_async_copy(mk_hbm.at[step * bb + b, :, h, :], kbuf.at[slot, b, h],
                                             sem.at[0, slot]))
            out.append(pltpu.make_async_copy(mv_hbm.at[step * bb + b, :, h, :], vbuf.at[slot, b, h],
                                             sem.at[1, slot]))
        return out

    slot = i % 2

    @pl.when(i == 0)
    def _():
        for c in copies(0, 0):
            c.start()

    @pl.when(i + 1 < n_steps)
    def _():
        for c in copies(i + 1, 1 - slot):
            c.start()

    for c in copies(i, slot):
        c.wait()

    q_all = q_ref[...].astype(F32)
    scores = []
    for b, h in pairs:
        q = q_all[b * seq:(b + 1) * seq, h * M_HEAD_DIM:(h + 1) * M_HEAD_DIM].astype(BF16)
        scores.append(lax.dot_general(q, kbuf[slot, b, h].astype(BF16), (((1,), (1,)), ((), ())),
                                      preferred_element_type=F32))
    e, l = _softmax_rows(jnp.concatenate(scores, axis=0))
    inv_l = 1.0 / l
    outs = []
    for n, (b, h) in enumerate(pairs):
        rows = slice(n * seq, (n + 1) * seq)
        o = jnp.dot(e[rows].astype(BF16), vbuf[slot, b, h].astype(BF16), preferred_element_type=F32)
        outs.append(o * inv_l[rows])
    out_rows = [jnp.concatenate(outs[b * M_HEADS:(b + 1) * M_HEADS], axis=1) for b in range(bb)]
    o_ref[...] = jnp.concatenate(out_rows, axis=0).astype(BF16)


def _mem_attn_sample(h_act, mk, mv, *, batch, seq):
    bb = 4
    tr = bb * seq
    return pl.pallas_call(
        functools.partial(_mem_sample_kernel, bb=bb, seq=seq),
        grid=(batch // bb,),
        in_specs=[
            pl.BlockSpec((tr, M_WIDTH), lambda i: (i, OFF_MQ // M_WIDTH)),
            pl.BlockSpec(memory_space=pl.ANY),
            pl.BlockSpec(memory_space=pl.ANY),
        ],
        out_specs=pl.BlockSpec((tr, M_WIDTH), lambda i: (i, 0)),
        out_shape=jax.ShapeDtypeStruct((batch * seq, M_WIDTH), BF16),
        scratch_shapes=[
            pltpu.VMEM((2, bb, M_HEADS, N_MEM, M_HEAD_DIM), F32),
            pltpu.VMEM((2, bb, M_HEADS, N_MEM, M_HEAD_DIM), F32),
            pltpu.SemaphoreType.DMA((2, 2)),
        ],
        compiler_params=_params("arbitrary"),
        name="mem_attn_sample",
    )(h_act, mk, mv)


def _merge_kernel(ga_ref, gb_ref, gc_ref, u_ref, va_ref, sw_ref, sb_ref, b_ref, c_ref,
                  wa_ref, wb_ref, wc_ref, o_ref, a_scr, *, tm):
    @pl.when(pl.program_id(1) == 0)
    def _():
        for r in range(tm // CHUNK):
            rows = slice(r * CHUNK, (r + 1) * CHUNK)
            for g in range(A_GROUPS):
                cols = slice(g * A_GROUP_DIM, (g + 1) * A_GROUP_DIM)
                z = jnp.dot(sw_ref[g], va_ref[rows, cols], preferred_element_type=F32) + sb_ref[g]
                a_scr[rows, cols] = (u_ref[rows, cols].astype(F32) * z).astype(BF16)

    ya = jnp.dot(a_scr[...], wa_ref[...], preferred_element_type=F32)
    yb = jnp.dot(b_ref[...], wb_ref[...], preferred_element_type=F32)
    yc = jnp.dot(c_ref[...], wc_ref[...], preferred_element_type=F32)
    merged = (ga_ref[...].astype(F32) * ya + gb_ref[...].astype(F32) * yb
              + gc_ref[...].astype(F32) * yc)
    o_ref[...] = merged.astype(BF16)


def _merge(h_act, va, b_act, c_act, sgu_w, sgu_b, wa, wb, wc, *, va_col_block):
    m = h_act.shape[0]
    tm, tn = 1024, 512
    n_j = D_MODEL // tn
    return pl.pallas_call(
        functools.partial(_merge_kernel, tm=tm),
        grid=(m // tm, n_j),
        in_specs=[
            pl.BlockSpec((tm, tn), lambda i, j: (i, j)),
            pl.BlockSpec((tm, tn), lambda i, j: (i, n_j + j)),
            pl.BlockSpec((tm, tn), lambda i, j: (i, 2 * n_j + j)),
            pl.BlockSpec((tm, A_WIDTH), lambda i, j: (i, OFF_AU // A_WIDTH)),
            pl.BlockSpec((tm, A_WIDTH), lambda i, j: (i, va_col_block)),
            pl.BlockSpec((A_GROUPS, CHUNK, CHUNK), lambda i, j: (0, 0, 0)),
            pl.BlockSpec((A_GROUPS, CHUNK, 1), lambda i, j: (0, 0, 0)),
            pl.BlockSpec((tm, R_V_WIDTH), lambda i, j: (i, 0)),
            pl.BlockSpec((tm, M_WIDTH), lambda i, j: (i, 0)),
            pl.BlockSpec((A_WIDTH, tn), lambda i, j: (0, j)),
            pl.BlockSpec((R_V_WIDTH, tn), lambda i, j: (0, j)),
            pl.BlockSpec((M_WIDTH, tn), lambda i, j: (0, j)),
        ],
        out_specs=pl.BlockSpec((tm, tn), lambda i, j: (i, j)),
        out_shape=jax.ShapeDtypeStruct((m, D_MODEL), BF16),
        scratch_shapes=[pltpu.VMEM((tm, A_WIDTH), BF16)],
        compiler_params=_params("arbitrary", "arbitrary"),
        name="merge",
    )(h_act, h_act, h_act, h_act, va, sgu_w, sgu_b, b_act, c_act, wa, wb, wc)


def _out_ln_kernel(m_ref, w_ref, x_ref, g_ref, b_ref, o_ref):
    y = jnp.dot(m_ref[...], w_ref[...], preferred_element_type=F32)
    o_ref[...] = _standardize(ALPHA * x_ref[...] + y) * g_ref[...] + b_ref[...]


def _out_ln(merged, w_out, x, ln_g, ln_b):
    m = x.shape[0]
    tm = 512
    return pl.pallas_call(
        _out_ln_kernel,
        grid=(m // tm,),
        in_specs=[
            pl.BlockSpec((tm, D_MODEL), lambda i: (i, 0)),
            pl.BlockSpec((D_MODEL, D_MODEL), lambda i: (0, 0)),
            pl.BlockSpec((tm, D_MODEL), lambda i: (i, 0)),
            pl.BlockSpec((1, D_MODEL), lambda i: (0, 0)),
            pl.BlockSpec((1, D_MODEL), lambda i: (0, 0)),
        ],
        out_specs=pl.BlockSpec((tm, D_MODEL), lambda i: (i, 0)),
        out_shape=jax.ShapeDtypeStruct((m, D_MODEL), F32),
        compiler_params=_params("arbitrary"),
        name="out_ln",
    )(merged, w_out, x, ln_g, ln_b)


def _ffn_kernel(x_ref, wg_ref, wu_ref, wd_ref, g_ref, b_ref, o_ref, xb_ref, acc_ref):
    f = pl.program_id(1)

    @pl.when(f == 0)
    def _():
        xb_ref[...] = x_ref[...].astype(BF16)
        acc_ref[...] = jnp.zeros_like(acc_ref)

    xb = xb_ref[...]
    gate = jnp.dot(xb, wg_ref[...], preferred_element_type=F32)
    up = jnp.dot(xb, wu_ref[...], preferred_element_type=F32)
    act = (jax.nn.silu(gate) * up).astype(BF16)
    acc_ref[...] += jnp.dot(act, wd_ref[...], preferred_element_type=F32)

    @pl.when(f == pl.num_programs(1) - 1)
    def _():
        o_ref[...] = _standardize(ALPHA * x_ref[...] + acc_ref[...]) * g_ref[...] + b_ref[...]


def _ffn(x1, wg, wu, wd, ln_g, ln_b):
    m = x1.shape[0]
    tm, tf = 512, 512
    return pl.pallas_call(
        _ffn_kernel,
        grid=(m // tm, D_FF // tf),
        in_specs=[
            pl.BlockSpec((tm, D_MODEL), lambda i, f: (i, 0)),
            pl.BlockSpec((D_MODEL, tf), lambda i, f: (0, f)),
            pl.BlockSpec((D_MODEL, tf), lambda i, f: (0, f)),
            pl.BlockSpec((tf, D_MODEL), lambda i, f: (f, 0)),
            pl.BlockSpec((1, D_MODEL), lambda i, f: (0, 0)),
            pl.BlockSpec((1, D_MODEL), lambda i, f: (0, 0)),
        ],
        out_specs=pl.BlockSpec((tm, D_MODEL), lambda i, f: (i, 0)),
        out_shape=jax.ShapeDtypeStruct((m, D_MODEL), F32),
        scratch_shapes=[pltpu.VMEM((tm, D_MODEL), BF16), pltpu.VMEM((tm, D_MODEL), F32)],
        compiler_params=_params("arbitrary", "arbitrary"),
        name="ffn",
    )(x1, wg, wu, wd, ln_g, ln_b)


def _rotary_tables(pos):
    half = R_QK_DIM // 2
    inv = ROPE_BASE ** (-jnp.arange(half, dtype=F32) / half)
    ang = pos[:, None] * inv[None, :]
    cos = jnp.cos(ang)
    sin = jnp.sin(ang)
    return jnp.concatenate([cos, cos], axis=-1), jnp.concatenate([-sin, sin], axis=-1)


def _sgu_tables(sgu_w, sgu_b, c):
    w = jnp.tril(sgu_w[:, :c, :c])
    b = sgu_b[:, :c]
    reps = CHUNK // c
    if reps > 1:
        blk = jnp.arange(CHUNK) // c
        w = jnp.where((blk[:, None] == blk[None, :])[None], jnp.tile(w, (1, reps, reps)), 0.0)
        b = jnp.tile(b, (1, reps))
    return w.astype(BF16), b[:, :, None]


def kernel(x_prompt, x_sample, mem_prompt, state_ret, cache_mem_k, cache_mem_v, w_in, sgu_ln_g, sgu_ln_b, sgu_w, sgu_b, w_proj_a, ret_gn_g, w_proj_b, w_mem_k, w_mem_v, w_proj_c, w_out, ln1_g, ln1_b, w_ffn_gate, w_ffn_up, w_ffn_down, ln2_g, ln2_b):
    assert w_in.shape[0] == DEPTH == 1
    bp, seq, _ = x_prompt.shape
    bs, dseq, _ = x_sample.shape
    l = 0
    bf = lambda w: w[l].astype(BF16)
    w_in_b, wa_b, wb_b, wc_b, wo_b = bf(w_in), bf(w_proj_a), bf(w_proj_b), bf(w_proj_c), bf(w_out)
    wmk_b, wmv_b = bf(w_mem_k), bf(w_mem_v)
    wg_b, wu_b, wd_b = bf(w_ffn_gate), bf(w_ffn_up), bf(w_ffn_down)
    ln_g = sgu_ln_g[l].reshape(1, A_WIDTH)
    ln_b = sgu_ln_b[l].reshape(1, A_WIDTH)
    gn = ret_gn_g[l].reshape(1, R_V_WIDTH)
    ln1g, ln1b = ln1_g[l].reshape(1, D_MODEL), ln1_b[l].reshape(1, D_MODEL)
    ln2g, ln2b = ln2_g[l].reshape(1, D_MODEL), ln2_b[l].reshape(1, D_MODEL)

    tm_in = 1024

    def tail(x2, h_act, va, b_act, c_act, sgu_tabs, va_col_block):
        merged = _merge(h_act, va, b_act, c_act, sgu_tabs[0], sgu_tabs[1], wa_b, wb_b, wc_b,
                        va_col_block=va_col_block)
        x1 = _out_ln(merged, wo_b, x2, ln1g, ln1b)
        return _ffn(x1, wg_b, wu_b, wd_b, ln2g, ln2b)

    xp = x_prompt.reshape(bp * seq, D_MODEL)
    cos_p, sin_p = _rotary_tables(0.0 + jnp.arange(seq, dtype=F32))
    (h_p,) = _in_proj(xp, w_in_b, cos_p, sin_p, ln_g, ln_b, tm=tm_in, with_va=False)
    memp = mem_prompt.reshape(bp * N_MEM, D_MODEL)
    mk_p, mk_pb = _mem_kv(memp, wmk_b)
    mv_p, mv_pb = _mem_kv(memp, wmv_b)
    b_p, s_p = _retention_prompt(h_p, _retention_tables(min(seq, R_CHUNK)), gn, batch=bp, seq=seq)
    c_p = _mem_attn_prompt(h_p, mk_pb, mv_pb, batch=bp, seq=seq)
    y_p = tail(xp, h_p, h_p, b_p, c_p, _sgu_tables(sgu_w[l], sgu_b[l], min(seq, CHUNK)), OFF_AV // A_WIDTH)

    xs = x_sample.reshape(bs * dseq, D_MODEL)
    cos_s, sin_s = _rotary_tables(float(PAST_LEN) + jnp.arange(dseq, dtype=F32))
    reps = (bs * dseq) // dseq
    cos_s, sin_s = jnp.tile(cos_s, (reps, 1)), jnp.tile(sin_s, (reps, 1))
    h_s, va_s = _in_proj(xs, w_in_b, cos_s, sin_s, ln_g, ln_b, tm=bs * dseq, with_va=True)
    b_s, s_s = _retention_sample(h_s, state_ret[l], _retention_tables(min(dseq, R_CHUNK)), gn,
                                 batch=bs, seq=dseq)
    c_s = _mem_attn_sample(h_s, cache_mem_k[l], cache_mem_v[l], batch=bs, seq=dseq)
    y_s = tail(xs, h_s, h_s, b_s, c_s, _sgu_tables(sgu_w[l], sgu_b[l], min(dseq, CHUNK)), OFF_AV // A_WIDTH)

    return (
        y_p.reshape(bp, seq, D_MODEL),
        y_s.reshape(bs, dseq, D_MODEL),
        s_p[None],
        mk_p.reshape(1, bp, N_MEM, M_HEADS, M_HEAD_DIM),
        mv_p.reshape(1, bp, N_MEM, M_HEADS, M_HEAD_DIM),
        s_s[None],
        va_s.reshape(1, bs, dseq, A_GROUPS, A_GROUP_DIM),
    )
```

```python
import functools
import math

import jax
import jax.numpy as jnp
from jax import lax
from jax.experimental import pallas as pl
from jax.experimental.pallas import tpu as pltpu

F32 = jnp.float32
BF16 = jnp.bfloat16

D_MODEL = 2048
DEPTH = 1
PAST_LEN = 16384
CHUNK = 128
A_GROUPS = 4
A_GROUP_DIM = 256
A_WIDTH = A_GROUPS * A_GROUP_DIM
R_HEADS = 8
R_QK_DIM = 128
R_V_DIM = 256
R_QK_WIDTH = R_HEADS * R_QK_DIM
R_V_WIDTH = R_HEADS * R_V_DIM
R_CHUNK = 128
N_MEM = 256
M_HEADS = 4
M_HEAD_DIM = 256
M_WIDTH = M_HEADS * M_HEAD_DIM
N_BRANCH = 3
D_FF = -(-8 * D_MODEL // (3 * 256)) * 256
ALPHA = (2.0 * DEPTH) ** 0.25
ROPE_BASE = 10000.0
LN_EPS = 1e-5

OFF_AU = N_BRANCH * D_MODEL
OFF_AV = OFF_AU + A_WIDTH
OFF_RQ = OFF_AV + A_WIDTH
OFF_RK = OFF_RQ + R_QK_WIDTH
OFF_RV = OFF_RK + R_QK_WIDTH
OFF_RG = OFF_RV + R_V_WIDTH
OFF_MQ = OFF_RG + R_V_WIDTH
IN_WIDTH = OFF_MQ + M_WIDTH

VMEM_LIMIT_BYTES = 56 * 1024 * 1024
LANES = 128


def _params(*sem):
    return pltpu.CompilerParams(dimension_semantics=sem, vmem_limit_bytes=VMEM_LIMIT_BYTES)


def _standardize(x):
    mu = jnp.mean(x, axis=-1, keepdims=True)
    xc = x - mu
    var = jnp.mean(xc * xc, axis=-1, keepdims=True)
    return xc * lax.rsqrt(var + LN_EPS)


IN_TN = 1024
IN_ROW_CHUNK = 256
_J_AU = OFF_AU // IN_TN
_J_AV = OFF_AV // IN_TN
_J_RQ = OFF_RQ // IN_TN
_J_RK = OFF_RK // IN_TN
_J_RV = OFF_RV // IN_TN
_J_RG = OFF_RG // IN_TN
_J_MQ = OFF_MQ // IN_TN
_J_END = IN_WIDTH // IN_TN


def _in_proj_kernel(x_ref, w_ref, cos_ref, sin_ref, lng_ref, lnb_ref, o_ref, *rest, first_pass):
    tm = x_ref.shape[0]
    if first_pass:
        va_ref, wb_ref, xb_ref = rest
        wb_ref[...] = w_ref[...].astype(BF16)
    else:
        (xb_ref,) = rest
        wb_ref = w_ref
    j = pl.program_id(1)

    @pl.when(j == 0)
    def _():
        xb_ref[...] = x_ref[...].astype(BF16)

    def column_range(pred, epilogue):
        @pl.when(pred)
        def _():
            for r in range(tm // IN_ROW_CHUNK):
                rows = slice(r * IN_ROW_CHUNK, (r + 1) * IN_ROW_CHUNK)
                acc = jnp.dot(xb_ref[rows, :], wb_ref[...], preferred_element_type=F32)
                epilogue(rows, acc)

    def store(fn):
        def epilogue(rows, acc):
            o_ref[rows, :] = fn(acc).astype(BF16)
        return epilogue

    def gelu_layernorm(rows, acc):
        v = jax.nn.gelu(acc)
        for g in range(IN_TN // A_GROUP_DIM):
            cols = slice(g * A_GROUP_DIM, (g + 1) * A_GROUP_DIM)
            vn = _standardize(v[:, cols]) * lng_ref[:, cols] + lnb_ref[:, cols]
            o_ref[rows, cols] = vn.astype(BF16)
            if first_pass:
                va_ref[rows, cols] = vn

    def rotary(rows, acc):
        for hh in range(IN_TN // R_QK_DIM):
            cols = slice(hh * R_QK_DIM, (hh + 1) * R_QK_DIM)
            xh = acc[:, cols]
            r = xh * cos_ref[rows, :] + pltpu.roll(xh, R_QK_DIM // 2, 1) * sin_ref[rows, :]
            o_ref[rows, cols] = r.astype(BF16)

    is_mq = j >= _J_MQ
    raw_scale = jnp.where(is_mq, M_HEAD_DIM ** -0.5, 1.0).astype(F32)
    column_range(j < _J_AU, store(jax.nn.sigmoid))
    column_range((j >= _J_AU) & (j < _J_AV), store(jax.nn.gelu))
    column_range((j >= _J_AV) & (j < _J_RQ), gelu_layernorm)
    column_range((j >= _J_RQ) & (j < _J_RV), rotary)
    column_range(((j >= _J_RV) & (j < _J_RG)) | is_mq, store(lambda acc: acc * raw_scale))
    column_range((j >= _J_RG) & (j < _J_MQ), store(jax.nn.silu))


def _in_proj(x, w_in, cos_t, sin_t, ln_g, ln_b, *, tm, first_pass):
    m = x.shape[0]
    n_pos_blocks = cos_t.shape[1] // tm
    assert A_WIDTH == IN_TN and R_QK_WIDTH == IN_TN
    table_idx = lambda i, j: (jnp.where(j == _J_RK, 1, 0), i % n_pos_blocks, 0)

    out_shape = [jax.ShapeDtypeStruct((m, IN_WIDTH), BF16)]
    out_specs = [pl.BlockSpec((tm, IN_TN), lambda i, j: (i, j))]
    if first_pass:
        assert m == tm
        out_shape += [jax.ShapeDtypeStruct((m, A_WIDTH), F32), jax.ShapeDtypeStruct(w_in.shape, BF16)]
        out_specs += [pl.BlockSpec((tm, A_WIDTH), lambda i, j: (i, 0)),
                      pl.BlockSpec((D_MODEL, IN_TN), lambda i, j: (0, j))]
    return pl.pallas_call(
        functools.partial(_in_proj_kernel, first_pass=first_pass),
        grid=(m // tm, _J_END),
        in_specs=[
            pl.BlockSpec((tm, D_MODEL), lambda i, j: (i, 0),
                         pipeline_mode=pl.Buffered(1) if first_pass else None),
            pl.BlockSpec((D_MODEL, IN_TN), lambda i, j: (0, j)),
            pl.BlockSpec((None, tm, R_QK_DIM), table_idx),
            pl.BlockSpec((None, tm, R_QK_DIM), table_idx),
            pl.BlockSpec((1, A_WIDTH), lambda i, j: (0, 0)),
            pl.BlockSpec((1, A_WIDTH), lambda i, j: (0, 0)),
        ],
        out_specs=out_specs,
        out_shape=out_shape,
        scratch_shapes=[pltpu.VMEM((tm, D_MODEL), BF16)],
        compiler_params=_params("arbitrary", "arbitrary"),
        name="in_proj",
    )(x, w_in, cos_t, sin_t, ln_g, ln_b)


def _matmul_kernel(x_ref, w_ref, o_ref, ob_ref):
    acc = jnp.dot(x_ref[...].astype(BF16), w_ref[...], preferred_element_type=F32)
    o_ref[...] = acc
    ob_ref[...] = acc.astype(BF16)


def _mem_kv(x, w):
    m, k = x.shape
    n = w.shape[1]
    tn = 512
    return pl.pallas_call(
        _matmul_kernel,
        grid=(n // tn,),
        in_specs=[pl.BlockSpec((m, k), lambda j: (0, 0)), pl.BlockSpec((k, tn), lambda j: (0, j))],
        out_specs=[pl.BlockSpec((m, tn), lambda j: (0, j)), pl.BlockSpec((m, tn), lambda j: (0, j))],
        out_shape=[jax.ShapeDtypeStruct((m, n), F32), jax.ShapeDtypeStruct((m, n), BF16)],
        compiler_params=_params("arbitrary"),
        name="mem_kv",
    )(x, w)


def _gated_groupnorm(o, gn, g):
    return (g.astype(F32) * (_standardize(o) * gn)).astype(BF16)


def _ret_prompt_kernel(cd_ref, q_ref, k_ref, v_ref, g_ref, dmask_ref, qd_ref, kd_ref, gn_ref,
                       o_ref, sfin_ref, s_scr, *, n_chunks):
    t = pl.program_id(1)

    @pl.when(t == 0)
    def _():
        s_scr[...] = jnp.zeros_like(s_scr)

    for c in range(n_chunks):
        rows = slice(c * R_CHUNK, (c + 1) * R_CHUNK)
        for h in range(R_HEADS):
            qk = slice(h * R_QK_DIM, (h + 1) * R_QK_DIM)
            vv = slice(h * R_V_DIM, (h + 1) * R_V_DIM)
            q = q_ref[rows, qk]
            k = k_ref[rows, qk]
            v = v_ref[rows, vv]
            s = s_scr[h]
            sc = lax.dot_general(q, k, (((1,), (1,)), ((), ())), preferred_element_type=F32) * dmask_ref[h]
            o = (jnp.dot(sc.astype(BF16), v, preferred_element_type=F32)
                 + jnp.dot(q, s.astype(BF16), preferred_element_type=F32) * qd_ref[h])
            kd = (k.astype(F32) * kd_ref[h]).astype(BF16)
            s_scr[h] = s * cd_ref[h] + lax.dot_general(kd, v, (((0,), (0,)), ((), ())),
                                                       preferred_element_type=F32)
            o_ref[rows, vv] = _gated_groupnorm(o, gn_ref[:, vv], g_ref[rows, vv])

    @pl.when(t == pl.num_programs(1) - 1)
    def _():
        sfin_ref[0] = s_scr[...]


def _retention_prompt(h_act, tabs, gn, *, batch, seq):
    tr = 512
    n_t = seq // tr
    m = batch * seq
    row = lambda b, t: b * n_t + t
    const3 = lambda b, t: (0, 0, 0)
    return pl.pallas_call(
        functools.partial(_ret_prompt_kernel, n_chunks=tr // R_CHUNK),
        grid=(batch, n_t),
        in_specs=[
            pl.BlockSpec(memory_space=pltpu.SMEM),
            pl.BlockSpec((tr, R_QK_WIDTH), lambda b, t: (row(b, t), OFF_RQ // R_QK_WIDTH)),
            pl.BlockSpec((tr, R_QK_WIDTH), lambda b, t: (row(b, t), OFF_RK // R_QK_WIDTH)),
            pl.BlockSpec((tr, R_V_WIDTH), lambda b, t: (row(b, t), OFF_RV // R_V_WIDTH)),
            pl.BlockSpec((tr, R_V_WIDTH), lambda b, t: (row(b, t), OFF_RG // R_V_WIDTH)),
            pl.BlockSpec((R_HEADS, R_CHUNK, R_CHUNK), const3),
            pl.BlockSpec((R_HEADS, R_CHUNK, 1), const3),
            pl.BlockSpec((R_HEADS, R_CHUNK, 1), const3),
            pl.BlockSpec((1, R_V_WIDTH), lambda b, t: (0, 0)),
        ],
        out_specs=[
            pl.BlockSpec((tr, R_V_WIDTH), lambda b, t: (row(b, t), 0)),
            pl.BlockSpec((1, R_HEADS, R_QK_DIM, R_V_DIM), lambda b, t: (b, 0, 0, 0)),
        ],
        out_shape=[
            jax.ShapeDtypeStruct((m, R_V_WIDTH), BF16),
            jax.ShapeDtypeStruct((batch, R_HEADS, R_QK_DIM, R_V_DIM), F32),
        ],
        scratch_shapes=[pltpu.VMEM((R_HEADS, R_QK_DIM, R_V_DIM), F32)],
        compiler_params=_params("arbitrary", "arbitrary"),
        name="retention_prompt",
    )(tabs["chunk_decay"], h_act, h_act, h_act, h_act, tabs["dmask"], tabs["q_decay"], tabs["k_decay"], gn)


def _ret_sample_kernel(cd_ref, q_ref, k_ref, v_ref, g_ref, s_ref, dmask_ref, qd_ref, kd_ref, gn_ref,
                       o_ref, snew_ref, *, bb, seq):
    q_all = q_ref[...].astype(F32)
    k_all = k_ref[...].astype(F32)
    v_all = v_ref[...].astype(F32)
    g_all = g_ref[...].astype(F32)
    out_rows = []
    for b in range(bb):
        rows = slice(b * seq, (b + 1) * seq)
        out_heads = []
        for h in range(R_HEADS):
            qk = slice(h * R_QK_DIM, (h + 1) * R_QK_DIM)
            vv = slice(h * R_V_DIM, (h + 1) * R_V_DIM)
            q = q_all[rows, qk].astype(BF16)
            kf = k_all[rows, qk]
            v = v_all[rows, vv].astype(BF16)
            s = s_ref[b, h]
            sc = lax.dot_general(q, kf.astype(BF16), (((1,), (1,)), ((), ())),
                                 preferred_element_type=F32) * dmask_ref[h]
            o = (jnp.dot(sc.astype(BF16), v, preferred_element_type=F32)
                 + jnp.dot(q, s.astype(BF16), preferred_element_type=F32) * qd_ref[h])
            kd = (kf * kd_ref[h]).astype(BF16)
            snew_ref[b, h] = s * cd_ref[h] + lax.dot_general(kd, v, (((0,), (0,)), ((), ())),
                                                             preferred_element_type=F32)
            out_heads.append(g_all[rows, vv] * (_standardize(o) * gn_ref[:, vv]))
        out_rows.append(jnp.concatenate(out_heads, axis=1))
    o_ref[...] = jnp.concatenate(out_rows, axis=0).astype(BF16)


def _retention_sample(h_act, state, tabs, gn, *, batch, seq):
    bb = 4
    m = batch * seq
    tr = bb * seq
    const3 = lambda i: (0, 0, 0)
    return pl.pallas_call(
        functools.partial(_ret_sample_kernel, bb=bb, seq=seq),
        grid=(batch // bb,),
        in_specs=[
            pl.BlockSpec(memory_space=pltpu.SMEM),
            pl.BlockSpec((tr, R_QK_WIDTH), lambda i: (i, OFF_RQ // R_QK_WIDTH)),
            pl.BlockSpec((tr, R_QK_WIDTH), lambda i: (i, OFF_RK // R_QK_WIDTH)),
            pl.BlockSpec((tr, R_V_WIDTH), lambda i: (i, OFF_RV // R_V_WIDTH)),
            pl.BlockSpec((tr, R_V_WIDTH), lambda i: (i, OFF_RG // R_V_WIDTH)),
            pl.BlockSpec((bb, R_HEADS, R_QK_DIM, R_V_DIM), lambda i: (i, 0, 0, 0)),
            pl.BlockSpec((R_HEADS, seq, seq), const3),
            pl.BlockSpec((R_HEADS, seq, 1), const3),
            pl.BlockSpec((R_HEADS, seq, 1), const3),
            pl.BlockSpec((1, R_V_WIDTH), lambda i: (0, 0)),
        ],
        out_specs=[
            pl.BlockSpec((tr, R_V_WIDTH), lambda i: (i, 0)),
            pl.BlockSpec((bb, R_HEADS, R_QK_DIM, R_V_DIM), lambda i: (i, 0, 0, 0)),
        ],
        out_shape=[
            jax.ShapeDtypeStruct((m, R_V_WIDTH), BF16),
            jax.ShapeDtypeStruct((batch, R_HEADS, R_QK_DIM, R_V_DIM), F32),
        ],
        compiler_params=_params("arbitrary"),
        name="retention_sample",
    )(tabs["chunk_decay"], h_act, h_act, h_act, h_act, state, tabs["dmask"], tabs["q_decay"],
      tabs["k_decay"], gn)


def _retention_tables(c):
    log_g = jnp.log1p(-jnp.exp2(-5.0 - jnp.arange(R_HEADS, dtype=F32)))
    idx = jnp.arange(c, dtype=F32)
    diff = idx[:, None] - idx[None, :]
    causal = diff >= 0
    dmask = jnp.where(causal[None], jnp.exp(jnp.where(causal, diff, 0.0)[None] * log_g[:, None, None]), 0.0)
    q_decay = jnp.exp((idx[:, None] + 1.0) * log_g[None, :])
    k_decay = jnp.exp((c - 1.0 - idx)[:, None] * log_g[None, :])
    chunk_decay = jnp.exp(c * log_g)
    return {
        "dmask": dmask,
        "q_decay": q_decay.T[:, :, None],
        "k_decay": k_decay.T[:, :, None],
        "chunk_decay": chunk_decay,
    }


def _softmax_rows(s):
    e = jnp.exp(s - jnp.max(s, axis=-1, keepdims=True))
    return e, jnp.sum(e, axis=-1, keepdims=True)


def _mem_prompt_kernel(q_ref, mk_ref, mv_ref, o_ref):
    for h in range(M_HEADS):
        cols = slice(h * M_HEAD_DIM, (h + 1) * M_HEAD_DIM)
        s = lax.dot_general(q_ref[:, cols], mk_ref[:, cols], (((1,), (1,)), ((), ())),
                            preferred_element_type=F32)
        e, l = _softmax_rows(s)
        o = jnp.dot(e.astype(BF16), mv_ref[:, cols], preferred_element_type=F32) / l
        o_ref[:, cols] = o.astype(BF16)


def _mem_attn_prompt(h_act, mk, mv, *, batch, seq):
    tq = 512
    n_t = seq // tq
    return pl.pallas_call(
        _mem_prompt_kernel,
        grid=(batch, n_t),
        in_specs=[
            pl.BlockSpec((tq, M_WIDTH), lambda b, t: (b * n_t + t, OFF_MQ // M_WIDTH)),
            pl.BlockSpec((N_MEM, M_WIDTH), lambda b, t: (b, 0)),
            pl.BlockSpec((N_MEM, M_WIDTH), lambda b, t: (b, 0)),
        ],
        out_specs=pl.BlockSpec((tq, M_WIDTH), lambda b, t: (b * n_t + t, 0)),
        out_shape=jax.ShapeDtypeStruct((batch * seq, M_WIDTH), BF16),
        compiler_params=_params("arbitrary", "arbitrary"),
        name="mem_attn_prompt",
    )(h_act, mk, mv)


def _mem_sample_kernel(q_ref, mk_hbm, mv_hbm, o_ref, kbuf, vbuf, sem, *, bb, seq):
    i = pl.program_id(0)
    n_steps = pl.num_programs(0)
    pairs = [(b, h) for b in range(bb) for h in range(M_HEADS)]

    def copies(step, slot):
        out = []
        for b, h in pairs:
            out.append(pltpu.make_async_copy(mk_hbm.at[step * bb + b, :, h, :], kbuf.at[slot, b, h],
                                             sem.at[0, slot]))
            out.append(pltpu.make_async_copy(mv_hbm.at[step * bb + b, :, h, :], vbuf.at[slot, b, h],
                                             sem.at[1, slot]))
        return out

    slot = i % 2

    @pl.when(i == 0)
    def _():
        for c in copies(0, 0):
            c.start()

    @pl.when(i + 1 < n_steps)
    def _():
        for c in copies(i + 1, 1 - slot):
            c.start()

    for c in copies(i, slot):
        c.wait()

    q_all = q_ref[...].astype(F32)
    scores = []
    for b, h in pairs:
        q = q_all[b * seq:(b + 1) * seq, h * M_HEAD_DIM:(h + 1) * M_HEAD_DIM].astype(BF16)
        scores.append(lax.dot_general(q, kbuf[slot, b, h].astype(BF16), (((1,), (1,)), ((), ())),
                                      preferred_element_type=F32))
    e, l = _softmax_rows(jnp.concatenate(scores, axis=0))
    inv_l = 1.0 / l
    outs = []
    for n, (b, h) in enumerate(pairs):
        rows = slice(n * seq, (n + 1) * seq)
        o = jnp.dot(e[rows].astype(BF16), vbuf[slot, b, h].astype(BF16), preferred_element_type=F32)
        outs.append(o * inv_l[rows])
    out_rows = [jnp.concatenate(outs[b * M_HEADS:(b + 1) * M_HEADS], axis=1) for b in range(bb)]
    o_ref[...] = jnp.concatenate(out_rows, axis=0).astype(BF16)


def _mem_attn_sample(h_act, mk, mv, *, batch, seq):
    bb = 4
    tr = bb * seq
    return pl.pallas_call(
        functools.partial(_mem_sample_kernel, bb=bb, seq=seq),
        grid=(batch // bb,),
        in_specs=[
            pl.BlockSpec((tr, M_WIDTH), lambda i: (i, OFF_MQ // M_WIDTH)),
            pl.BlockSpec(memory_space=pl.ANY),
            pl.BlockSpec(memory_space=pl.ANY),
        ],
        out_specs=pl.BlockSpec((tr, M_WIDTH), lambda i: (i, 0)),
        out_shape=jax.ShapeDtypeStruct((batch * seq, M_WIDTH), BF16),
        scratch_shapes=[
            pltpu.VMEM((2, bb, M_HEADS, N_MEM, M_HEAD_DIM), F32),
            pltpu.VMEM((2, bb, M_HEADS, N_MEM, M_HEAD_DIM), F32),
            pltpu.SemaphoreType.DMA((2, 2)),
        ],
        compiler_params=_params("arbitrary"),
        name="mem_attn_sample",
    )(h_act, mk, mv)


def _merge_kernel(ga_ref, gb_ref, gc_ref, u_ref, va_ref, sw_ref, sb_ref, b_ref, c_ref,
                  wa_ref, wb_ref, wc_ref, o_ref, a_scr, *, tm):
    @pl.when(pl.program_id(1) == 0)
    def _():
        for r in range(tm // CHUNK):
            rows = slice(r * CHUNK, (r + 1) * CHUNK)
            for g in range(A_GROUPS):
                cols = slice(g * A_GROUP_DIM, (g + 1) * A_GROUP_DIM)
                z = jnp.dot(sw_ref[g], va_ref[rows, cols], preferred_element_type=F32) + sb_ref[g]
                a_scr[rows, cols] = (u_ref[rows, cols].astype(F32) * z).astype(BF16)

    ya = jnp.dot(a_scr[...], wa_ref[...], preferred_element_type=F32)
    yb = jnp.dot(b_ref[...], wb_ref[...], preferred_element_type=F32)
    yc = jnp.dot(c_ref[...], wc_ref[...], preferred_element_type=F32)
    merged = (ga_ref[...].astype(F32) * ya + gb_ref[...].astype(F32) * yb
              + gc_ref[...].astype(F32) * yc)
    o_ref[...] = merged.astype(BF16)


def _merge(h_act, va, b_act, c_act, sgu_w, sgu_b, wa, wb, wc, *, va_col_block):
    m = h_act.shape[0]
    tm, tn = 1024, 512
    n_j = D_MODEL // tn
    return pl.pallas_call(
        functools.partial(_merge_kernel, tm=tm),
        grid=(m // tm, n_j),
        in_specs=[
            pl.BlockSpec((tm, tn), lambda i, j: (i, j)),
            pl.BlockSpec((tm, tn), lambda i, j: (i, n_j + j)),
            pl.BlockSpec((tm, tn), lambda i, j: (i, 2 * n_j + j)),
            pl.BlockSpec((tm, A_WIDTH), lambda i, j: (i, OFF_AU // A_WIDTH)),
            pl.BlockSpec((tm, A_WIDTH), lambda i, j: (i, va_col_block)),
            pl.BlockSpec((A_GROUPS, CHUNK, CHUNK), lambda i, j: (0, 0, 0)),
            pl.BlockSpec((A_GROUPS, CHUNK, 1), lambda i, j: (0, 0, 0)),
            pl.BlockSpec((tm, R_V_WIDTH), lambda i, j: (i, 0)),
            pl.BlockSpec((tm, M_WIDTH), lambda i, j: (i, 0)),
            pl.BlockSpec((A_WIDTH, tn), lambda i, j: (0, j)),
            pl.BlockSpec((R_V_WIDTH, tn), lambda i, j: (0, j)),
            pl.BlockSpec((M_WIDTH, tn), lambda i, j: (0, j)),
        ],
        out_specs=pl.BlockSpec((tm, tn), lambda i, j: (i, j)),
        out_shape=jax.ShapeDtypeStruct((m, D_MODEL), BF16),
        scratch_shapes=[pltpu.VMEM((tm, A_WIDTH), BF16)],
        compiler_params=_params("arbitrary", "arbitrary"),
        name="merge",
    )(h_act, h_act, h_act, h_act, va, sgu_w, sgu_b, b_act, c_act, wa, wb, wc)


OUT_ROW_CHUNK = 256


def _out_ln_kernel(m_ref, w_ref, x_ref, g_ref, b_ref, o_ref):
    for r in range(m_ref.shape[0] // OUT_ROW_CHUNK):
        rows = slice(r * OUT_ROW_CHUNK, (r + 1) * OUT_ROW_CHUNK)
        y = jnp.dot(m_ref[rows, :], w_ref[...], preferred_element_type=F32)
        o_ref[rows, :] = _standardize(ALPHA * x_ref[rows, :] + y) * g_ref[...] + b_ref[...]


def _out_ln(merged, w_out, x, ln_g, ln_b):
    m = x.shape[0]
    tm = 1024
    return pl.pallas_call(
        _out_ln_kernel,
        grid=(m // tm,),
        in_specs=[
            pl.BlockSpec((tm, D_MODEL), lambda i: (i, 0)),
            pl.BlockSpec((D_MODEL, D_MODEL), lambda i: (0, 0), pipeline_mode=pl.Buffered(1)),
            pl.BlockSpec((tm, D_MODEL), lambda i: (i, 0)),
            pl.BlockSpec((1, D_MODEL), lambda i: (0, 0)),
            pl.BlockSpec((1, D_MODEL), lambda i: (0, 0)),
        ],
        out_specs=pl.BlockSpec((tm, D_MODEL), lambda i: (i, 0)),
        out_shape=jax.ShapeDtypeStruct((m, D_MODEL), F32),
        compiler_params=_params("arbitrary"),
        name="out_ln",
    )(merged, w_out, x, ln_g, ln_b)


def _ffn_kernel(x_ref, wg_ref, wu_ref, wd_ref, g_ref, b_ref, o_ref, *rest, first_pass):
    if first_pass:
        wgb_ref, wub_ref, wdb_ref, xb_ref, acc_ref = rest
        wgb_ref[...] = wg_ref[...].astype(BF16)
        wub_ref[...] = wu_ref[...].astype(BF16)
        wdb_ref[...] = wd_ref[...].astype(BF16)
    else:
        xb_ref, acc_ref = rest
        wgb_ref, wub_ref, wdb_ref = wg_ref, wu_ref, wd_ref
    f = pl.program_id(1)

    @pl.when(f == 0)
    def _():
        xb_ref[...] = x_ref[...].astype(BF16)
        acc_ref[...] = jnp.zeros_like(acc_ref)

    xb = xb_ref[...]
    gate = jnp.dot(xb, wgb_ref[...], preferred_element_type=F32)
    up = jnp.dot(xb, wub_ref[...], preferred_element_type=F32)
    act = (jax.nn.silu(gate) * up).astype(BF16)
    acc_ref[...] += jnp.dot(act, wdb_ref[...], preferred_element_type=F32)

    @pl.when(f == pl.num_programs(1) - 1)
    def _():
        o_ref[...] = _standardize(ALPHA * x_ref[...] + acc_ref[...]) * g_ref[...] + b_ref[...]


def _ffn(x1, wg, wu, wd, ln_g, ln_b, *, tm, tf, first_pass):
    m = x1.shape[0]
    one_buffer = pl.Buffered(1) if first_pass else None
    out_shape = [jax.ShapeDtypeStruct((m, D_MODEL), F32)]
    out_specs = [pl.BlockSpec((tm, D_MODEL), lambda i, f: (i, 0), pipeline_mode=one_buffer)]
    if first_pass:
        assert m == tm
        out_shape += [jax.ShapeDtypeStruct(w.shape, BF16) for w in (wg, wu, wd)]
        out_specs += [pl.BlockSpec((D_MODEL, tf), lambda i, f: (0, f)),
                      pl.BlockSpec((D_MODEL, tf), lambda i, f: (0, f)),
                      pl.BlockSpec((tf, D_MODEL), lambda i, f: (f, 0))]
    return pl.pallas_call(
        functools.partial(_ffn_kernel, first_pass=first_pass),
        grid=(m // tm, D_FF // tf),
        in_specs=[
            pl.BlockSpec((tm, D_MODEL), lambda i, f: (i, 0), pipeline_mode=one_buffer),
            pl.BlockSpec((D_MODEL, tf), lambda i, f: (0, f)),
            pl.BlockSpec((D_MODEL, tf), lambda i, f: (0, f)),
            pl.BlockSpec((tf, D_MODEL), lambda i, f: (f, 0)),
            pl.BlockSpec((1, D_MODEL), lambda i, f: (0, 0)),
            pl.BlockSpec((1, D_MODEL), lambda i, f: (0, 0)),
        ],
        out_specs=out_specs,
        out_shape=out_shape,
        scratch_shapes=[pltpu.VMEM((tm, D_MODEL), BF16), pltpu.VMEM((tm, D_MODEL), F32)],
        compiler_params=_params("arbitrary", "arbitrary"),
        name="ffn",
    )(x1, wg, wu, wd, ln_g, ln_b)


def _rotary_tables(pos):
    half = R_QK_DIM // 2
    inv = ROPE_BASE ** (-jnp.arange(half, dtype=F32) / half)
    ang = pos[:, None] * inv[None, :]
    cos = jnp.cos(ang)
    sin = jnp.sin(ang)
    cos2 = jnp.concatenate([cos, cos], axis=-1)
    sin2 = jnp.concatenate([-sin, sin], axis=-1)
    k_scale = R_QK_DIM ** -0.5
    return jnp.stack([cos2, cos2 * k_scale]), jnp.stack([sin2, sin2 * k_scale])


def _sgu_tables(sgu_w, sgu_b, c):
    w = jnp.tril(sgu_w[:, :c, :c])
    b = sgu_b[:, :c]
    reps = CHUNK // c
    if reps > 1:
        blk = jnp.arange(CHUNK) // c
        w = jnp.where((blk[:, None] == blk[None, :])[None], jnp.tile(w, (1, reps, reps)), 0.0)
        b = jnp.tile(b, (1, reps))
    return w.astype(BF16), b[:, :, None]


def kernel(x_prompt, x_sample, mem_prompt, state_ret, cache_mem_k, cache_mem_v, w_in, sgu_ln_g, sgu_ln_b, sgu_w, sgu_b, w_proj_a, ret_gn_g, w_proj_b, w_mem_k, w_mem_v, w_proj_c, w_out, ln1_g, ln1_b, w_ffn_gate, w_ffn_up, w_ffn_down, ln2_g, ln2_b):
    assert w_in.shape[0] == DEPTH == 1
    bp, seq, _ = x_prompt.shape
    bs, dseq, _ = x_sample.shape
    l = 0
    bf = lambda w: w[l].astype(BF16)
    wa_b, wb_b, wc_b, wo_b = bf(w_proj_a), bf(w_proj_b), bf(w_proj_c), bf(w_out)
    wmk_b, wmv_b = bf(w_mem_k), bf(w_mem_v)
    ln_g = sgu_ln_g[l].reshape(1, A_WIDTH)
    ln_b = sgu_ln_b[l].reshape(1, A_WIDTH)
    gn = ret_gn_g[l].reshape(1, R_V_WIDTH)
    ln1g, ln1b = ln1_g[l].reshape(1, D_MODEL), ln1_b[l].reshape(1, D_MODEL)
    ln2g, ln2b = ln2_g[l].reshape(1, D_MODEL), ln2_b[l].reshape(1, D_MODEL)

    def mixed(x2, h_act, b_act, c_act, sgu_tabs):
        merged = _merge(h_act, h_act, b_act, c_act, sgu_tabs[0], sgu_tabs[1], wa_b, wb_b, wc_b,
                        va_col_block=OFF_AV // A_WIDTH)
        return _out_ln(merged, wo_b, x2, ln1g, ln1b)

    m_s = bs * dseq
    xs = x_sample.reshape(m_s, D_MODEL)
    cos_s, sin_s = _rotary_tables(float(PAST_LEN) + jnp.arange(dseq, dtype=F32))
    cos_s, sin_s = jnp.tile(cos_s, (1, bs, 1)), jnp.tile(sin_s, (1, bs, 1))
    h_s, va_s, w_in_b = _in_proj(xs, w_in[l], cos_s, sin_s, ln_g, ln_b, tm=m_s, first_pass=True)
    b_s, s_s = _retention_sample(h_s, state_ret[l], _retention_tables(min(dseq, R_CHUNK)), gn,
                                 batch=bs, seq=dseq)
    c_s = _mem_attn_sample(h_s, cache_mem_k[l], cache_mem_v[l], batch=bs, seq=dseq)
    x1_s = mixed(xs, h_s, b_s, c_s, _sgu_tables(sgu_w[l], sgu_b[l], min(dseq, CHUNK)))
    y_s, wg_b, wu_b, wd_b = _ffn(x1_s, w_ffn_gate[l], w_ffn_up[l], w_ffn_down[l], ln2g, ln2b,
                                 tm=m_s, tf=256, first_pass=True)

    xp = x_prompt.reshape(bp * seq, D_MODEL)
    cos_p, sin_p = _rotary_tables(0.0 + jnp.arange(seq, dtype=F32))
    (h_p,) = _in_proj(xp, w_in_b, cos_p, sin_p, ln_g, ln_b, tm=1024, first_pass=False)
    memp = mem_prompt.reshape(bp * N_MEM, D_MODEL)
    mk_p, mk_pb = _mem_kv(memp, wmk_b)
    mv_p, mv_pb = _mem_kv(memp, wmv_b)
    b_p, s_p = _retention_prompt(h_p, _retention_tables(min(seq, R_CHUNK)), gn, batch=bp, seq=seq)
    c_p = _mem_attn_prompt(h_p, mk_pb, mv_pb, batch=bp, seq=seq)
    x1_p = mixed(xp, h_p, b_p, c_p, _sgu_tables(sgu_w[l], sgu_b[l], min(seq, CHUNK)))
    (y_p,) = _ffn(x1_p, wg_b, wu_b, wd_b, ln2g, ln2b, tm=512, tf=512, first_pass=False)

    return (
        y_p.reshape(bp, seq, D_MODEL),
        y_s.reshape(bs, dseq, D_MODEL),
        s_p[None],
        mk_p.reshape(1, bp, N_MEM, M_HEADS, M_HEAD_DIM),
        mv_p.reshape(1, bp, N_MEM, M_HEADS, M_HEAD_DIM),
        s_s[None],
        va_s.reshape(1, bs, dseq, A_GROUPS, A_GROUP_DIM),
    )
```

```python
import functools
import math

import jax
import jax.numpy as jnp
from jax import lax
from jax.experimental import pallas as pl
from jax.experimental.pallas import tpu as pltpu

F32 = jnp.float32
BF16 = jnp.bfloat16

D_MODEL = 2048
DEPTH = 1
PAST_LEN = 16384
CHUNK = 128
A_GROUPS = 4
A_GROUP_DIM = 256
A_WIDTH = A_GROUPS * A_GROUP_DIM
R_HEADS = 8
R_QK_DIM = 128
R_V_DIM = 256
R_QK_WIDTH = R_HEADS * R_QK_DIM
R_V_WIDTH = R_HEADS * R_V_DIM
R_CHUNK = 128
N_MEM = 256
M_HEADS = 4
M_HEAD_DIM = 256
M_WIDTH = M_HEADS * M_HEAD_DIM
N_BRANCH = 3
D_FF = -(-8 * D_MODEL // (3 * 256)) * 256
ALPHA = (2.0 * DEPTH) ** 0.25
ROPE_BASE = 10000.0
LN_EPS = 1e-5

OFF_AU = N_BRANCH * D_MODEL
OFF_AV = OFF_AU + A_WIDTH
OFF_RQ = OFF_AV + A_WIDTH
OFF_RK = OFF_RQ + R_QK_WIDTH
OFF_RV = OFF_RK + R_QK_WIDTH
OFF_RG = OFF_RV + R_V_WIDTH
OFF_MQ = OFF_RG + R_V_WIDTH
IN_WIDTH = OFF_MQ + M_WIDTH

VMEM_LIMIT_BYTES = 56 * 1024 * 1024
LANES = 128


def _params(*sem):
    return pltpu.CompilerParams(dimension_semantics=sem, vmem_limit_bytes=VMEM_LIMIT_BYTES)


def _standardize(x):
    mu = jnp.mean(x, axis=-1, keepdims=True)
    xc = x - mu
    var = jnp.mean(xc * xc, axis=-1, keepdims=True)
    return xc * lax.rsqrt(var + LN_EPS)


IN_TN = 1024
IN_ROW_CHUNK = 256
_J_AU = OFF_AU // IN_TN
_J_AV = OFF_AV // IN_TN
_J_RQ = OFF_RQ // IN_TN
_J_RK = OFF_RK // IN_TN
_J_RV = OFF_RV // IN_TN
_J_RG = OFF_RG // IN_TN
_J_MQ = OFF_MQ // IN_TN
_J_END = IN_WIDTH // IN_TN


def _in_proj_kernel(x_ref, w_ref, cos_ref, sin_ref, lng_ref, lnb_ref, o_ref, *rest, first_pass):
    tm = x_ref.shape[0]
    if first_pass:
        va_ref, wb_ref, xb_ref = rest
        wb_ref[...] = w_ref[...].astype(BF16)
    else:
        (xb_ref,) = rest
        wb_ref = w_ref
    j = pl.program_id(1)

    @pl.when(j == 0)
    def _():
        xb_ref[...] = x_ref[...].astype(BF16)

    def column_range(pred, epilogue):
        @pl.when(pred)
        def _():
            for r in range(tm // IN_ROW_CHUNK):
                rows = slice(r * IN_ROW_CHUNK, (r + 1) * IN_ROW_CHUNK)
                acc = jnp.dot(xb_ref[rows, :], wb_ref[...], preferred_element_type=F32)
                epilogue(rows, acc)

    def store(fn):
        def epilogue(rows, acc):
            o_ref[rows, :] = fn(acc).astype(BF16)
        return epilogue

    def gelu_layernorm(rows, acc):
        v = jax.nn.gelu(acc)
        for g in range(IN_TN // A_GROUP_DIM):
            cols = slice(g * A_GROUP_DIM, (g + 1) * A_GROUP_DIM)
            vn = _standardize(v[:, cols]) * lng_ref[:, cols] + lnb_ref[:, cols]
            o_ref[rows, cols] = vn.astype(BF16)
            if first_pass:
                va_ref[rows, cols] = vn

    def rotary(rows, acc):
        for hh in range(IN_TN // R_QK_DIM):
            cols = slice(hh * R_QK_DIM, (hh + 1) * R_QK_DIM)
            xh = acc[:, cols]
            r = xh * cos_ref[rows, :] + pltpu.roll(xh, R_QK_DIM // 2, 1) * sin_ref[rows, :]
            o_ref[rows, cols] = r.astype(BF16)

    is_mq = j >= _J_MQ
    raw_scale = jnp.where(is_mq, M_HEAD_DIM ** -0.5, 1.0).astype(F32)
    column_range(j < _J_AU, store(jax.nn.sigmoid))
    column_range((j >= _J_AU) & (j < _J_AV), store(jax.nn.gelu))
    column_range((j >= _J_AV) & (j < _J_RQ), gelu_layernorm)
    column_range((j >= _J_RQ) & (j < _J_RV), rotary)
    column_range(((j >= _J_RV) & (j < _J_RG)) | is_mq, store(lambda acc: acc * raw_scale))
    column_range((j >= _J_RG) & (j < _J_MQ), store(jax.nn.silu))


def _in_proj(x, w_in, cos_t, sin_t, ln_g, ln_b, *, tm, first_pass):
    m = x.shape[0]
    n_pos_blocks = cos_t.shape[1] // tm
    assert A_WIDTH == IN_TN and R_QK_WIDTH == IN_TN
    table_idx = lambda i, j: (jnp.where(j == _J_RK, 1, 0), i % n_pos_blocks, 0)

    out_shape = [jax.ShapeDtypeStruct((m, IN_WIDTH), BF16)]
    out_specs = [pl.BlockSpec((tm, IN_TN), lambda i, j: (i, j))]
    if first_pass:
        assert m == tm
        out_shape += [jax.ShapeDtypeStruct((m, A_WIDTH), F32), jax.ShapeDtypeStruct(w_in.shape, BF16)]
        out_specs += [pl.BlockSpec((tm, A_WIDTH), lambda i, j: (i, 0)),
                      pl.BlockSpec((D_MODEL, IN_TN), lambda i, j: (0, j))]
    return pl.pallas_call(
        functools.partial(_in_proj_kernel, first_pass=first_pass),
        grid=(m // tm, _J_END),
        in_specs=[
            pl.BlockSpec((tm, D_MODEL), lambda i, j: (i, 0),
                         pipeline_mode=pl.Buffered(1) if first_pass else None),
            pl.BlockSpec((D_MODEL, IN_TN), lambda i, j: (0, j)),
            pl.BlockSpec((None, tm, R_QK_DIM), table_idx),
            pl.BlockSpec((None, tm, R_QK_DIM), table_idx),
            pl.BlockSpec((1, A_WIDTH), lambda i, j: (0, 0)),
            pl.BlockSpec((1, A_WIDTH), lambda i, j: (0, 0)),
        ],
        out_specs=out_specs,
        out_shape=out_shape,
        scratch_shapes=[pltpu.VMEM((tm, D_MODEL), BF16)],
        compiler_params=_params("arbitrary", "arbitrary"),
        name="in_proj",
    )(x, w_in, cos_t, sin_t, ln_g, ln_b)


def _matmul_kernel(x_ref, w_ref, o_ref, ob_ref):
    acc = jnp.dot(x_ref[...].astype(BF16), w_ref[...], preferred_element_type=F32)
    o_ref[...] = acc
    ob_ref[...] = acc.astype(BF16)


def _mem_kv(x, w):
    m, k = x.shape
    n = w.shape[1]
    tn = 512
    return pl.pallas_call(
        _matmul_kernel,
        grid=(n // tn,),
        in_specs=[pl.BlockSpec((m, k), lambda j: (0, 0)), pl.BlockSpec((k, tn), lambda j: (0, j))],
        out_specs=[pl.BlockSpec((m, tn), lambda j: (0, j)), pl.BlockSpec((m, tn), lambda j: (0, j))],
        out_shape=[jax.ShapeDtypeStruct((m, n), F32), jax.ShapeDtypeStruct((m, n), BF16)],
        compiler_params=_params("arbitrary"),
        name="mem_kv",
    )(x, w)


def _gated_groupnorm(o, gn, g):
    return (g.astype(F32) * (_standardize(o) * gn)).astype(BF16)


def _ret_prompt_kernel(cd_ref, q_ref, k_ref, v_ref, g_ref, dmask_ref, qd_ref, kd_ref, gn_ref,
                       o_ref, sfin_ref, s_scr, *, n_chunks):
    t = pl.program_id(1)

    @pl.when(t == 0)
    def _():
        s_scr[...] = jnp.zeros_like(s_scr)

    state = [s_scr[h] for h in range(R_HEADS)]
    for c in range(n_chunks):
        rows = slice(c * R_CHUNK, (c + 1) * R_CHUNK)
        for h in range(R_HEADS):
            qk = slice(h * R_QK_DIM, (h + 1) * R_QK_DIM)
            vv = slice(h * R_V_DIM, (h + 1) * R_V_DIM)
            q = q_ref[rows, qk]
            k = k_ref[rows, qk]
            v = v_ref[rows, vv]
            s = state[h]
            sc = lax.dot_general(q, k, (((1,), (1,)), ((), ())), preferred_element_type=F32) * dmask_ref[h]
            lhs = jnp.concatenate([sc.astype(BF16), (q.astype(F32) * qd_ref[h]).astype(BF16)], axis=1)
            rhs = jnp.concatenate([v, s.astype(BF16)], axis=0)
            o = jnp.dot(lhs, rhs, preferred_element_type=F32)
            kd = (k.astype(F32) * kd_ref[h]).astype(BF16)
            state[h] = s * cd_ref[h] + lax.dot_general(kd, v, (((0,), (0,)), ((), ())),
                                                       preferred_element_type=F32)
            o_ref[rows, vv] = _gated_groupnorm(o, gn_ref[:, vv], g_ref[rows, vv])
    for h in range(R_HEADS):
        s_scr[h] = state[h]

    @pl.when(t == pl.num_programs(1) - 1)
    def _():
        sfin_ref[0] = s_scr[...]


def _retention_prompt(h_act, tabs, gn, *, batch, seq):
    tr = 512
    n_t = seq // tr
    m = batch * seq
    row = lambda b, t: b * n_t + t
    const3 = lambda b, t: (0, 0, 0)
    return pl.pallas_call(
        functools.partial(_ret_prompt_kernel, n_chunks=tr // R_CHUNK),
        grid=(batch, n_t),
        in_specs=[
            pl.BlockSpec(memory_space=pltpu.SMEM),
            pl.BlockSpec((tr, R_QK_WIDTH), lambda b, t: (row(b, t), OFF_RQ // R_QK_WIDTH)),
            pl.BlockSpec((tr, R_QK_WIDTH), lambda b, t: (row(b, t), OFF_RK // R_QK_WIDTH)),
            pl.BlockSpec((tr, R_V_WIDTH), lambda b, t: (row(b, t), OFF_RV // R_V_WIDTH)),
            pl.BlockSpec((tr, R_V_WIDTH), lambda b, t: (row(b, t), OFF_RG // R_V_WIDTH)),
            pl.BlockSpec((R_HEADS, R_CHUNK, R_CHUNK), const3),
            pl.BlockSpec((R_HEADS, R_CHUNK, R_QK_DIM), const3),
            pl.BlockSpec((R_HEADS, R_CHUNK, R_QK_DIM), const3),
            pl.BlockSpec((1, R_V_WIDTH), lambda b, t: (0, 0)),
        ],
        out_specs=[
            pl.BlockSpec((tr, R_V_WIDTH), lambda b, t: (row(b, t), 0)),
            pl.BlockSpec((1, R_HEADS, R_QK_DIM, R_V_DIM), lambda b, t: (b, 0, 0, 0)),
        ],
        out_shape=[
            jax.ShapeDtypeStruct((m, R_V_WIDTH), BF16),
            jax.ShapeDtypeStruct((batch, R_HEADS, R_QK_DIM, R_V_DIM), F32),
        ],
        scratch_shapes=[pltpu.VMEM((R_HEADS, R_QK_DIM, R_V_DIM), F32)],
        compiler_params=_params("arbitrary", "arbitrary"),
        name="retention_prompt",
    )(tabs["chunk_decay"], h_act, h_act, h_act, h_act, tabs["dmask"],
      jnp.broadcast_to(tabs["q_decay"], (R_HEADS, R_CHUNK, R_QK_DIM)),
      jnp.broadcast_to(tabs["k_decay"], (R_HEADS, R_CHUNK, R_QK_DIM)), gn)


def _ret_sample_kernel(cd_ref, q_ref, k_ref, v_ref, g_ref, s_ref, dmask_ref, qd_ref, kd_ref, gn_ref,
                       o_ref, snew_ref, *, bb, seq):
    k_all = k_ref[...].astype(F32)
    v_all = v_ref[...].astype(F32)
    for h in range(R_HEADS):
        qk = slice(h * R_QK_DIM, (h + 1) * R_QK_DIM)
        vv = slice(h * R_V_DIM, (h + 1) * R_V_DIM)
        q = q_ref[:, qk]
        sc = lax.dot_general(q, k_ref[:, qk], (((1,), (1,)), ((), ())),
                             preferred_element_type=F32) * dmask_ref[h]
        intra = jnp.dot(sc.astype(BF16), v_ref[:, vv], preferred_element_type=F32)
        q_scaled = q.astype(F32) * qd_ref[h]
        k_scaled = k_all[:, qk] * kd_ref[h]
        inter = []
        for b in range(bb):
            rows = slice(b * seq, (b + 1) * seq)
            s = s_ref[b, h]
            inter.append(jnp.dot(q_scaled[rows].astype(BF16), s.astype(BF16), preferred_element_type=F32))
            snew_ref[b, h] = s * cd_ref[h] + lax.dot_general(
                k_scaled[rows].astype(BF16), v_all[rows, vv].astype(BF16), (((0,), (0,)), ((), ())),
                preferred_element_type=F32)
        o = intra + jnp.concatenate(inter, axis=0)
        o_ref[:, vv] = _gated_groupnorm(o, gn_ref[:, vv], g_ref[:, vv])


def _retention_sample(h_act, state, tabs, gn, *, batch, seq):
    bb = 8
    m = batch * seq
    tr = bb * seq
    const3 = lambda i: (0, 0, 0)
    blk = jnp.arange(tr) // seq
    dmask = jnp.where((blk[:, None] == blk[None, :])[None], jnp.tile(tabs["dmask"], (1, bb, bb)), 0.0)
    q_decay = jnp.broadcast_to(jnp.tile(tabs["q_decay"], (1, bb, 1)), (R_HEADS, tr, R_QK_DIM))
    k_decay = jnp.broadcast_to(jnp.tile(tabs["k_decay"], (1, bb, 1)), (R_HEADS, tr, R_QK_DIM))
    return pl.pallas_call(
        functools.partial(_ret_sample_kernel, bb=bb, seq=seq),
        grid=(batch // bb,),
        in_specs=[
            pl.BlockSpec(memory_space=pltpu.SMEM),
            pl.BlockSpec((tr, R_QK_WIDTH), lambda i: (i, OFF_RQ // R_QK_WIDTH)),
            pl.BlockSpec((tr, R_QK_WIDTH), lambda i: (i, OFF_RK // R_QK_WIDTH)),
            pl.BlockSpec((tr, R_V_WIDTH), lambda i: (i, OFF_RV // R_V_WIDTH)),
            pl.BlockSpec((tr, R_V_WIDTH), lambda i: (i, OFF_RG // R_V_WIDTH)),
            pl.BlockSpec((bb, R_HEADS, R_QK_DIM, R_V_DIM), lambda i: (i, 0, 0, 0)),
            pl.BlockSpec((R_HEADS, tr, tr), const3),
            pl.BlockSpec((R_HEADS, tr, R_QK_DIM), const3),
            pl.BlockSpec((R_HEADS, tr, R_QK_DIM), const3),
            pl.BlockSpec((1, R_V_WIDTH), lambda i: (0, 0)),
        ],
        out_specs=[
            pl.BlockSpec((tr, R_V_WIDTH), lambda i: (i, 0)),
            pl.BlockSpec((bb, R_HEADS, R_QK_DIM, R_V_DIM), lambda i: (i, 0, 0, 0)),
        ],
        out_shape=[
            jax.ShapeDtypeStruct((m, R_V_WIDTH), BF16),
            jax.ShapeDtypeStruct((batch, R_HEADS, R_QK_DIM, R_V_DIM), F32),
        ],
        compiler_params=_params("arbitrary"),
        name="retention_sample",
    )(tabs["chunk_decay"], h_act, h_act, h_act, h_act, state, dmask, q_decay, k_decay, gn)


def _retention_tables(c):
    log_g = jnp.log1p(-jnp.exp2(-5.0 - jnp.arange(R_HEADS, dtype=F32)))
    idx = jnp.arange(c, dtype=F32)
    diff = idx[:, None] - idx[None, :]
    causal = diff >= 0
    dmask = jnp.where(causal[None], jnp.exp(jnp.where(causal, diff, 0.0)[None] * log_g[:, None, None]), 0.0)
    q_decay = jnp.exp((idx[:, None] + 1.0) * log_g[None, :])
    k_decay = jnp.exp((c - 1.0 - idx)[:, None] * log_g[None, :])
    chunk_decay = jnp.exp(c * log_g)
    return {
        "dmask": dmask,
        "q_decay": q_decay.T[:, :, None],
        "k_decay": k_decay.T[:, :, None],
        "chunk_decay": chunk_decay,
    }


def _softmax_rows(s):
    e = jnp.exp(s - jnp.max(s, axis=-1, keepdims=True))
    return e, jnp.sum(e, axis=-1, keepdims=True)


def _mem_prompt_kernel(q_ref, mk_ref, mv_ref, o_ref):
    for h in range(M_HEADS):
        cols = slice(h * M_HEAD_DIM, (h + 1) * M_HEAD_DIM)
        s = lax.dot_general(q_ref[:, cols], mk_ref[:, cols], (((1,), (1,)), ((), ())),
                            preferred_element_type=F32)
        e, l = _softmax_rows(s)
        o = jnp.dot(e.astype(BF16), mv_ref[:, cols], preferred_element_type=F32) / l
        o_ref[:, cols] = o.astype(BF16)


def _mem_attn_prompt(h_act, mk, mv, *, batch, seq):
    tq = 512
    n_t = seq // tq
    return pl.pallas_call(
        _mem_prompt_kernel,
        grid=(batch, n_t),
        in_specs=[
            pl.BlockSpec((tq, M_WIDTH), lambda b, t: (b * n_t + t, OFF_MQ // M_WIDTH)),
            pl.BlockSpec((N_MEM, M_WIDTH), lambda b, t: (b, 0)),
            pl.BlockSpec((N_MEM, M_WIDTH), lambda b, t: (b, 0)),
        ],
        out_specs=pl.BlockSpec((tq, M_WIDTH), lambda b, t: (b * n_t + t, 0)),
        out_shape=jax.ShapeDtypeStruct((batch * seq, M_WIDTH), BF16),
        compiler_params=_params("arbitrary", "arbitrary"),
        name="mem_attn_prompt",
    )(h_act, mk, mv)


def _mem_sample_kernel(q_ref, mk_hbm, mv_hbm, o_ref, kbuf, vbuf, sem, *, bb, seq):
    i = pl.program_id(0)
    n_steps = pl.num_programs(0)
    pairs = [(b, h) for b in range(bb) for h in range(M_HEADS)]

    def copies(step, slot):
        out = []
        for b, h in pairs:
            out.append(pltpu.make_async_copy(mk_hbm.at[step * bb + b, :, h, :], kbuf.at[slot, b, h],
                                             sem.at[0, slot]))
            out.append(pltpu.make_async_copy(mv_hbm.at[step * bb + b, :, h, :], vbuf.at[slot, b, h],
                                             sem.at[1, slot]))
        return out

    slot = i % 2

    @pl.when(i == 0)
    def _():
        for c in copies(0, 0):
            c.start()

    @pl.when(i + 1 < n_steps)
    def _():
        for c in copies(i + 1, 1 - slot):
            c.start()

    for c in copies(i, slot):
        c.wait()

    q_all = q_ref[...].astype(F32)
    scores = []
    for b, h in pairs:
        q = q_all[b * seq:(b + 1) * seq, h * M_HEAD_DIM:(h + 1) * M_HEAD_DIM].astype(BF16)
        scores.append(lax.dot_general(q, kbuf[slot, b, h].astype(BF16), (((1,), (1,)), ((), ())),
                                      preferred_element_type=F32))
    e, l = _softmax_rows(jnp.concatenate(scores, axis=0))
    inv_l = 1.0 / l
    outs = []
    for n, (b, h) in enumerate(pairs):
        rows = slice(n * seq, (n + 1) * seq)
        o = jnp.dot(e[rows].astype(BF16), vbuf[slot, b, h].astype(BF16), preferred_element_type=F32)
        outs.append(o * inv_l[rows])
    out_rows = [jnp.concatenate(outs[b * M_HEADS:(b + 1) * M_HEADS], axis=1) for b in range(bb)]
    o_ref[...] = jnp.concatenate(out_rows, axis=0).astype(BF16)


def _mem_attn_sample(h_act, mk, mv, *, batch, seq):
    bb = 4
    tr = bb * seq
    return pl.pallas_call(
        functools.partial(_mem_sample_kernel, bb=bb, seq=seq),
        grid=(batch // bb,),
        in_specs=[
            pl.BlockSpec((tr, M_WIDTH), lambda i: (i, OFF_MQ // M_WIDTH)),
            pl.BlockSpec(memory_space=pl.ANY),
            pl.BlockSpec(memory_space=pl.ANY),
        ],
        out_specs=pl.BlockSpec((tr, M_WIDTH), lambda i: (i, 0)),
        out_shape=jax.ShapeDtypeStruct((batch * seq, M_WIDTH), BF16),
        scratch_shapes=[
            pltpu.VMEM((2, bb, M_HEADS, N_MEM, M_HEAD_DIM), F32),
            pltpu.VMEM((2, bb, M_HEADS, N_MEM, M_HEAD_DIM), F32),
            pltpu.SemaphoreType.DMA((2, 2)),
        ],
        compiler_params=_params("arbitrary"),
        name="mem_attn_sample",
    )(h_act, mk, mv)


MERGE_ROW_CHUNK = 256


def _merge_kernel(ga_ref, gb_ref, gc_ref, u_ref, va_ref, sw_ref, sb_ref, b_ref, c_ref,
                  wa_ref, wb_ref, wc_ref, o_ref, a_scr, *, tm):
    @pl.when(pl.program_id(1) == 0)
    def _():
        for r in range(tm // CHUNK):
            rows = slice(r * CHUNK, (r + 1) * CHUNK)
            for g in range(A_GROUPS):
                cols = slice(g * A_GROUP_DIM, (g + 1) * A_GROUP_DIM)
                z = jnp.dot(sw_ref[g], va_ref[rows, cols], preferred_element_type=F32) + sb_ref[g]
                a_scr[rows, cols] = (u_ref[rows, cols].astype(F32) * z).astype(BF16)

    for r in range(tm // MERGE_ROW_CHUNK):
        rows = slice(r * MERGE_ROW_CHUNK, (r + 1) * MERGE_ROW_CHUNK)
        ya = jnp.dot(a_scr[rows, :], wa_ref[...], preferred_element_type=F32)
        yb = jnp.dot(b_ref[rows, :], wb_ref[...], preferred_element_type=F32)
        yc = jnp.dot(c_ref[rows, :], wc_ref[...], preferred_element_type=F32)
        merged = (ga_ref[rows, :].astype(F32) * ya + gb_ref[rows, :].astype(F32) * yb
                  + gc_ref[rows, :].astype(F32) * yc)
        o_ref[rows, :] = merged.astype(BF16)


def _merge(h_act, va, b_act, c_act, sgu_w, sgu_b, wa, wb, wc, *, va_col_block):
    m = h_act.shape[0]
    tm, tn = 1024, 512
    n_j = D_MODEL // tn
    return pl.pallas_call(
        functools.partial(_merge_kernel, tm=tm),
        grid=(m // tm, n_j),
        in_specs=[
            pl.BlockSpec((tm, tn), lambda i, j: (i, j)),
            pl.BlockSpec((tm, tn), lambda i, j: (i, n_j + j)),
            pl.BlockSpec((tm, tn), lambda i, j: (i, 2 * n_j + j)),
            pl.BlockSpec((tm, A_WIDTH), lambda i, j: (i, OFF_AU // A_WIDTH)),
            pl.BlockSpec((tm, A_WIDTH), lambda i, j: (i, va_col_block)),
            pl.BlockSpec((A_GROUPS, CHUNK, CHUNK), lambda i, j: (0, 0, 0)),
            pl.BlockSpec((A_GROUPS, CHUNK, 1), lambda i, j: (0, 0, 0)),
            pl.BlockSpec((tm, R_V_WIDTH), lambda i, j: (i, 0)),
            pl.BlockSpec((tm, M_WIDTH), lambda i, j: (i, 0)),
            pl.BlockSpec((A_WIDTH, tn), lambda i, j: (0, j)),
            pl.BlockSpec((R_V_WIDTH, tn), lambda i, j: (0, j)),
            pl.BlockSpec((M_WIDTH, tn), lambda i, j: (0, j)),
        ],
        out_specs=pl.BlockSpec((tm, tn), lambda i, j: (i, j)),
        out_shape=jax.ShapeDtypeStruct((m, D_MODEL), BF16),
        scratch_shapes=[pltpu.VMEM((tm, A_WIDTH), BF16)],
        compiler_params=_params("arbitrary", "arbitrary"),
        name="merge",
    )(h_act, h_act, h_act, h_act, va, sgu_w, sgu_b, b_act, c_act, wa, wb, wc)


OUT_ROW_CHUNK = 256


def _out_ln_kernel(m_ref, w_ref, x_ref, g_ref, b_ref, o_ref):
    for r in range(m_ref.shape[0] // OUT_ROW_CHUNK):
        rows = slice(r * OUT_ROW_CHUNK, (r + 1) * OUT_ROW_CHUNK)
        y = jnp.dot(m_ref[rows, :], w_ref[...], preferred_element_type=F32)
        o_ref[rows, :] = _standardize(ALPHA * x_ref[rows, :] + y) * g_ref[...] + b_ref[...]


def _out_ln(merged, w_out, x, ln_g, ln_b):
    m = x.shape[0]
    tm = 1024
    return pl.pallas_call(
        _out_ln_kernel,
        grid=(m // tm,),
        in_specs=[
            pl.BlockSpec((tm, D_MODEL), lambda i: (i, 0)),
            pl.BlockSpec((D_MODEL, D_MODEL), lambda i: (0, 0), pipeline_mode=pl.Buffered(1)),
            pl.BlockSpec((tm, D_MODEL), lambda i: (i, 0)),
            pl.BlockSpec((1, D_MODEL), lambda i: (0, 0)),
            pl.BlockSpec((1, D_MODEL), lambda i: (0, 0)),
        ],
        out_specs=pl.BlockSpec((tm, D_MODEL), lambda i: (i, 0)),
        out_shape=jax.ShapeDtypeStruct((m, D_MODEL), F32),
        compiler_params=_params("arbitrary"),
        name="out_ln",
    )(merged, w_out, x, ln_g, ln_b)


def _ffn_kernel(x_ref, wg_ref, wu_ref, wd_ref, g_ref, b_ref, o_ref, *rest, first_pass):
    if first_pass:
        wgb_ref, wub_ref, wdb_ref, xb_ref, acc_ref = rest
        wgb_ref[...] = wg_ref[...].astype(BF16)
        wub_ref[...] = wu_ref[...].astype(BF16)
        wdb_ref[...] = wd_ref[...].astype(BF16)
    else:
        xb_ref, acc_ref = rest
        wgb_ref, wub_ref, wdb_ref = wg_ref, wu_ref, wd_ref
    f = pl.program_id(1)

    @pl.when(f == 0)
    def _():
        xb_ref[...] = x_ref[...].astype(BF16)
        acc_ref[...] = jnp.zeros_like(acc_ref)

    xb = xb_ref[...]
    gate = jnp.dot(xb, wgb_ref[...], preferred_element_type=F32)
    up = jnp.dot(xb, wub_ref[...], preferred_element_type=F32)
    act = (jax.nn.silu(gate) * up).astype(BF16)
    acc_ref[...] += jnp.dot(act, wdb_ref[...], preferred_element_type=F32)

    @pl.when(f == pl.num_programs(1) - 1)
    def _():
        o_ref[...] = _standardize(ALPHA * x_ref[...] + acc_ref[...]) * g_ref[...] + b_ref[...]


def _ffn(x1, wg, wu, wd, ln_g, ln_b, *, tm, tf, first_pass):
    m = x1.shape[0]
    one_buffer = pl.Buffered(1) if first_pass else None
    out_shape = [jax.ShapeDtypeStruct((m, D_MODEL), F32)]
    out_specs = [pl.BlockSpec((tm, D_MODEL), lambda i, f: (i, 0), pipeline_mode=one_buffer)]
    if first_pass:
        assert m == tm
        out_shape += [jax.ShapeDtypeStruct(w.shape, BF16) for w in (wg, wu, wd)]
        out_specs += [pl.BlockSpec((D_MODEL, tf), lambda i, f: (0, f)),
                      pl.BlockSpec((D_MODEL, tf), lambda i, f: (0, f)),
                      pl.BlockSpec((tf, D_MODEL), lambda i, f: (f, 0))]
    return pl.pallas_call(
        functools.partial(_ffn_kernel, first_pass=first_pass),
        grid=(m // tm, D_FF // tf),
        in_specs=[
            pl.BlockSpec((tm, D_MODEL), lambda i, f: (i, 0), pipeline_mode=one_buffer),
            pl.BlockSpec((D_MODEL, tf), lambda i, f: (0, f)),
            pl.BlockSpec((D_MODEL, tf), lambda i, f: (0, f)),
            pl.BlockSpec((tf, D_MODEL), lambda i, f: (f, 0)),
            pl.BlockSpec((1, D_MODEL), lambda i, f: (0, 0)),
            pl.BlockSpec((1, D_MODEL), lambda i, f: (0, 0)),
        ],
        out_specs=out_specs,
        out_shape=out_shape,
        scratch_shapes=[pltpu.VMEM((tm, D_MODEL), BF16), pltpu.VMEM((tm, D_MODEL), F32)],
        compiler_params=_params("arbitrary", "arbitrary"),
        name="ffn",
    )(x1, wg, wu, wd, ln_g, ln_b)


def _rotary_tables(pos):
    half = R_QK_DIM // 2
    inv = ROPE_BASE ** (-jnp.arange(half, dtype=F32) / half)
    ang = pos[:, None] * inv[None, :]
    cos = jnp.cos(ang)
    sin = jnp.sin(ang)
    cos2 = jnp.concatenate([cos, cos], axis=-1)
    sin2 = jnp.concatenate([-sin, sin], axis=-1)
    k_scale = R_QK_DIM ** -0.5
    return jnp.stack([cos2, cos2 * k_scale]), jnp.stack([sin2, sin2 * k_scale])


def _sgu_tables(sgu_w, sgu_b, c):
    w = jnp.tril(sgu_w[:, :c, :c])
    b = sgu_b[:, :c]
    reps = CHUNK // c
    if reps > 1:
        blk = jnp.arange(CHUNK) // c
        w = jnp.where((blk[:, None] == blk[None, :])[None], jnp.tile(w, (1, reps, reps)), 0.0)
        b = jnp.tile(b, (1, reps))
    return w.astype(BF16), b[:, :, None]


def kernel(x_prompt, x_sample, mem_prompt, state_ret, cache_mem_k, cache_mem_v, w_in, sgu_ln_g, sgu_ln_b, sgu_w, sgu_b, w_proj_a, ret_gn_g, w_proj_b, w_mem_k, w_mem_v, w_proj_c, w_out, ln1_g, ln1_b, w_ffn_gate, w_ffn_up, w_ffn_down, ln2_g, ln2_b):
    assert w_in.shape[0] == DEPTH == 1
    bp, seq, _ = x_prompt.shape
    bs, dseq, _ = x_sample.shape
    l = 0
    bf = lambda w: w[l].astype(BF16)
    wa_b, wb_b, wc_b, wo_b = bf(w_proj_a), bf(w_proj_b), bf(w_proj_c), bf(w_out)
    wmk_b, wmv_b = bf(w_mem_k), bf(w_mem_v)
    ln_g = sgu_ln_g[l].reshape(1, A_WIDTH)
    ln_b = sgu_ln_b[l].reshape(1, A_WIDTH)
    gn = ret_gn_g[l].reshape(1, R_V_WIDTH)
    ln1g, ln1b = ln1_g[l].reshape(1, D_MODEL), ln1_b[l].reshape(1, D_MODEL)
    ln2g, ln2b = ln2_g[l].reshape(1, D_MODEL), ln2_b[l].reshape(1, D_MODEL)

    def mixed(x2, h_act, b_act, c_act, sgu_tabs):
        merged = _merge(h_act, h_act, b_act, c_act, sgu_tabs[0], sgu_tabs[1], wa_b, wb_b, wc_b,
                        va_col_block=OFF_AV // A_WIDTH)
        return _out_ln(merged, wo_b, x2, ln1g, ln1b)

    m_s = bs * dseq
    xs = x_sample.reshape(m_s, D_MODEL)
    cos_s, sin_s = _rotary_tables(float(PAST_LEN) + jnp.arange(dseq, dtype=F32))
    cos_s, sin_s = jnp.tile(cos_s, (1, bs, 1)), jnp.tile(sin_s, (1, bs, 1))
    h_s, va_s, w_in_b = _in_proj(xs, w_in[l], cos_s, sin_s, ln_g, ln_b, tm=m_s, first_pass=True)
    b_s, s_s = _retention_sample(h_s, state_ret[l], _retention_tables(min(dseq, R_CHUNK)), gn,
                                 batch=bs, seq=dseq)
    c_s = _mem_attn_sample(h_s, cache_mem_k[l], cache_mem_v[l], batch=bs, seq=dseq)
    x1_s = mixed(xs, h_s, b_s, c_s, _sgu_tables(sgu_w[l], sgu_b[l], min(dseq, CHUNK)))
    y_s, wg_b, wu_b, wd_b = _ffn(x1_s, w_ffn_gate[l], w_ffn_up[l], w_ffn_down[l], ln2g, ln2b,
                                 tm=m_s, tf=256, first_pass=True)

    xp = x_prompt.reshape(bp * seq, D_MODEL)
    cos_p, sin_p = _rotary_tables(0.0 + jnp.arange(seq, dtype=F32))
    (h_p,) = _in_proj(xp, w_in_b, cos_p, sin_p, ln_g, ln_b, tm=1024, first_pass=False)
    memp = mem_prompt.reshape(bp * N_MEM, D_MODEL)
    mk_p, mk_pb = _mem_kv(memp, wmk_b)
    mv_p, mv_pb = _mem_kv(memp, wmv_b)
    b_p, s_p = _retention_prompt(h_p, _retention_tables(min(seq, R_CHUNK)), gn, batch=bp, seq=seq)
    c_p = _mem_attn_prompt(h_p, mk_pb, mv_pb, batch=bp, seq=seq)
    x1_p = mixed(xp, h_p, b_p, c_p, _sgu_tables(sgu_w[l], sgu_b[l], min(seq, CHUNK)))
    (y_p,) = _ffn(x1_p, wg_b, wu_b, wd_b, ln2g, ln2b, tm=512, tf=512, first_pass=False)

    return (
        y_p.reshape(bp, seq, D_MODEL),
        y_s.reshape(bs, dseq, D_MODEL),
        s_p[None],
        mk_p.reshape(1, bp, N_MEM, M_HEADS, M_HEAD_DIM),
        mv_p.reshape(1, bp, N_MEM, M_HEADS, M_HEAD_DIM),
        s_s[None],
        va_s.reshape(1, bs, dseq, A_GROUPS, A_GROUP_DIM),
    )
```

```python
import functools
import math

import jax
import jax.numpy as jnp
from jax import lax
from jax.experimental import pallas as pl
from jax.experimental.pallas import tpu as pltpu

F32 = jnp.float32
BF16 = jnp.bfloat16

D_MODEL = 2048
DEPTH = 1
PAST_LEN = 16384
CHUNK = 128
A_GROUPS = 4
A_GROUP_DIM = 256
A_WIDTH = A_GROUPS * A_GROUP_DIM
R_HEADS = 8
R_QK_DIM = 128
R_V_DIM = 256
R_QK_WIDTH = R_HEADS * R_QK_DIM
R_V_WIDTH = R_HEADS * R_V_DIM
R_CHUNK = 128
N_MEM = 256
M_HEADS = 4
M_HEAD_DIM = 256
M_WIDTH = M_HEADS * M_HEAD_DIM
N_BRANCH = 3
D_FF = -(-8 * D_MODEL // (3 * 256)) * 256
ALPHA = (2.0 * DEPTH) ** 0.25
ROPE_BASE = 10000.0
LN_EPS = 1e-5

OFF_AU = N_BRANCH * D_MODEL
OFF_AV = OFF_AU + A_WIDTH
OFF_RQ = OFF_AV + A_WIDTH
OFF_RK = OFF_RQ + R_QK_WIDTH
OFF_RV = OFF_RK + R_QK_WIDTH
OFF_RG = OFF_RV + R_V_WIDTH
OFF_MQ = OFF_RG + R_V_WIDTH
IN_WIDTH = OFF_MQ + M_WIDTH

VMEM_LIMIT_BYTES = 56 * 1024 * 1024
LANES = 128


def _params(*sem):
    return pltpu.CompilerParams(dimension_semantics=sem, vmem_limit_bytes=VMEM_LIMIT_BYTES)


def _standardize(x):
    mu = jnp.mean(x, axis=-1, keepdims=True)
    xc = x - mu
    var = jnp.mean(xc * xc, axis=-1, keepdims=True)
    return xc * lax.rsqrt(var + LN_EPS)


IN_TN = 1024
IN_ROW_CHUNK = 256
_J_AU = OFF_AU // IN_TN
_J_AV = OFF_AV // IN_TN
_J_RQ = OFF_RQ // IN_TN
_J_RK = OFF_RK // IN_TN
_J_RV = OFF_RV // IN_TN
_J_RG = OFF_RG // IN_TN
_J_MQ = OFF_MQ // IN_TN
_J_END = IN_WIDTH // IN_TN


def _in_proj_kernel(x_ref, w_ref, cos_ref, sin_ref, lng_ref, lnb_ref, o_ref, *rest, first_pass):
    tm = x_ref.shape[0]
    if first_pass:
        va_ref, wb_ref, xb_ref = rest
        wb_ref[...] = w_ref[...].astype(BF16)
    else:
        (xb_ref,) = rest
        wb_ref = w_ref
    j = pl.program_id(1)

    @pl.when(j == 0)
    def _():
        xb_ref[...] = x_ref[...].astype(BF16)

    def column_range(pred, epilogue):
        @pl.when(pred)
        def _():
            for r in range(tm // IN_ROW_CHUNK):
                rows = slice(r * IN_ROW_CHUNK, (r + 1) * IN_ROW_CHUNK)
                acc = jnp.dot(xb_ref[rows, :], wb_ref[...], preferred_element_type=F32)
                epilogue(rows, acc)

    def store(fn):
        def epilogue(rows, acc):
            o_ref[rows, :] = fn(acc).astype(BF16)
        return epilogue

    def gelu_layernorm(rows, acc):
        v = jax.nn.gelu(acc)
        for g in range(IN_TN // A_GROUP_DIM):
            cols = slice(g * A_GROUP_DIM, (g + 1) * A_GROUP_DIM)
            vn = _standardize(v[:, cols]) * lng_ref[:, cols] + lnb_ref[:, cols]
            o_ref[rows, cols] = vn.astype(BF16)
            if first_pass:
                va_ref[rows, cols] = vn

    def rotary(rows, acc):
        for hh in range(IN_TN // R_QK_DIM):
            cols = slice(hh * R_QK_DIM, (hh + 1) * R_QK_DIM)
            xh = acc[:, cols]
            r = xh * cos_ref[rows, :] + pltpu.roll(xh, R_QK_DIM // 2, 1) * sin_ref[rows, :]
            o_ref[rows, cols] = r.astype(BF16)

    is_mq = j >= _J_MQ
    raw_scale = jnp.where(is_mq, M_HEAD_DIM ** -0.5, 1.0).astype(F32)
    column_range(j < _J_AU, store(jax.nn.sigmoid))
    column_range((j >= _J_AU) & (j < _J_AV), store(jax.nn.gelu))
    column_range((j >= _J_AV) & (j < _J_RQ), gelu_layernorm)
    column_range((j >= _J_RQ) & (j < _J_RV), rotary)
    column_range(((j >= _J_RV) & (j < _J_RG)) | is_mq, store(lambda acc: acc * raw_scale))
    column_range((j >= _J_RG) & (j < _J_MQ), store(jax.nn.silu))


def _in_proj(x, w_in, cos_t, sin_t, ln_g, ln_b, *, tm, first_pass):
    m = x.shape[0]
    n_pos_blocks = cos_t.shape[1] // tm
    assert A_WIDTH == IN_TN and R_QK_WIDTH == IN_TN
    table_idx = lambda i, j: (jnp.where(j == _J_RK, 1, 0), i % n_pos_blocks, 0)

    out_shape = [jax.ShapeDtypeStruct((m, IN_WIDTH), BF16)]
    out_specs = [pl.BlockSpec((tm, IN_TN), lambda i, j: (i, j))]
    if first_pass:
        assert m == tm
        out_shape += [jax.ShapeDtypeStruct((m, A_WIDTH), F32), jax.ShapeDtypeStruct(w_in.shape, BF16)]
        out_specs += [pl.BlockSpec((tm, A_WIDTH), lambda i, j: (i, 0)),
                      pl.BlockSpec((D_MODEL, IN_TN), lambda i, j: (0, j))]
    return pl.pallas_call(
        functools.partial(_in_proj_kernel, first_pass=first_pass),
        grid=(m // tm, _J_END),
        in_specs=[
            pl.BlockSpec((tm, D_MODEL), lambda i, j: (i, 0),
                         pipeline_mode=pl.Buffered(1) if first_pass else None),
            pl.BlockSpec((D_MODEL, IN_TN), lambda i, j: (0, j)),
            pl.BlockSpec((None, tm, R_QK_DIM), table_idx),
            pl.BlockSpec((None, tm, R_QK_DIM), table_idx),
            pl.BlockSpec((1, A_WIDTH), lambda i, j: (0, 0)),
            pl.BlockSpec((1, A_WIDTH), lambda i, j: (0, 0)),
        ],
        out_specs=out_specs,
        out_shape=out_shape,
        scratch_shapes=[pltpu.VMEM((tm, D_MODEL), BF16)],
        compiler_params=_params("arbitrary", "arbitrary"),
        name="in_proj",
    )(x, w_in, cos_t, sin_t, ln_g, ln_b)


def _matmul_kernel(x_ref, w_ref, o_ref, ob_ref):
    acc = jnp.dot(x_ref[...].astype(BF16), w_ref[...], preferred_element_type=F32)
    o_ref[...] = acc
    ob_ref[...] = acc.astype(BF16)


def _mem_kv(x, w):
    m, k = x.shape
    n = w.shape[1]
    tn = 512
    return pl.pallas_call(
        _matmul_kernel,
        grid=(n // tn,),
        in_specs=[pl.BlockSpec((m, k), lambda j: (0, 0)), pl.BlockSpec((k, tn), lambda j: (0, j))],
        out_specs=[pl.BlockSpec((m, tn), lambda j: (0, j)), pl.BlockSpec((m, tn), lambda j: (0, j))],
        out_shape=[jax.ShapeDtypeStruct((m, n), F32), jax.ShapeDtypeStruct((m, n), BF16)],
        compiler_params=_params("arbitrary"),
        name="mem_kv",
    )(x, w)


def _gated_groupnorm(o, gn, g):
    return (g.astype(F32) * (_standardize(o) * gn)).astype(BF16)


def _ret_prompt_kernel(cd_ref, q_ref, k_ref, v_ref, g_ref, dmask_ref, qd_ref, kd_ref, gn_ref,
                       o_ref, sfin_ref, s_scr, *, n_chunks):
    t = pl.program_id(1)

    @pl.when(t == 0)
    def _():
        s_scr[...] = jnp.zeros_like(s_scr)

    state = [s_scr[h] for h in range(R_HEADS)]
    for c in range(n_chunks):
        rows = slice(c * R_CHUNK, (c + 1) * R_CHUNK)
        for h in range(R_HEADS):
            qk = slice(h * R_QK_DIM, (h + 1) * R_QK_DIM)
            vv = slice(h * R_V_DIM, (h + 1) * R_V_DIM)
            q = q_ref[rows, qk]
            k = k_ref[rows, qk]
            v = v_ref[rows, vv]
            s = state[h]
            sc = lax.dot_general(q, k, (((1,), (1,)), ((), ())), preferred_element_type=F32) * dmask_ref[h]
            lhs = jnp.concatenate([sc.astype(BF16), (q.astype(F32) * qd_ref[h]).astype(BF16)], axis=1)
            rhs = jnp.concatenate([v, s.astype(BF16)], axis=0)
            o = jnp.dot(lhs, rhs, preferred_element_type=F32)
            kd = (k.astype(F32) * kd_ref[h]).astype(BF16)
            state[h] = s * cd_ref[h] + lax.dot_general(kd, v, (((0,), (0,)), ((), ())),
                                                       preferred_element_type=F32)
            o_ref[rows, vv] = _gated_groupnorm(o, gn_ref[:, vv], g_ref[rows, vv])
    for h in range(R_HEADS):
        s_scr[h] = state[h]

    @pl.when(t == pl.num_programs(1) - 1)
    def _():
        sfin_ref[0] = s_scr[...]


def _retention_prompt(h_act, tabs, gn, *, batch, seq):
    tr = 512
    n_t = seq // tr
    m = batch * seq
    row = lambda b, t: b * n_t + t
    const3 = lambda b, t: (0, 0, 0)
    return pl.pallas_call(
        functools.partial(_ret_prompt_kernel, n_chunks=tr // R_CHUNK),
        grid=(batch, n_t),
        in_specs=[
            pl.BlockSpec(memory_space=pltpu.SMEM),
            pl.BlockSpec((tr, R_QK_WIDTH), lambda b, t: (row(b, t), OFF_RQ // R_QK_WIDTH)),
            pl.BlockSpec((tr, R_QK_WIDTH), lambda b, t: (row(b, t), OFF_RK // R_QK_WIDTH)),
            pl.BlockSpec((tr, R_V_WIDTH), lambda b, t: (row(b, t), OFF_RV // R_V_WIDTH)),
            pl.BlockSpec((tr, R_V_WIDTH), lambda b, t: (row(b, t), OFF_RG // R_V_WIDTH)),
            pl.BlockSpec((R_HEADS, R_CHUNK, R_CHUNK), const3),
            pl.BlockSpec((R_HEADS, R_CHUNK, R_QK_DIM), const3),
            pl.BlockSpec((R_HEADS, R_CHUNK, R_QK_DIM), const3),
            pl.BlockSpec((1, R_V_WIDTH), lambda b, t: (0, 0)),
        ],
        out_specs=[
            pl.BlockSpec((tr, R_V_WIDTH), lambda b, t: (row(b, t), 0)),
            pl.BlockSpec((1, R_HEADS, R_QK_DIM, R_V_DIM), lambda b, t: (b, 0, 0, 0)),
        ],
        out_shape=[
            jax.ShapeDtypeStruct((m, R_V_WIDTH), BF16),
            jax.ShapeDtypeStruct((batch, R_HEADS, R_QK_DIM, R_V_DIM), F32),
        ],
        scratch_shapes=[pltpu.VMEM((R_HEADS, R_QK_DIM, R_V_DIM), F32)],
        compiler_params=_params("arbitrary", "arbitrary"),
        name="retention_prompt",
    )(tabs["chunk_decay"], h_act, h_act, h_act, h_act, tabs["dmask"],
      jnp.broadcast_to(tabs["q_decay"], (R_HEADS, R_CHUNK, R_QK_DIM)),
      jnp.broadcast_to(tabs["k_decay"], (R_HEADS, R_CHUNK, R_QK_DIM)), gn)


def _ret_sample_kernel(cd_ref, q_ref, k_ref, v_ref, g_ref, s_ref, dmask_ref, qd_ref, kd_ref, gn_ref,
                       o_ref, snew_ref, *, bb, seq):
    k_all = k_ref[...].astype(F32)
    v_all = v_ref[...].astype(F32)
    for h in range(R_HEADS):
        qk = slice(h * R_QK_DIM, (h + 1) * R_QK_DIM)
        vv = slice(h * R_V_DIM, (h + 1) * R_V_DIM)
        q = q_ref[:, qk]
        sc = lax.dot_general(q, k_ref[:, qk], (((1,), (1,)), ((), ())),
                             preferred_element_type=F32) * dmask_ref[h]
        intra = jnp.dot(sc.astype(BF16), v_ref[:, vv], preferred_element_type=F32)
        q_scaled = q.astype(F32) * qd_ref[h]
        k_scaled = k_all[:, qk] * kd_ref[h]
        inter = []
        for b in range(bb):
            rows = slice(b * seq, (b + 1) * seq)
            s = s_ref[b, h]
            inter.append(jnp.dot(q_scaled[rows].astype(BF16), s.astype(BF16), preferred_element_type=F32))
            snew_ref[b, h] = s * cd_ref[h] + lax.dot_general(
                k_scaled[rows].astype(BF16), v_all[rows, vv].astype(BF16), (((0,), (0,)), ((), ())),
                preferred_element_type=F32)
        o = intra + jnp.concatenate(inter, axis=0)
        o_ref[:, vv] = _gated_groupnorm(o, gn_ref[:, vv], g_ref[:, vv])


def _retention_sample(h_act, state, tabs, gn, *, batch, seq):
    bb = 8
    m = batch * seq
    tr = bb * seq
    const3 = lambda i: (0, 0, 0)
    blk = jnp.arange(tr) // seq
    dmask = jnp.where((blk[:, None] == blk[None, :])[None], jnp.tile(tabs["dmask"], (1, bb, bb)), 0.0)
    q_decay = jnp.broadcast_to(jnp.tile(tabs["q_decay"], (1, bb, 1)), (R_HEADS, tr, R_QK_DIM))
    k_decay = jnp.broadcast_to(jnp.tile(tabs["k_decay"], (1, bb, 1)), (R_HEADS, tr, R_QK_DIM))
    return pl.pallas_call(
        functools.partial(_ret_sample_kernel, bb=bb, seq=seq),
        grid=(batch // bb,),
        in_specs=[
            pl.BlockSpec(memory_space=pltpu.SMEM),
            pl.BlockSpec((tr, R_QK_WIDTH), lambda i: (i, OFF_RQ // R_QK_WIDTH)),
            pl.BlockSpec((tr, R_QK_WIDTH), lambda i: (i, OFF_RK // R_QK_WIDTH)),
            pl.BlockSpec((tr, R_V_WIDTH), lambda i: (i, OFF_RV // R_V_WIDTH)),
            pl.BlockSpec((tr, R_V_WIDTH), lambda i: (i, OFF_RG // R_V_WIDTH)),
            pl.BlockSpec((bb, R_HEADS, R_QK_DIM, R_V_DIM), lambda i: (i, 0, 0, 0)),
            pl.BlockSpec((R_HEADS, tr, tr), const3),
            pl.BlockSpec((R_HEADS, tr, R_QK_DIM), const3),
            pl.BlockSpec((R_HEADS, tr, R_QK_DIM), const3),
            pl.BlockSpec((1, R_V_WIDTH), lambda i: (0, 0)),
        ],
        out_specs=[
            pl.BlockSpec((tr, R_V_WIDTH), lambda i: (i, 0)),
            pl.BlockSpec((bb, R_HEADS, R_QK_DIM, R_V_DIM), lambda i: (i, 0, 0, 0)),
        ],
        out_shape=[
            jax.ShapeDtypeStruct((m, R_V_WIDTH), BF16),
            jax.ShapeDtypeStruct((batch, R_HEADS, R_QK_DIM, R_V_DIM), F32),
        ],
        compiler_params=_params("arbitrary"),
        name="retention_sample",
    )(tabs["chunk_decay"], h_act, h_act, h_act, h_act, state, dmask, q_decay, k_decay, gn)


def _retention_tables(c):
    log_g = jnp.log1p(-jnp.exp2(-5.0 - jnp.arange(R_HEADS, dtype=F32)))
    idx = jnp.arange(c, dtype=F32)
    diff = idx[:, None] - idx[None, :]
    causal = diff >= 0
    dmask = jnp.where(causal[None], jnp.exp(jnp.where(causal, diff, 0.0)[None] * log_g[:, None, None]), 0.0)
    q_decay = jnp.exp((idx[:, None] + 1.0) * log_g[None, :])
    k_decay = jnp.exp((c - 1.0 - idx)[:, None] * log_g[None, :])
    chunk_decay = jnp.exp(c * log_g)
    return {
        "dmask": dmask,
        "q_decay": q_decay.T[:, :, None],
        "k_decay": k_decay.T[:, :, None],
        "chunk_decay": chunk_decay,
    }


def _softmax_rows(s):
    e = jnp.exp(s - jnp.max(s, axis=-1, keepdims=True))
    return e, jnp.sum(e, axis=-1, keepdims=True)


def _mem_prompt_kernel(q_ref, mk_ref, mv_ref, o_ref):
    for h in range(M_HEADS):
        cols = slice(h * M_HEAD_DIM, (h + 1) * M_HEAD_DIM)
        s = lax.dot_general(q_ref[:, cols], mk_ref[:, cols], (((1,), (1,)), ((), ())),
                            preferred_element_type=F32)
        e, l = _softmax_rows(s)
        o = jnp.dot(e.astype(BF16), mv_ref[:, cols], preferred_element_type=F32) / l
        o_ref[:, cols] = o.astype(BF16)


def _mem_attn_prompt(h_act, mk, mv, *, batch, seq):
    tq = 512
    n_t = seq // tq
    return pl.pallas_call(
        _mem_prompt_kernel,
        grid=(batch, n_t),
        in_specs=[
            pl.BlockSpec((tq, M_WIDTH), lambda b, t: (b * n_t + t, OFF_MQ // M_WIDTH)),
            pl.BlockSpec((N_MEM, M_WIDTH), lambda b, t: (b, 0)),
            pl.BlockSpec((N_MEM, M_WIDTH), lambda b, t: (b, 0)),
        ],
        out_specs=pl.BlockSpec((tq, M_WIDTH), lambda b, t: (b * n_t + t, 0)),
        out_shape=jax.ShapeDtypeStruct((batch * seq, M_WIDTH), BF16),
        compiler_params=_params("arbitrary", "arbitrary"),
        name="mem_attn_prompt",
    )(h_act, mk, mv)


def _mem_sample_kernel(q_ref, mk_hbm, mv_hbm, o_ref, kbuf, vbuf, sem, *, bb, seq):
    i = pl.program_id(0)
    n_steps = pl.num_programs(0)
    pairs = [(b, h) for b in range(bb) for h in range(M_HEADS)]

    def copies(step, slot):
        out = []
        for b, h in pairs:
            out.append(pltpu.make_async_copy(mk_hbm.at[step * bb + b, :, h, :], kbuf.at[slot, b, h],
                                             sem.at[0, slot]))
            out.append(pltpu.make_async_copy(mv_hbm.at[step * bb + b, :, h, :], vbuf.at[slot, b, h],
                                             sem.at[1, slot]))
        return out

    slot = i % 2

    @pl.when(i == 0)
    def _():
        for c in copies(0, 0):
            c.start()

    @pl.when(i + 1 < n_steps)
    def _():
        for c in copies(i + 1, 1 - slot):
            c.start()

    for c in copies(i, slot):
        c.wait()

    q_all = q_ref[...].astype(F32)
    scores = []
    for b, h in pairs:
        q = q_all[b * seq:(b + 1) * seq, h * M_HEAD_DIM:(h + 1) * M_HEAD_DIM].astype(BF16)
        scores.append(lax.dot_general(q, kbuf[slot, b, h].astype(BF16), (((1,), (1,)), ((), ())),
                                      preferred_element_type=F32))
    e, l = _softmax_rows(jnp.concatenate(scores, axis=0))
    inv_l = 1.0 / l
    outs = []
    for n, (b, h) in enumerate(pairs):
        rows = slice(n * seq, (n + 1) * seq)
        o = jnp.dot(e[rows].astype(BF16), vbuf[slot, b, h].astype(BF16), preferred_element_type=F32)
        outs.append(o * inv_l[rows])
    out_rows = [jnp.concatenate(outs[b * M_HEADS:(b + 1) * M_HEADS], axis=1) for b in range(bb)]
    o_ref[...] = jnp.concatenate(out_rows, axis=0).astype(BF16)


def _mem_attn_sample(h_act, mk, mv, *, batch, seq):
    bb = 4
    tr = bb * seq
    return pl.pallas_call(
        functools.partial(_mem_sample_kernel, bb=bb, seq=seq),
        grid=(batch // bb,),
        in_specs=[
            pl.BlockSpec((tr, M_WIDTH), lambda i: (i, OFF_MQ // M_WIDTH)),
            pl.BlockSpec(memory_space=pl.ANY),
            pl.BlockSpec(memory_space=pl.ANY),
        ],
        out_specs=pl.BlockSpec((tr, M_WIDTH), lambda i: (i, 0)),
        out_shape=jax.ShapeDtypeStruct((batch * seq, M_WIDTH), BF16),
        scratch_shapes=[
            pltpu.VMEM((2, bb, M_HEADS, N_MEM, M_HEAD_DIM), F32),
            pltpu.VMEM((2, bb, M_HEADS, N_MEM, M_HEAD_DIM), F32),
            pltpu.SemaphoreType.DMA((2, 2)),
        ],
        compiler_params=_params("arbitrary"),
        name="mem_attn_sample",
    )(h_act, mk, mv)


MERGE_ROW_CHUNK = 256


def _merge_kernel(ga_ref, gb_ref, gc_ref, u_ref, va_ref, sw_ref, sb_ref, b_ref, c_ref,
                  wa_ref, wb_ref, wc_ref, o_ref, a_scr, *, tm):
    @pl.when(pl.program_id(1) == 0)
    def _():
        for r in range(tm // CHUNK):
            rows = slice(r * CHUNK, (r + 1) * CHUNK)
            for g in range(A_GROUPS):
                cols = slice(g * A_GROUP_DIM, (g + 1) * A_GROUP_DIM)
                z = jnp.dot(sw_ref[g], va_ref[rows, cols], preferred_element_type=F32) + sb_ref[g]
                a_scr[rows, cols] = (u_ref[rows, cols].astype(F32) * z).astype(BF16)

    for r in range(tm // MERGE_ROW_CHUNK):
        rows = slice(r * MERGE_ROW_CHUNK, (r + 1) * MERGE_ROW_CHUNK)
        ya = jnp.dot(a_scr[rows, :], wa_ref[...], preferred_element_type=F32)
        yb = jnp.dot(b_ref[rows, :], wb_ref[...], preferred_element_type=F32)
        yc = jnp.dot(c_ref[rows, :], wc_ref[...], preferred_element_type=F32)
        merged = (ga_ref[rows, :].astype(F32) * ya + gb_ref[rows, :].astype(F32) * yb
                  + gc_ref[rows, :].astype(F32) * yc)
        o_ref[rows, :] = merged.astype(BF16)


def _merge(h_act, va, b_act, c_act, sgu_w, sgu_b, wa, wb, wc, *, va_col_block):
    m = h_act.shape[0]
    tm, tn = 1024, 512
    n_j = D_MODEL // tn
    return pl.pallas_call(
        functools.partial(_merge_kernel, tm=tm),
        grid=(m // tm, n_j),
        in_specs=[
            pl.BlockSpec((tm, tn), lambda i, j: (i, j)),
            pl.BlockSpec((tm, tn), lambda i, j: (i, n_j + j)),
            pl.BlockSpec((tm, tn), lambda i, j: (i, 2 * n_j + j)),
            pl.BlockSpec((tm, A_WIDTH), lambda i, j: (i, OFF_AU // A_WIDTH)),
            pl.BlockSpec((tm, A_WIDTH), lambda i, j: (i, va_col_block)),
            pl.BlockSpec((A_GROUPS, CHUNK, CHUNK), lambda i, j: (0, 0, 0)),
            pl.BlockSpec((A_GROUPS, CHUNK, 1), lambda i, j: (0, 0, 0)),
            pl.BlockSpec((tm, R_V_WIDTH), lambda i, j: (i, 0)),
            pl.BlockSpec((tm, M_WIDTH), lambda i, j: (i, 0)),
            pl.BlockSpec((A_WIDTH, tn), lambda i, j: (0, j)),
            pl.BlockSpec((R_V_WIDTH, tn), lambda i, j: (0, j)),
            pl.BlockSpec((M_WIDTH, tn), lambda i, j: (0, j)),
        ],
        out_specs=pl.BlockSpec((tm, tn), lambda i, j: (i, j)),
        out_shape=jax.ShapeDtypeStruct((m, D_MODEL), BF16),
        scratch_shapes=[pltpu.VMEM((tm, A_WIDTH), BF16)],
        compiler_params=_params("arbitrary", "arbitrary"),
        name="merge",
    )(h_act, h_act, h_act, h_act, va, sgu_w, sgu_b, b_act, c_act, wa, wb, wc)


OUT_ROW_CHUNK = 256


def _out_ln_kernel(m_ref, w_ref, x_ref, g_ref, b_ref, o_ref):
    for r in range(m_ref.shape[0] // OUT_ROW_CHUNK):
        rows = slice(r * OUT_ROW_CHUNK, (r + 1) * OUT_ROW_CHUNK)
        y = jnp.dot(m_ref[rows, :], w_ref[...], preferred_element_type=F32)
        o_ref[rows, :] = _standardize(ALPHA * x_ref[rows, :] + y) * g_ref[...] + b_ref[...]


def _out_ln(merged, w_out, x, ln_g, ln_b):
    m = x.shape[0]
    tm = 1024
    return pl.pallas_call(
        _out_ln_kernel,
        grid=(m // tm,),
        in_specs=[
            pl.BlockSpec((tm, D_MODEL), lambda i: (i, 0)),
            pl.BlockSpec((D_MODEL, D_MODEL), lambda i: (0, 0), pipeline_mode=pl.Buffered(1)),
            pl.BlockSpec((tm, D_MODEL), lambda i: (i, 0)),
            pl.BlockSpec((1, D_MODEL), lambda i: (0, 0)),
            pl.BlockSpec((1, D_MODEL), lambda i: (0, 0)),
        ],
        out_specs=pl.BlockSpec((tm, D_MODEL), lambda i: (i, 0)),
        out_shape=jax.ShapeDtypeStruct((m, D_MODEL), F32),
        compiler_params=_params("arbitrary"),
        name="out_ln",
    )(merged, w_out, x, ln_g, ln_b)


def _ffn_kernel(x_ref, wg_ref, wu_ref, wd_ref, g_ref, b_ref, o_ref, *rest, first_pass):
    if first_pass:
        wgb_ref, wub_ref, wdb_ref, xb_ref, acc_ref = rest
        wgb_ref[...] = wg_ref[...].astype(BF16)
        wub_ref[...] = wu_ref[...].astype(BF16)
        wdb_ref[...] = wd_ref[...].astype(BF16)
    else:
        xb_ref, acc_ref = rest
        wgb_ref, wub_ref, wdb_ref = wg_ref, wu_ref, wd_ref
    f = pl.program_id(1)

    @pl.when(f == 0)
    def _():
        xb_ref[...] = x_ref[...].astype(BF16)
        acc_ref[...] = jnp.zeros_like(acc_ref)

    xb = xb_ref[...]
    gate = jnp.dot(xb, wgb_ref[...], preferred_element_type=F32)
    up = jnp.dot(xb, wub_ref[...], preferred_element_type=F32)
    act = (jax.nn.silu(gate) * up).astype(BF16)
    acc_ref[...] += jnp.dot(act, wdb_ref[...], preferred_element_type=F32)

    @pl.when(f == pl.num_programs(1) - 1)
    def _():
        o_ref[...] = _standardize(ALPHA * x_ref[...] + acc_ref[...]) * g_ref[...] + b_ref[...]


def _ffn(x1, wg, wu, wd, ln_g, ln_b, *, tm, tf, first_pass):
    m = x1.shape[0]
    one_buffer = pl.Buffered(1) if first_pass else None
    out_shape = [jax.ShapeDtypeStruct((m, D_MODEL), F32)]
    out_specs = [pl.BlockSpec((tm, D_MODEL), lambda i, f: (i, 0), pipeline_mode=one_buffer)]
    if first_pass:
        assert m == tm
        out_shape += [jax.ShapeDtypeStruct(w.shape, BF16) for w in (wg, wu, wd)]
        out_specs += [pl.BlockSpec((D_MODEL, tf), lambda i, f: (0, f)),
                      pl.BlockSpec((D_MODEL, tf), lambda i, f: (0, f)),
                      pl.BlockSpec((tf, D_MODEL), lambda i, f: (f, 0))]
    return pl.pallas_call(
        functools.partial(_ffn_kernel, first_pass=first_pass),
        grid=(m // tm, D_FF // tf),
        in_specs=[
            pl.BlockSpec((tm, D_MODEL), lambda i, f: (i, 0), pipeline_mode=one_buffer),
            pl.BlockSpec((D_MODEL, tf), lambda i, f: (0, f)),
            pl.BlockSpec((D_MODEL, tf), lambda i, f: (0, f)),
            pl.BlockSpec((tf, D_MODEL), lambda i, f: (f, 0)),
            pl.BlockSpec((1, D_MODEL), lambda i, f: (0, 0)),
            pl.BlockSpec((1, D_MODEL), lambda i, f: (0, 0)),
        ],
        out_specs=out_specs,
        out_shape=out_shape,
        scratch_shapes=[pltpu.VMEM((tm, D_MODEL), BF16), pltpu.VMEM((tm, D_MODEL), F32)],
        compiler_params=_params("arbitrary", "arbitrary"),
        name="ffn",
    )(x1, wg, wu, wd, ln_g, ln_b)


TAIL_ROW_CHUNK = 256


def _tail_kernel(m_ref, wo_ref, x_ref, g1_ref, b1_ref, wg_ref, wu_ref, wd_ref, g2_ref, b2_ref,
                 o_ref, x1_scr, x1b_scr):
    f = pl.program_id(1)
    n_f = pl.num_programs(1)
    chunks = [slice(r * TAIL_ROW_CHUNK, (r + 1) * TAIL_ROW_CHUNK)
              for r in range(m_ref.shape[0] // TAIL_ROW_CHUNK)]

    def swiglu(rows):
        xb = x1b_scr[rows, :]
        gate = jnp.dot(xb, wg_ref[...], preferred_element_type=F32)
        up = jnp.dot(xb, wu_ref[...], preferred_element_type=F32)
        act = (jax.nn.silu(gate) * up).astype(BF16)
        return jnp.dot(act, wd_ref[...], preferred_element_type=F32)

    @pl.when(f == 0)
    def _():
        for rows in chunks:
            y = jnp.dot(m_ref[rows, :], wo_ref[...], preferred_element_type=F32)
            x1 = _standardize(ALPHA * x_ref[rows, :] + y) * g1_ref[...] + b1_ref[...]
            x1_scr[rows, :] = x1
            x1b_scr[rows, :] = x1.astype(BF16)
        for rows in chunks:
            o_ref[rows, :] = swiglu(rows)

    @pl.when((f > 0) & (f < n_f - 1))
    def _():
        o_ref[...] += swiglu(slice(None))

    @pl.when(f == n_f - 1)
    def _():
        for rows in chunks:
            ff = o_ref[rows, :] + swiglu(rows)
            o_ref[rows, :] = _standardize(ALPHA * x1_scr[rows, :] + ff) * g2_ref[...] + b2_ref[...]


def _tail(merged, w_out, x, ln1_g, ln1_b, wg, wu, wd, ln2_g, ln2_b):
    m = x.shape[0]
    tm, tf = 512, 512
    assert D_FF // tf >= 2
    row = lambda i, f: (i, 0)
    const = lambda i, f: (0, 0)
    return pl.pallas_call(
        _tail_kernel,
        grid=(m // tm, D_FF // tf),
        in_specs=[
            pl.BlockSpec((tm, D_MODEL), row),
            pl.BlockSpec((D_MODEL, D_MODEL), const, pipeline_mode=pl.Buffered(1)),
            pl.BlockSpec((tm, D_MODEL), row),
            pl.BlockSpec((1, D_MODEL), const),
            pl.BlockSpec((1, D_MODEL), const),
            pl.BlockSpec((D_MODEL, tf), lambda i, f: (0, f)),
            pl.BlockSpec((D_MODEL, tf), lambda i, f: (0, f)),
            pl.BlockSpec((tf, D_MODEL), lambda i, f: (f, 0)),
            pl.BlockSpec((1, D_MODEL), const),
            pl.BlockSpec((1, D_MODEL), const),
        ],
        out_specs=pl.BlockSpec((tm, D_MODEL), row),
        out_shape=jax.ShapeDtypeStruct((m, D_MODEL), F32),
        scratch_shapes=[pltpu.VMEM((tm, D_MODEL), F32), pltpu.VMEM((tm, D_MODEL), BF16)],
        compiler_params=_params("arbitrary", "arbitrary"),
        name="tail",
    )(merged, w_out, x, ln1_g, ln1_b, wg, wu, wd, ln2_g, ln2_b)


def _rotary_tables(pos):
    half = R_QK_DIM // 2
    inv = ROPE_BASE ** (-jnp.arange(half, dtype=F32) / half)
    ang = pos[:, None] * inv[None, :]
    cos = jnp.cos(ang)
    sin = jnp.sin(ang)
    cos2 = jnp.concatenate([cos, cos], axis=-1)
    sin2 = jnp.concatenate([-sin, sin], axis=-1)
    k_scale = R_QK_DIM ** -0.5
    return jnp.stack([cos2, cos2 * k_scale]), jnp.stack([sin2, sin2 * k_scale])


def _sgu_tables(sgu_w, sgu_b, c):
    w = jnp.tril(sgu_w[:, :c, :c])
    b = sgu_b[:, :c]
    reps = CHUNK // c
    if reps > 1:
        blk = jnp.arange(CHUNK) // c
        w = jnp.where((blk[:, None] == blk[None, :])[None], jnp.tile(w, (1, reps, reps)), 0.0)
        b = jnp.tile(b, (1, reps))
    return w.astype(BF16), b[:, :, None]


def kernel(x_prompt, x_sample, mem_prompt, state_ret, cache_mem_k, cache_mem_v, w_in, sgu_ln_g, sgu_ln_b, sgu_w, sgu_b, w_proj_a, ret_gn_g, w_proj_b, w_mem_k, w_mem_v, w_proj_c, w_out, ln1_g, ln1_b, w_ffn_gate, w_ffn_up, w_ffn_down, ln2_g, ln2_b):
    assert w_in.shape[0] == DEPTH == 1
    bp, seq, _ = x_prompt.shape
    bs, dseq, _ = x_sample.shape
    l = 0
    bf = lambda w: w[l].astype(BF16)
    wa_b, wb_b, wc_b, wo_b = bf(w_proj_a), bf(w_proj_b), bf(w_proj_c), bf(w_out)
    wmk_b, wmv_b = bf(w_mem_k), bf(w_mem_v)
    ln_g = sgu_ln_g[l].reshape(1, A_WIDTH)
    ln_b = sgu_ln_b[l].reshape(1, A_WIDTH)
    gn = ret_gn_g[l].reshape(1, R_V_WIDTH)
    ln1g, ln1b = ln1_g[l].reshape(1, D_MODEL), ln1_b[l].reshape(1, D_MODEL)
    ln2g, ln2b = ln2_g[l].reshape(1, D_MODEL), ln2_b[l].reshape(1, D_MODEL)

    def merged(h_act, b_act, c_act, sgu_tabs):
        return _merge(h_act, h_act, b_act, c_act, sgu_tabs[0], sgu_tabs[1], wa_b, wb_b, wc_b,
                      va_col_block=OFF_AV // A_WIDTH)

    m_s = bs * dseq
    xs = x_sample.reshape(m_s, D_MODEL)
    cos_s, sin_s = _rotary_tables(float(PAST_LEN) + jnp.arange(dseq, dtype=F32))
    cos_s, sin_s = jnp.tile(cos_s, (1, bs, 1)), jnp.tile(sin_s, (1, bs, 1))
    h_s, va_s, w_in_b = _in_proj(xs, w_in[l], cos_s, sin_s, ln_g, ln_b, tm=m_s, first_pass=True)
    b_s, s_s = _retention_sample(h_s, state_ret[l], _retention_tables(min(dseq, R_CHUNK)), gn,
                                 batch=bs, seq=dseq)
    c_s = _mem_attn_sample(h_s, cache_mem_k[l], cache_mem_v[l], batch=bs, seq=dseq)
    m_s_merged = merged(h_s, b_s, c_s, _sgu_tables(sgu_w[l], sgu_b[l], min(dseq, CHUNK)))
    x1_s = _out_ln(m_s_merged, wo_b, xs, ln1g, ln1b)
    y_s, wg_b, wu_b, wd_b = _ffn(x1_s, w_ffn_gate[l], w_ffn_up[l], w_ffn_down[l], ln2g, ln2b,
                                 tm=m_s, tf=256, first_pass=True)

    xp = x_prompt.reshape(bp * seq, D_MODEL)
    cos_p, sin_p = _rotary_tables(0.0 + jnp.arange(seq, dtype=F32))
    (h_p,) = _in_proj(xp, w_in_b, cos_p, sin_p, ln_g, ln_b, tm=1024, first_pass=False)
    memp = mem_prompt.reshape(bp * N_MEM, D_MODEL)
    mk_p, mk_pb = _mem_kv(memp, wmk_b)
    mv_p, mv_pb = _mem_kv(memp, wmv_b)
    b_p, s_p = _retention_prompt(h_p, _retention_tables(min(seq, R_CHUNK)), gn, batch=bp, seq=seq)
    c_p = _mem_attn_prompt(h_p, mk_pb, mv_pb, batch=bp, seq=seq)
    m_p_merged = merged(h_p, b_p, c_p, _sgu_tables(sgu_w[l], sgu_b[l], min(seq, CHUNK)))
    y_p = _tail(m_p_merged, wo_b, xp, ln1g, ln1b, wg_b, wu_b, wd_b, ln2g, ln2b)

    return (
        y_p.reshape(bp, seq, D_MODEL),
        y_s.reshape(bs, dseq, D_MODEL),
        s_p[None],
        mk_p.reshape(1, bp, N_MEM, M_HEADS, M_HEAD_DIM),
        mv_p.reshape(1, bp, N_MEM, M_HEADS, M_HEAD_DIM),
        s_s[None],
        va_s.reshape(1, bs, dseq, A_GROUPS, A_GROUP_DIM),
    )
```

```python
import functools
import math

import jax
import jax.numpy as jnp
from jax import lax
from jax.experimental import pallas as pl
from jax.experimental.pallas import tpu as pltpu

F32 = jnp.float32
BF16 = jnp.bfloat16

D_MODEL = 2048
DEPTH = 1
PAST_LEN = 16384
CHUNK = 128
A_GROUPS = 4
A_GROUP_DIM = 256
A_WIDTH = A_GROUPS * A_GROUP_DIM
R_HEADS = 8
R_QK_DIM = 128
R_V_DIM = 256
R_QK_WIDTH = R_HEADS * R_QK_DIM
R_V_WIDTH = R_HEADS * R_V_DIM
R_CHUNK = 128
N_MEM = 256
M_HEADS = 4
M_HEAD_DIM = 256
M_WIDTH = M_HEADS * M_HEAD_DIM
N_BRANCH = 3
D_FF = -(-8 * D_MODEL // (3 * 256)) * 256
ALPHA = (2.0 * DEPTH) ** 0.25
ROPE_BASE = 10000.0
LN_EPS = 1e-5

OFF_AU = N_BRANCH * D_MODEL
OFF_AV = OFF_AU + A_WIDTH
OFF_RQ = OFF_AV + A_WIDTH
OFF_RK = OFF_RQ + R_QK_WIDTH
OFF_RV = OFF_RK + R_QK_WIDTH
OFF_RG = OFF_RV + R_V_WIDTH
OFF_MQ = OFF_RG + R_V_WIDTH
IN_WIDTH = OFF_MQ + M_WIDTH

VMEM_LIMIT_BYTES = 56 * 1024 * 1024
LANES = 128


def _params(*sem):
    return pltpu.CompilerParams(dimension_semantics=sem, vmem_limit_bytes=VMEM_LIMIT_BYTES)


def _standardize(x):
    mu = jnp.mean(x, axis=-1, keepdims=True)
    xc = x - mu
    var = jnp.mean(xc * xc, axis=-1, keepdims=True)
    return xc * lax.rsqrt(var + LN_EPS)


IN_TN = 1024
IN_ROW_CHUNK = 256
_J_AU = OFF_AU // IN_TN
_J_AV = OFF_AV // IN_TN
_J_RQ = OFF_RQ // IN_TN
_J_RK = OFF_RK // IN_TN
_J_RV = OFF_RV // IN_TN
_J_RG = OFF_RG // IN_TN
_J_MQ = OFF_MQ // IN_TN
_J_END = IN_WIDTH // IN_TN


def _in_proj_kernel(x_ref, w_ref, cos_ref, sin_ref, lng_ref, lnb_ref, *rest, first_pass, side_seq):
    tm = x_ref.shape[0]
    j = pl.program_id(1)
    if first_pass:
        o_ref, va_ref, wb_ref, xb_ref = rest
        wb_ref[...] = w_ref[...].astype(BF16)
    else:
        qs_ref, mk_hbm, mv_hbm, o_ref, cs_ref, xb_ref, kbuf, vbuf, sem = rest
        wb_ref = w_ref
        _mem_attend_step(pl.program_id(0) * pl.num_programs(1) + j, mk_hbm.shape[0] // MEM_SEQS_PER_STEP,
                         qs_ref, mk_hbm, mv_hbm, cs_ref, kbuf, vbuf, sem, seq=side_seq)

    @pl.when(j == 0)
    def _():
        xb_ref[...] = x_ref[...].astype(BF16)

    def column_range(pred, epilogue):
        @pl.when(pred)
        def _():
            for r in range(tm // IN_ROW_CHUNK):
                rows = slice(r * IN_ROW_CHUNK, (r + 1) * IN_ROW_CHUNK)
                acc = jnp.dot(xb_ref[rows, :], wb_ref[...], preferred_element_type=F32)
                epilogue(rows, acc)

    def store(fn):
        def epilogue(rows, acc):
            o_ref[rows, :] = fn(acc).astype(BF16)
        return epilogue

    def gelu_layernorm(rows, acc):
        v = jax.nn.gelu(acc)
        for g in range(IN_TN // A_GROUP_DIM):
            cols = slice(g * A_GROUP_DIM, (g + 1) * A_GROUP_DIM)
            vn = _standardize(v[:, cols]) * lng_ref[:, cols] + lnb_ref[:, cols]
            o_ref[rows, cols] = vn.astype(BF16)
            if first_pass:
                va_ref[rows, cols] = vn

    def rotary(rows, acc):
        for hh in range(IN_TN // R_QK_DIM):
            cols = slice(hh * R_QK_DIM, (hh + 1) * R_QK_DIM)
            xh = acc[:, cols]
            r = xh * cos_ref[rows, :] + pltpu.roll(xh, R_QK_DIM // 2, 1) * sin_ref[rows, :]
            o_ref[rows, cols] = r.astype(BF16)

    is_mq = j >= _J_MQ
    raw_scale = jnp.where(is_mq, M_HEAD_DIM ** -0.5, 1.0).astype(F32)
    column_range(j < _J_AU, store(jax.nn.sigmoid))
    column_range((j >= _J_AU) & (j < _J_AV), store(jax.nn.gelu))
    column_range((j >= _J_AV) & (j < _J_RQ), gelu_layernorm)
    column_range((j >= _J_RQ) & (j < _J_RV), rotary)
    column_range(((j >= _J_RV) & (j < _J_RG)) | is_mq, store(lambda acc: acc * raw_scale))
    column_range((j >= _J_RG) & (j < _J_MQ), store(jax.nn.silu))


def _in_proj(x, w_in, cos_t, sin_t, ln_g, ln_b, *, tm, side=None):
    m = x.shape[0]
    first_pass = side is None
    n_pos_blocks = cos_t.shape[1] // tm
    assert A_WIDTH == IN_TN and R_QK_WIDTH == IN_TN
    table_idx = lambda i, j: (jnp.where(j == _J_RK, 1, 0), i % n_pos_blocks, 0)
    grid = (m // tm, _J_END)

    operands = [x, w_in, cos_t, sin_t, ln_g, ln_b]
    in_specs = [
        pl.BlockSpec((tm, D_MODEL), lambda i, j: (i, 0),
                     pipeline_mode=pl.Buffered(1) if first_pass else None),
        pl.BlockSpec((D_MODEL, IN_TN), lambda i, j: (0, j)),
        pl.BlockSpec((None, tm, R_QK_DIM), table_idx),
        pl.BlockSpec((None, tm, R_QK_DIM), table_idx),
        pl.BlockSpec((1, A_WIDTH), lambda i, j: (0, 0)),
        pl.BlockSpec((1, A_WIDTH), lambda i, j: (0, 0)),
    ]
    out_shape = [jax.ShapeDtypeStruct((m, IN_WIDTH), BF16)]
    out_specs = [pl.BlockSpec((tm, IN_TN), lambda i, j: (i, j))]
    scratch_shapes = [pltpu.VMEM((tm, D_MODEL), BF16)]
    side_seq = None
    if first_pass:
        assert m == tm
        out_shape += [jax.ShapeDtypeStruct((m, A_WIDTH), F32), jax.ShapeDtypeStruct(w_in.shape, BF16)]
        out_specs += [pl.BlockSpec((tm, A_WIDTH), lambda i, j: (i, 0)),
                      pl.BlockSpec((D_MODEL, IN_TN), lambda i, j: (0, j))]
    else:
        h_side, mem_k, mem_v, side_seq = side
        side_rows = MEM_SEQS_PER_STEP * side_seq
        n_side = mem_k.shape[0] // MEM_SEQS_PER_STEP
        assert mem_k.shape[0] % MEM_SEQS_PER_STEP == 0 and n_side <= grid[0] * grid[1]
        side_block = lambda i, j: jnp.minimum(i * grid[1] + j, n_side - 1)
        operands += [h_side, mem_k, mem_v]
        in_specs += [pl.BlockSpec((side_rows, M_WIDTH), lambda i, j: (side_block(i, j), OFF_MQ // M_WIDTH)),
                     pl.BlockSpec(memory_space=pl.ANY),
                     pl.BlockSpec(memory_space=pl.ANY)]
        out_shape.append(jax.ShapeDtypeStruct((h_side.shape[0], M_WIDTH), BF16))
        out_specs.append(pl.BlockSpec((side_rows, M_WIDTH), lambda i, j: (side_block(i, j), 0)))
        scratch_shapes += [pltpu.VMEM((2, MEM_SEQS_PER_STEP, M_HEADS, N_MEM, M_HEAD_DIM), F32),
                           pltpu.VMEM((2, MEM_SEQS_PER_STEP, M_HEADS, N_MEM, M_HEAD_DIM), F32),
                           pltpu.SemaphoreType.DMA((2, 2))]
    return pl.pallas_call(
        functools.partial(_in_proj_kernel, first_pass=first_pass, side_seq=side_seq),
        grid=grid,
        in_specs=in_specs,
        out_specs=out_specs,
        out_shape=out_shape,
        scratch_shapes=scratch_shapes,
        compiler_params=_params("arbitrary", "arbitrary"),
        name="in_proj",
    )(*operands)


def _matmul_kernel(x_ref, w_ref, o_ref, ob_ref):
    acc = jnp.dot(x_ref[...].astype(BF16), w_ref[...], preferred_element_type=F32)
    o_ref[...] = acc
    ob_ref[...] = acc.astype(BF16)


def _mem_kv(x, w):
    m, k = x.shape
    n = w.shape[1]
    tn = 512
    return pl.pallas_call(
        _matmul_kernel,
        grid=(n // tn,),
        in_specs=[pl.BlockSpec((m, k), lambda j: (0, 0)), pl.BlockSpec((k, tn), lambda j: (0, j))],
        out_specs=[pl.BlockSpec((m, tn), lambda j: (0, j)), pl.BlockSpec((m, tn), lambda j: (0, j))],
        out_shape=[jax.ShapeDtypeStruct((m, n), F32), jax.ShapeDtypeStruct((m, n), BF16)],
        compiler_params=_params("arbitrary"),
        name="mem_kv",
    )(x, w)


def _gated_groupnorm(o, gn, g):
    return (g.astype(F32) * (_standardize(o) * gn)).astype(BF16)


def _ret_prompt_kernel(cd_ref, q_ref, k_ref, v_ref, g_ref, dmask_ref, qd_ref, kd_ref, gn_ref,
                       o_ref, sfin_ref, s_scr, *, n_chunks):
    t = pl.program_id(1)

    @pl.when(t == 0)
    def _():
        s_scr[...] = jnp.zeros_like(s_scr)

    state = [s_scr[h] for h in range(R_HEADS)]
    for c in range(n_chunks):
        rows = slice(c * R_CHUNK, (c + 1) * R_CHUNK)
        for h in range(R_HEADS):
            qk = slice(h * R_QK_DIM, (h + 1) * R_QK_DIM)
            vv = slice(h * R_V_DIM, (h + 1) * R_V_DIM)
            q = q_ref[rows, qk]
            k = k_ref[rows, qk]
            v = v_ref[rows, vv]
            s = state[h]
            sc = lax.dot_general(q, k, (((1,), (1,)), ((), ())), preferred_element_type=F32) * dmask_ref[h]
            lhs = jnp.concatenate([sc.astype(BF16), (q.astype(F32) * qd_ref[h]).astype(BF16)], axis=1)
            rhs = jnp.concatenate([v, s.astype(BF16)], axis=0)
            o = jnp.dot(lhs, rhs, preferred_element_type=F32)
            kd = (k.astype(F32) * kd_ref[h]).astype(BF16)
            state[h] = s * cd_ref[h] + lax.dot_general(kd, v, (((0,), (0,)), ((), ())),
                                                       preferred_element_type=F32)
            o_ref[rows, vv] = _gated_groupnorm(o, gn_ref[:, vv], g_ref[rows, vv])
    for h in range(R_HEADS):
        s_scr[h] = state[h]

    @pl.when(t == pl.num_programs(1) - 1)
    def _():
        sfin_ref[0] = s_scr[...]


def _retention_prompt(h_act, tabs, gn, *, batch, seq):
    tr = 512
    n_t = seq // tr
    m = batch * seq
    row = lambda b, t: b * n_t + t
    const3 = lambda b, t: (0, 0, 0)
    return pl.pallas_call(
        functools.partial(_ret_prompt_kernel, n_chunks=tr // R_CHUNK),
        grid=(batch, n_t),
        in_specs=[
            pl.BlockSpec(memory_space=pltpu.SMEM),
            pl.BlockSpec((tr, R_QK_WIDTH), lambda b, t: (row(b, t), OFF_RQ // R_QK_WIDTH)),
            pl.BlockSpec((tr, R_QK_WIDTH), lambda b, t: (row(b, t), OFF_RK // R_QK_WIDTH)),
            pl.BlockSpec((tr, R_V_WIDTH), lambda b, t: (row(b, t), OFF_RV // R_V_WIDTH)),
            pl.BlockSpec((tr, R_V_WIDTH), lambda b, t: (row(b, t), OFF_RG // R_V_WIDTH)),
            pl.BlockSpec((R_HEADS, R_CHUNK, R_CHUNK), const3),
            pl.BlockSpec((R_HEADS, R_CHUNK, R_QK_DIM), const3),
            pl.BlockSpec((R_HEADS, R_CHUNK, R_QK_DIM), const3),
            pl.BlockSpec((1, R_V_WIDTH), lambda b, t: (0, 0)),
        ],
        out_specs=[
            pl.BlockSpec((tr, R_V_WIDTH), lambda b, t: (row(b, t), 0)),
            pl.BlockSpec((1, R_HEADS, R_QK_DIM, R_V_DIM), lambda b, t: (b, 0, 0, 0)),
        ],
        out_shape=[
            jax.ShapeDtypeStruct((m, R_V_WIDTH), BF16),
            jax.ShapeDtypeStruct((batch, R_HEADS, R_QK_DIM, R_V_DIM), F32),
        ],
        scratch_shapes=[pltpu.VMEM((R_HEADS, R_QK_DIM, R_V_DIM), F32)],
        compiler_params=_params("arbitrary", "arbitrary"),
        name="retention_prompt",
    )(tabs["chunk_decay"], h_act, h_act, h_act, h_act, tabs["dmask"],
      jnp.broadcast_to(tabs["q_decay"], (R_HEADS, R_CHUNK, R_QK_DIM)),
      jnp.broadcast_to(tabs["k_decay"], (R_HEADS, R_CHUNK, R_QK_DIM)), gn)


def _ret_sample_kernel(cd_ref, q_ref, k_ref, v_ref, g_ref, s_ref, dmask_ref, qd_ref, kd_ref, gn_ref,
                       o_ref, snew_ref, *, bb, seq):
    k_all = k_ref[...].astype(F32)
    v_all = v_ref[...].astype(F32)
    for h in range(R_HEADS):
        qk = slice(h * R_QK_DIM, (h + 1) * R_QK_DIM)
        vv = slice(h * R_V_DIM, (h + 1) * R_V_DIM)
        q = q_ref[:, qk]
        sc = lax.dot_general(q, k_ref[:, qk], (((1,), (1,)), ((), ())),
                             preferred_element_type=F32) * dmask_ref[h]
        intra = jnp.dot(sc.astype(BF16), v_ref[:, vv], preferred_element_type=F32)
        q_scaled = q.astype(F32) * qd_ref[h]
        k_scaled = k_all[:, qk] * kd_ref[h]
        inter = []
        for b in range(bb):
            rows = slice(b * seq, (b + 1) * seq)
            s = s_ref[b, h]
            inter.append(jnp.dot(q_scaled[rows].astype(BF16), s.astype(BF16), preferred_element_type=F32))
            snew_ref[b, h] = s * cd_ref[h] + lax.dot_general(
                k_scaled[rows].astype(BF16), v_all[rows, vv].astype(BF16), (((0,), (0,)), ((), ())),
                preferred_element_type=F32)
        o = intra + jnp.concatenate(inter, axis=0)
        o_ref[:, vv] = _gated_groupnorm(o, gn_ref[:, vv], g_ref[:, vv])


def _retention_sample(h_act, state, tabs, gn, *, batch, seq):
    bb = 8
    m = batch * seq
    tr = bb * seq
    const3 = lambda i: (0, 0, 0)
    blk = jnp.arange(tr) // seq
    dmask = jnp.where((blk[:, None] == blk[None, :])[None], jnp.tile(tabs["dmask"], (1, bb, bb)), 0.0)
    q_decay = jnp.broadcast_to(jnp.tile(tabs["q_decay"], (1, bb, 1)), (R_HEADS, tr, R_QK_DIM))
    k_decay = jnp.broadcast_to(jnp.tile(tabs["k_decay"], (1, bb, 1)), (R_HEADS, tr, R_QK_DIM))
    return pl.pallas_call(
        functools.partial(_ret_sample_kernel, bb=bb, seq=seq),
        grid=(batch // bb,),
        in_specs=[
            pl.BlockSpec(memory_space=pltpu.SMEM),
            pl.BlockSpec((tr, R_QK_WIDTH), lambda i: (i, OFF_RQ // R_QK_WIDTH)),
            pl.BlockSpec((tr, R_QK_WIDTH), lambda i: (i, OFF_RK // R_QK_WIDTH)),
            pl.BlockSpec((tr, R_V_WIDTH), lambda i: (i, OFF_RV // R_V_WIDTH)),
            pl.BlockSpec((tr, R_V_WIDTH), lambda i: (i, OFF_RG // R_V_WIDTH)),
            pl.BlockSpec((bb, R_HEADS, R_QK_DIM, R_V_DIM), lambda i: (i, 0, 0, 0)),
            pl.BlockSpec((R_HEADS, tr, tr), const3),
            pl.BlockSpec((R_HEADS, tr, R_QK_DIM), const3),
            pl.BlockSpec((R_HEADS, tr, R_QK_DIM), const3),
            pl.BlockSpec((1, R_V_WIDTH), lambda i: (0, 0)),
        ],
        out_specs=[
            pl.BlockSpec((tr, R_V_WIDTH), lambda i: (i, 0)),
            pl.BlockSpec((bb, R_HEADS, R_QK_DIM, R_V_DIM), lambda i: (i, 0, 0, 0)),
        ],
        out_shape=[
            jax.ShapeDtypeStruct((m, R_V_WIDTH), BF16),
            jax.ShapeDtypeStruct((batch, R_HEADS, R_QK_DIM, R_V_DIM), F32),
        ],
        compiler_params=_params("arbitrary"),
        name="retention_sample",
    )(tabs["chunk_decay"], h_act, h_act, h_act, h_act, state, dmask, q_decay, k_decay, gn)


def _retention_tables(c):
    log_g = jnp.log1p(-jnp.exp2(-5.0 - jnp.arange(R_HEADS, dtype=F32)))
    idx = jnp.arange(c, dtype=F32)
    diff = idx[:, None] - idx[None, :]
    causal = diff >= 0
    dmask = jnp.where(causal[None], jnp.exp(jnp.where(causal, diff, 0.0)[None] * log_g[:, None, None]), 0.0)
    q_decay = jnp.exp((idx[:, None] + 1.0) * log_g[None, :])
    k_decay = jnp.exp((c - 1.0 - idx)[:, None] * log_g[None, :])
    chunk_decay = jnp.exp(c * log_g)
    return {
        "dmask": dmask,
        "q_decay": q_decay.T[:, :, None],
        "k_decay": k_decay.T[:, :, None],
        "chunk_decay": chunk_decay,
    }


def _softmax_rows(s):
    e = jnp.exp(s - jnp.max(s, axis=-1, keepdims=True))
    return e, jnp.sum(e, axis=-1, keepdims=True)


def _mem_prompt_kernel(q_ref, mk_ref, mv_ref, o_ref):
    for h in range(M_HEADS):
        cols = slice(h * M_HEAD_DIM, (h + 1) * M_HEAD_DIM)
        s = lax.dot_general(q_ref[:, cols], mk_ref[:, cols], (((1,), (1,)), ((), ())),
                            preferred_element_type=F32)
        e, l = _softmax_rows(s)
        o = jnp.dot(e.astype(BF16), mv_ref[:, cols], preferred_element_type=F32) / l
        o_ref[:, cols] = o.astype(BF16)


def _mem_attn_prompt(h_act, mk, mv, *, batch, seq):
    tq = 512
    n_t = seq // tq
    return pl.pallas_call(
        _mem_prompt_kernel,
        grid=(batch, n_t),
        in_specs=[
            pl.BlockSpec((tq, M_WIDTH), lambda b, t: (b * n_t + t, OFF_MQ // M_WIDTH)),
            pl.BlockSpec((N_MEM, M_WIDTH), lambda b, t: (b, 0)),
            pl.BlockSpec((N_MEM, M_WIDTH), lambda b, t: (b, 0)),
        ],
        out_specs=pl.BlockSpec((tq, M_WIDTH), lambda b, t: (b * n_t + t, 0)),
        out_shape=jax.ShapeDtypeStruct((batch * seq, M_WIDTH), BF16),
        compiler_params=_params("arbitrary", "arbitrary"),
        name="mem_attn_prompt",
    )(h_act, mk, mv)


MEM_SEQS_PER_STEP = 4


def _mem_attend_step(step, n_steps, q_ref, mk_hbm, mv_hbm, o_ref, kbuf, vbuf, sem, *, seq):
    bb = MEM_SEQS_PER_STEP
    pairs = [(b, h) for b in range(bb) for h in range(M_HEADS)]

    def copies(st, slot):
        out = []
        for b, h in pairs:
            out.append(pltpu.make_async_copy(mk_hbm.at[st * bb + b, :, h, :], kbuf.at[slot, b, h],
                                             sem.at[0, slot]))
            out.append(pltpu.make_async_copy(mv_hbm.at[st * bb + b, :, h, :], vbuf.at[slot, b, h],
                                             sem.at[1, slot]))
        return out

    slot = step % 2

    @pl.when(step == 0)
    def _():
        for c in copies(0, 0):
            c.start()

    @pl.when(step + 1 < n_steps)
    def _():
        for c in copies(step + 1, 1 - slot):
            c.start()

    @pl.when(step < n_steps)
    def _():
        for c in copies(step, slot):
            c.wait()
        q_all = q_ref[...].astype(F32)
        scores = []
        for b, h in pairs:
            q = q_all[b * seq:(b + 1) * seq, h * M_HEAD_DIM:(h + 1) * M_HEAD_DIM].astype(BF16)
            scores.append(lax.dot_general(q, kbuf[slot, b, h].astype(BF16), (((1,), (1,)), ((), ())),
                                          preferred_element_type=F32))
        e, l = _softmax_rows(jnp.concatenate(scores, axis=0))
        inv_l = 1.0 / l
        outs = []
        for n, (b, h) in enumerate(pairs):
            rows = slice(n * seq, (n + 1) * seq)
            o = jnp.dot(e[rows].astype(BF16), vbuf[slot, b, h].astype(BF16), preferred_element_type=F32)
            outs.append(o * inv_l[rows])
        out_rows = [jnp.concatenate(outs[b * M_HEADS:(b + 1) * M_HEADS], axis=1) for b in range(bb)]
        o_ref[...] = jnp.concatenate(out_rows, axis=0).astype(BF16)


MERGE_ROW_CHUNK = 256


def _merge_kernel(ga_ref, gb_ref, gc_ref, u_ref, va_ref, sw_ref, sb_ref, b_ref, c_ref,
                  wa_ref, wb_ref, wc_ref, o_ref, a_scr, *, tm):
    @pl.when(pl.program_id(1) == 0)
    def _():
        for r in range(tm // CHUNK):
            rows = slice(r * CHUNK, (r + 1) * CHUNK)
            for g in range(A_GROUPS):
                cols = slice(g * A_GROUP_DIM, (g + 1) * A_GROUP_DIM)
                z = jnp.dot(sw_ref[g], va_ref[rows, cols], preferred_element_type=F32) + sb_ref[g]
                a_scr[rows, cols] = (u_ref[rows, cols].astype(F32) * z).astype(BF16)

    for r in range(tm // MERGE_ROW_CHUNK):
        rows = slice(r * MERGE_ROW_CHUNK, (r + 1) * MERGE_ROW_CHUNK)
        ya = jnp.dot(a_scr[rows, :], wa_ref[...], preferred_element_type=F32)
        yb = jnp.dot(b_ref[rows, :], wb_ref[...], preferred_element_type=F32)
        yc = jnp.dot(c_ref[rows, :], wc_ref[...], preferred_element_type=F32)
        merged = (ga_ref[rows, :].astype(F32) * ya + gb_ref[rows, :].astype(F32) * yb
                  + gc_ref[rows, :].astype(F32) * yc)
        o_ref[rows, :] = merged.astype(BF16)


def _merge(h_act, va, b_act, c_act, sgu_w, sgu_b, wa, wb, wc, *, va_col_block):
    m = h_act.shape[0]
    tm, tn = 1024, 512
    n_j = D_MODEL // tn
    return pl.pallas_call(
        functools.partial(_merge_kernel, tm=tm),
        grid=(m // tm, n_j),
        in_specs=[
            pl.BlockSpec((tm, tn), lambda i, j: (i, j)),
            pl.BlockSpec((tm, tn), lambda i, j: (i, n_j + j)),
            pl.BlockSpec((tm, tn), lambda i, j: (i, 2 * n_j + j)),
            pl.BlockSpec((tm, A_WIDTH), lambda i, j: (i, OFF_AU // A_WIDTH)),
            pl.BlockSpec((tm, A_WIDTH), lambda i, j: (i, va_col_block)),
            pl.BlockSpec((A_GROUPS, CHUNK, CHUNK), lambda i, j: (0, 0, 0)),
            pl.BlockSpec((A_GROUPS, CHUNK, 1), lambda i, j: (0, 0, 0)),
            pl.BlockSpec((tm, R_V_WIDTH), lambda i, j: (i, 0)),
            pl.BlockSpec((tm, M_WIDTH), lambda i, j: (i, 0)),
            pl.BlockSpec((A_WIDTH, tn), lambda i, j: (0, j)),
            pl.BlockSpec((R_V_WIDTH, tn), lambda i, j: (0, j)),
            pl.BlockSpec((M_WIDTH, tn), lambda i, j: (0, j)),
        ],
        out_specs=pl.BlockSpec((tm, tn), lambda i, j: (i, j)),
        out_shape=jax.ShapeDtypeStruct((m, D_MODEL), BF16),
        scratch_shapes=[pltpu.VMEM((tm, A_WIDTH), BF16)],
        compiler_params=_params("arbitrary", "arbitrary"),
        name="merge",
    )(h_act, h_act, h_act, h_act, va, sgu_w, sgu_b, b_act, c_act, wa, wb, wc)


OUT_ROW_CHUNK = 256


def _out_ln_kernel(m_ref, w_ref, x_ref, g_ref, b_ref, o_ref):
    for r in range(m_ref.shape[0] // OUT_ROW_CHUNK):
        rows = slice(r * OUT_ROW_CHUNK, (r + 1) * OUT_ROW_CHUNK)
        y = jnp.dot(m_ref[rows, :], w_ref[...], preferred_element_type=F32)
        o_ref[rows, :] = _standardize(ALPHA * x_ref[rows, :] + y) * g_ref[...] + b_ref[...]


def _out_ln(merged, w_out, x, ln_g, ln_b):
    m = x.shape[0]
    tm = 1024
    return pl.pallas_call(
        _out_ln_kernel,
        grid=(m // tm,),
        in_specs=[
            pl.BlockSpec((tm, D_MODEL), lambda i: (i, 0)),
            pl.BlockSpec((D_MODEL, D_MODEL), lambda i: (0, 0), pipeline_mode=pl.Buffered(1)),
            pl.BlockSpec((tm, D_MODEL), lambda i: (i, 0)),
            pl.BlockSpec((1, D_MODEL), lambda i: (0, 0)),
            pl.BlockSpec((1, D_MODEL), lambda i: (0, 0)),
        ],
        out_specs=pl.BlockSpec((tm, D_MODEL), lambda i: (i, 0)),
        out_shape=jax.ShapeDtypeStruct((m, D_MODEL), F32),
        compiler_params=_params("arbitrary"),
        name="out_ln",
    )(merged, w_out, x, ln_g, ln_b)


def _ffn_kernel(x_ref, wg_ref, wu_ref, wd_ref, g_ref, b_ref, o_ref, *rest, first_pass):
    if first_pass:
        wgb_ref, wub_ref, wdb_ref, xb_ref, acc_ref = rest
        wgb_ref[...] = wg_ref[...].astype(BF16)
        wub_ref[...] = wu_ref[...].astype(BF16)
        wdb_ref[...] = wd_ref[...].astype(BF16)
    else:
        xb_ref, acc_ref = rest
        wgb_ref, wub_ref, wdb_ref = wg_ref, wu_ref, wd_ref
    f = pl.program_id(1)

    @pl.when(f == 0)
    def _():
        xb_ref[...] = x_ref[...].astype(BF16)
        acc_ref[...] = jnp.zeros_like(acc_ref)

    xb = xb_ref[...]
    gate = jnp.dot(xb, wgb_ref[...], preferred_element_type=F32)
    up = jnp.dot(xb, wub_ref[...], preferred_element_type=F32)
    act = (jax.nn.silu(gate) * up).astype(BF16)
    acc_ref[...] += jnp.dot(act, wdb_ref[...], preferred_element_type=F32)

    @pl.when(f == pl.num_programs(1) - 1)
    def _():
        o_ref[...] = _standardize(ALPHA * x_ref[...] + acc_ref[...]) * g_ref[...] + b_ref[...]


def _ffn(x1, wg, wu, wd, ln_g, ln_b, *, tm, tf, first_pass):
    m = x1.shape[0]
    one_buffer = pl.Buffered(1) if first_pass else None
    out_shape = [jax.ShapeDtypeStruct((m, D_MODEL), F32)]
    out_specs = [pl.BlockSpec((tm, D_MODEL), lambda i, f: (i, 0), pipeline_mode=one_buffer)]
    if first_pass:
        assert m == tm
        out_shape += [jax.ShapeDtypeStruct(w.shape, BF16) for w in (wg, wu, wd)]
        out_specs += [pl.BlockSpec((D_MODEL, tf), lambda i, f: (0, f)),
                      pl.BlockSpec((D_MODEL, tf), lambda i, f: (0, f)),
                      pl.BlockSpec((tf, D_MODEL), lambda i, f: (f, 0))]
    return pl.pallas_call(
        functools.partial(_ffn_kernel, first_pass=first_pass),
        grid=(m // tm, D_FF // tf),
        in_specs=[
            pl.BlockSpec((tm, D_MODEL), lambda i, f: (i, 0), pipeline_mode=one_buffer),
            pl.BlockSpec((D_MODEL, tf), lambda i, f: (0, f)),
            pl.BlockSpec((D_MODEL, tf), lambda i, f: (0, f)),
            pl.BlockSpec((tf, D_MODEL), lambda i, f: (f, 0)),
            pl.BlockSpec((1, D_MODEL), lambda i, f: (0, 0)),
            pl.BlockSpec((1, D_MODEL), lambda i, f: (0, 0)),
        ],
        out_specs=out_specs,
        out_shape=out_shape,
        scratch_shapes=[pltpu.VMEM((tm, D_MODEL), BF16), pltpu.VMEM((tm, D_MODEL), F32)],
        compiler_params=_params("arbitrary", "arbitrary"),
        name="ffn",
    )(x1, wg, wu, wd, ln_g, ln_b)


TAIL_ROW_CHUNK = 256


def _tail_kernel(m_ref, wo_ref, x_ref, g1_ref, b1_ref, wg_ref, wu_ref, wd_ref, g2_ref, b2_ref,
                 o_ref, x1_scr, x1b_scr):
    f = pl.program_id(1)
    n_f = pl.num_programs(1)
    chunks = [slice(r * TAIL_ROW_CHUNK, (r + 1) * TAIL_ROW_CHUNK)
              for r in range(m_ref.shape[0] // TAIL_ROW_CHUNK)]

    def swiglu(rows):
        xb = x1b_scr[rows, :]
        gate = jnp.dot(xb, wg_ref[...], preferred_element_type=F32)
        up = jnp.dot(xb, wu_ref[...], preferred_element_type=F32)
        act = (jax.nn.silu(gate) * up).astype(BF16)
        return jnp.dot(act, wd_ref[...], preferred_element_type=F32)

    @pl.when(f == 0)
    def _():
        for rows in chunks:
            y = jnp.dot(m_ref[rows, :], wo_ref[...], preferred_element_type=F32)
            x1 = _standardize(ALPHA * x_ref[rows, :] + y) * g1_ref[...] + b1_ref[...]
            x1_scr[rows, :] = x1
            x1b_scr[rows, :] = x1.astype(BF16)
        for rows in chunks:
            o_ref[rows, :] = swiglu(rows)

    @pl.when((f > 0) & (f < n_f - 1))
    def _():
        o_ref[...] += swiglu(slice(None))

    @pl.when(f == n_f - 1)
    def _():
        for rows in chunks:
            ff = o_ref[rows, :] + swiglu(rows)
            o_ref[rows, :] = _standardize(ALPHA * x1_scr[rows, :] + ff) * g2_ref[...] + b2_ref[...]


def _tail(merged, w_out, x, ln1_g, ln1_b, wg, wu, wd, ln2_g, ln2_b):
    m = x.shape[0]
    tm, tf = 512, 512
    assert D_FF // tf >= 2
    row = lambda i, f: (i, 0)
    const = lambda i, f: (0, 0)
    return pl.pallas_call(
        _tail_kernel,
        grid=(m // tm, D_FF // tf),
        in_specs=[
            pl.BlockSpec((tm, D_MODEL), row),
            pl.BlockSpec((D_MODEL, D_MODEL), const, pipeline_mode=pl.Buffered(1)),
            pl.BlockSpec((tm, D_MODEL), row),
            pl.BlockSpec((1, D_MODEL), const),
            pl.BlockSpec((1, D_MODEL), const),
            pl.BlockSpec((D_MODEL, tf), lambda i, f: (0, f)),
            pl.BlockSpec((D_MODEL, tf), lambda i, f: (0, f)),
            pl.BlockSpec((tf, D_MODEL), lambda i, f: (f, 0)),
            pl.BlockSpec((1, D_MODEL), const),
            pl.BlockSpec((1, D_MODEL), const),
        ],
        out_specs=pl.BlockSpec((tm, D_MODEL), row),
        out_shape=jax.ShapeDtypeStruct((m, D_MODEL), F32),
        scratch_shapes=[pltpu.VMEM((tm, D_MODEL), F32), pltpu.VMEM((tm, D_MODEL), BF16)],
        compiler_params=_params("arbitrary", "arbitrary"),
        name="tail",
    )(merged, w_out, x, ln1_g, ln1_b, wg, wu, wd, ln2_g, ln2_b)


def _rotary_tables(pos):
    half = R_QK_DIM // 2
    inv = ROPE_BASE ** (-jnp.arange(half, dtype=F32) / half)
    ang = pos[:, None] * inv[None, :]
    cos = jnp.cos(ang)
    sin = jnp.sin(ang)
    cos2 = jnp.concatenate([cos, cos], axis=-1)
    sin2 = jnp.concatenate([-sin, sin], axis=-1)
    k_scale = R_QK_DIM ** -0.5
    return jnp.stack([cos2, cos2 * k_scale]), jnp.stack([sin2, sin2 * k_scale])


def _sgu_tables(sgu_w, sgu_b, c):
    w = jnp.tril(sgu_w[:, :c, :c])
    b = sgu_b[:, :c]
    reps = CHUNK // c
    if reps > 1:
        blk = jnp.arange(CHUNK) // c
        w = jnp.where((blk[:, None] == blk[None, :])[None], jnp.tile(w, (1, reps, reps)), 0.0)
        b = jnp.tile(b, (1, reps))
    return w.astype(BF16), b[:, :, None]


def kernel(x_prompt, x_sample, mem_prompt, state_ret, cache_mem_k, cache_mem_v, w_in, sgu_ln_g, sgu_ln_b, sgu_w, sgu_b, w_proj_a, ret_gn_g, w_proj_b, w_mem_k, w_mem_v, w_proj_c, w_out, ln1_g, ln1_b, w_ffn_gate, w_ffn_up, w_ffn_down, ln2_g, ln2_b):
    assert w_in.shape[0] == DEPTH == 1
    bp, seq, _ = x_prompt.shape
    bs, dseq, _ = x_sample.shape
    l = 0
    bf = lambda w: w[l].astype(BF16)
    wa_b, wb_b, wc_b, wo_b = bf(w_proj_a), bf(w_proj_b), bf(w_proj_c), bf(w_out)
    wmk_b, wmv_b = bf(w_mem_k), bf(w_mem_v)
    ln_g = sgu_ln_g[l].reshape(1, A_WIDTH)
    ln_b = sgu_ln_b[l].reshape(1, A_WIDTH)
    gn = ret_gn_g[l].reshape(1, R_V_WIDTH)
    ln1g, ln1b = ln1_g[l].reshape(1, D_MODEL), ln1_b[l].reshape(1, D_MODEL)
    ln2g, ln2b = ln2_g[l].reshape(1, D_MODEL), ln2_b[l].reshape(1, D_MODEL)

    def merged(h_act, b_act, c_act, sgu_tabs):
        return _merge(h_act, h_act, b_act, c_act, sgu_tabs[0], sgu_tabs[1], wa_b, wb_b, wc_b,
                      va_col_block=OFF_AV // A_WIDTH)

    m_s = bs * dseq
    xs = x_sample.reshape(m_s, D_MODEL)
    cos_s, sin_s = _rotary_tables(float(PAST_LEN) + jnp.arange(dseq, dtype=F32))
    cos_s, sin_s = jnp.tile(cos_s, (1, bs, 1)), jnp.tile(sin_s, (1, bs, 1))
    h_s, va_s, w_in_b = _in_proj(xs, w_in[l], cos_s, sin_s, ln_g, ln_b, tm=m_s)
    b_s, s_s = _retention_sample(h_s, state_ret[l], _retention_tables(min(dseq, R_CHUNK)), gn,
                                 batch=bs, seq=dseq)

    xp = x_prompt.reshape(bp * seq, D_MODEL)
    cos_p, sin_p = _rotary_tables(0.0 + jnp.arange(seq, dtype=F32))
    h_p, c_s = _in_proj(xp, w_in_b, cos_p, sin_p, ln_g, ln_b, tm=1024,
                        side=(h_s, cache_mem_k[l], cache_mem_v[l], dseq))

    m_s_merged = merged(h_s, b_s, c_s, _sgu_tables(sgu_w[l], sgu_b[l], min(dseq, CHUNK)))
    x1_s = _out_ln(m_s_merged, wo_b, xs, ln1g, ln1b)
    y_s, wg_b, wu_b, wd_b = _ffn(x1_s, w_ffn_gate[l], w_ffn_up[l], w_ffn_down[l], ln2g, ln2b,
                                 tm=m_s, tf=256, first_pass=True)

    memp = mem_prompt.reshape(bp * N_MEM, D_MODEL)
    mk_p, mk_pb = _mem_kv(memp, wmk_b)
    mv_p, mv_pb = _mem_kv(memp, wmv_b)
    b_p, s_p = _retention_prompt(h_p, _retention_tables(min(seq, R_CHUNK)), gn, batch=bp, seq=seq)
    c_p = _mem_attn_prompt(h_p, mk_pb, mv_pb, batch=bp, seq=seq)
    m_p_merged = merged(h_p, b_p, c_p, _sgu_tables(sgu_w[l], sgu_b[l], min(seq, CHUNK)))
    y_p = _tail(m_p_merged, wo_b, xp, ln1g, ln1b, wg_b, wu_b, wd_b, ln2g, ln2b)

    return (
        y_p.reshape(bp, seq, D_MODEL),
        y_s.reshape(bs, dseq, D_MODEL),
        s_p[None],
        mk_p.reshape(1, bp, N_MEM, M_HEADS, M_HEAD_DIM),
        mv_p.reshape(1, bp, N_MEM, M_HEADS, M_HEAD_DIM),
        s_s[None],
        va_s.reshape(1, bs, dseq, A_GROUPS, A_GROUP_DIM),
    )
```

```python
import functools
import math

import jax
import jax.numpy as jnp
from jax import lax
from jax.experimental import pallas as pl
from jax.experimental.pallas import tpu as pltpu

F32 = jnp.float32
BF16 = jnp.bfloat16

D_MODEL = 2048
DEPTH = 1
PAST_LEN = 16384
CHUNK = 128
A_GROUPS = 4
A_GROUP_DIM = 256
A_WIDTH = A_GROUPS * A_GROUP_DIM
R_HEADS = 8
R_QK_DIM = 128
R_V_DIM = 256
R_QK_WIDTH = R_HEADS * R_QK_DIM
R_V_WIDTH = R_HEADS * R_V_DIM
R_CHUNK = 128
N_MEM = 256
M_HEADS = 4
M_HEAD_DIM = 256
M_WIDTH = M_HEADS * M_HEAD_DIM
N_BRANCH = 3
D_FF = -(-8 * D_MODEL // (3 * 256)) * 256
ALPHA = (2.0 * DEPTH) ** 0.25
ROPE_BASE = 10000.0
LN_EPS = 1e-5

OFF_AU = N_BRANCH * D_MODEL
OFF_AV = OFF_AU + A_WIDTH
OFF_RQ = OFF_AV + A_WIDTH
OFF_RK = OFF_RQ + R_QK_WIDTH
OFF_RV = OFF_RK + R_QK_WIDTH
OFF_RG = OFF_RV + R_V_WIDTH
OFF_MQ = OFF_RG + R_V_WIDTH
IN_WIDTH = OFF_MQ + M_WIDTH

VMEM_LIMIT_BYTES = 56 * 1024 * 1024
LANES = 128


def _params(*sem):
    return pltpu.CompilerParams(dimension_semantics=sem, vmem_limit_bytes=VMEM_LIMIT_BYTES)


def _standardize(x):
    mu = jnp.mean(x, axis=-1, keepdims=True)
    xc = x - mu
    var = jnp.mean(xc * xc, axis=-1, keepdims=True)
    return xc * lax.rsqrt(var + LN_EPS)


IN_TN = 1024
IN_ROW_CHUNK = 256
_J_AU = OFF_AU // IN_TN
_J_AV = OFF_AV // IN_TN
_J_RQ = OFF_RQ // IN_TN
_J_RK = OFF_RK // IN_TN
_J_RV = OFF_RV // IN_TN
_J_RG = OFF_RG // IN_TN
_J_MQ = OFF_MQ // IN_TN
_J_END = IN_WIDTH // IN_TN


def _in_proj_kernel(x_ref, w_ref, cos_ref, sin_ref, lng_ref, lnb_ref, *rest, first_pass, side_seq,
                    side_steps):
    tm = x_ref.shape[0]
    j = pl.program_id(1)
    if first_pass:
        o_ref, va_ref, wb_ref, xb_ref = rest
        wb_ref[...] = w_ref[...].astype(BF16)
    else:
        qs_ref, mk_hbm, mv_hbm, o_ref, cs_ref, xb_ref, kbuf, vbuf, sem = rest
        wb_ref = w_ref

    @pl.when(j == 0)
    def _():
        xb_ref[...] = x_ref[...].astype(BF16)

    def column_range(pred, epilogue, with_side_job=False):
        @pl.when(pred)
        def _():
            if with_side_job:
                _mem_attend_step(pl.program_id(0) * side_steps + j, pl.num_programs(0) * side_steps,
                                 qs_ref, mk_hbm, mv_hbm, cs_ref, kbuf, vbuf, sem, seq=side_seq)
            for r in range(tm // IN_ROW_CHUNK):
                rows = slice(r * IN_ROW_CHUNK, (r + 1) * IN_ROW_CHUNK)
                acc = jnp.dot(xb_ref[rows, :], wb_ref[...], preferred_element_type=F32)
                epilogue(rows, acc)

    def store(fn):
        def epilogue(rows, acc):
            o_ref[rows, :] = fn(acc).astype(BF16)
        return epilogue

    def gelu_layernorm(rows, acc):
        v = jax.nn.gelu(acc)
        for g in range(IN_TN // A_GROUP_DIM):
            cols = slice(g * A_GROUP_DIM, (g + 1) * A_GROUP_DIM)
            vn = _standardize(v[:, cols]) * lng_ref[:, cols] + lnb_ref[:, cols]
            o_ref[rows, cols] = vn.astype(BF16)
            if first_pass:
                va_ref[rows, cols] = vn

    def rotary(rows, acc):
        for hh in range(IN_TN // R_QK_DIM):
            cols = slice(hh * R_QK_DIM, (hh + 1) * R_QK_DIM)
            xh = acc[:, cols]
            r = xh * cos_ref[rows, :] + pltpu.roll(xh, R_QK_DIM // 2, 1) * sin_ref[rows, :]
            o_ref[rows, cols] = r.astype(BF16)

    is_mq = j >= _J_MQ
    raw_scale = jnp.where(is_mq, M_HEAD_DIM ** -0.5, 1.0).astype(F32)
    if first_pass:
        column_range(j < _J_AU, store(jax.nn.sigmoid))
    else:
        column_range(j < side_steps, store(jax.nn.sigmoid), with_side_job=True)
        column_range((j >= side_steps) & (j < _J_AU), store(jax.nn.sigmoid))
    column_range((j >= _J_AU) & (j < _J_AV), store(jax.nn.gelu))
    column_range((j >= _J_AV) & (j < _J_RQ), gelu_layernorm)
    column_range((j >= _J_RQ) & (j < _J_RV), rotary)
    column_range(((j >= _J_RV) & (j < _J_RG)) | is_mq, store(lambda acc: acc * raw_scale))
    column_range((j >= _J_RG) & (j < _J_MQ), store(jax.nn.silu))


def _in_proj(x, w_in, cos_t, sin_t, ln_g, ln_b, *, tm, side=None):
    m = x.shape[0]
    first_pass = side is None
    n_pos_blocks = cos_t.shape[1] // tm
    assert A_WIDTH == IN_TN and R_QK_WIDTH == IN_TN
    table_idx = lambda i, j: (jnp.where(j == _J_RK, 1, 0), i % n_pos_blocks, 0)
    grid = (m // tm, _J_END)

    operands = [x, w_in, cos_t, sin_t, ln_g, ln_b]
    in_specs = [
        pl.BlockSpec((tm, D_MODEL), lambda i, j: (i, 0),
                     pipeline_mode=pl.Buffered(1) if first_pass else None),
        pl.BlockSpec((D_MODEL, IN_TN), lambda i, j: (0, j)),
        pl.BlockSpec((None, tm, R_QK_DIM), table_idx),
        pl.BlockSpec((None, tm, R_QK_DIM), table_idx),
        pl.BlockSpec((1, A_WIDTH), lambda i, j: (0, 0)),
        pl.BlockSpec((1, A_WIDTH), lambda i, j: (0, 0)),
    ]
    out_shape = [jax.ShapeDtypeStruct((m, IN_WIDTH), BF16)]
    out_specs = [pl.BlockSpec((tm, IN_TN), lambda i, j: (i, j))]
    scratch_shapes = [pltpu.VMEM((tm, D_MODEL), BF16)]
    side_seq = side_steps = None
    if first_pass:
        assert m == tm
        out_shape += [jax.ShapeDtypeStruct((m, A_WIDTH), F32), jax.ShapeDtypeStruct(w_in.shape, BF16)]
        out_specs += [pl.BlockSpec((tm, A_WIDTH), lambda i, j: (i, 0)),
                      pl.BlockSpec((D_MODEL, IN_TN), lambda i, j: (0, j))]
    else:
        h_side, mem_k, mem_v, side_seq = side
        side_rows = MEM_SEQS_PER_STEP * side_seq
        side_steps, rem = divmod(mem_k.shape[0], MEM_SEQS_PER_STEP * grid[0])
        assert rem == 0 and 0 < side_steps <= _J_AU
        side_block = lambda i, j: i * side_steps + jnp.minimum(j, side_steps - 1)
        operands += [h_side, mem_k, mem_v]
        in_specs += [pl.BlockSpec((side_rows, M_WIDTH), lambda i, j: (side_block(i, j), OFF_MQ // M_WIDTH)),
                     pl.BlockSpec(memory_space=pl.ANY),
                     pl.BlockSpec(memory_space=pl.ANY)]
        out_shape.append(jax.ShapeDtypeStruct((h_side.shape[0], M_WIDTH), BF16))
        out_specs.append(pl.BlockSpec((side_rows, M_WIDTH), lambda i, j: (side_block(i, j), 0)))
        scratch_shapes += [pltpu.VMEM((2, MEM_SEQS_PER_STEP, M_HEADS, N_MEM, M_HEAD_DIM), F32),
                           pltpu.VMEM((2, MEM_SEQS_PER_STEP, M_HEADS, N_MEM, M_HEAD_DIM), F32),
                           pltpu.SemaphoreType.DMA((2, 2))]
    return pl.pallas_call(
        functools.partial(_in_proj_kernel, first_pass=first_pass, side_seq=side_seq, side_steps=side_steps),
        grid=grid,
        in_specs=in_specs,
        out_specs=out_specs,
        out_shape=out_shape,
        scratch_shapes=scratch_shapes,
        compiler_params=_params("arbitrary", "arbitrary"),
        name="in_proj",
    )(*operands)


def _matmul_kernel(x_ref, w_ref, o_ref, ob_ref):
    acc = jnp.dot(x_ref[...].astype(BF16), w_ref[...], preferred_element_type=F32)
    o_ref[...] = acc
    ob_ref[...] = acc.astype(BF16)


def _mem_kv(x, w):
    m, k = x.shape
    n = w.shape[1]
    tn = 512
    return pl.pallas_call(
        _matmul_kernel,
        grid=(n // tn,),
        in_specs=[pl.BlockSpec((m, k), lambda j: (0, 0)), pl.BlockSpec((k, tn), lambda j: (0, j))],
        out_specs=[pl.BlockSpec((m, tn), lambda j: (0, j)), pl.BlockSpec((m, tn), lambda j: (0, j))],
        out_shape=[jax.ShapeDtypeStruct((m, n), F32), jax.ShapeDtypeStruct((m, n), BF16)],
        compiler_params=_params("arbitrary"),
        name="mem_kv",
    )(x, w)


def _gated_groupnorm(o, gn, g):
    return (g.astype(F32) * (_standardize(o) * gn)).astype(BF16)


def _ret_long_step(cd_ref, q_ref, k_ref, v_ref, g_ref, dmask_ref, qd_ref, kd_ref, gn_ref,
                   o_ref, sfin_ref, s_scr, *, n_chunks):
    t = pl.program_id(1)

    @pl.when(t == 0)
    def _():
        s_scr[...] = jnp.zeros_like(s_scr)

    state = [s_scr[h] for h in range(R_HEADS)]
    for c in range(n_chunks):
        rows = slice(c * R_CHUNK, (c + 1) * R_CHUNK)
        for h in range(R_HEADS):
            qk = slice(h * R_QK_DIM, (h + 1) * R_QK_DIM)
            vv = slice(h * R_V_DIM, (h + 1) * R_V_DIM)
            q = q_ref[rows, qk]
            k = k_ref[rows, qk]
            v = v_ref[rows, vv]
            s = state[h]
            sc = lax.dot_general(q, k, (((1,), (1,)), ((), ())), preferred_element_type=F32) * dmask_ref[h]
            lhs = jnp.concatenate([sc.astype(BF16), (q.astype(F32) * qd_ref[h]).astype(BF16)], axis=1)
            rhs = jnp.concatenate([v, s.astype(BF16)], axis=0)
            o = jnp.dot(lhs, rhs, preferred_element_type=F32)
            kd = (k.astype(F32) * kd_ref[h]).astype(BF16)
            state[h] = s * cd_ref[h] + lax.dot_general(kd, v, (((0,), (0,)), ((), ())),
                                                       preferred_element_type=F32)
            o_ref[rows, vv] = _gated_groupnorm(o, gn_ref[:, vv], g_ref[rows, vv])
    for h in range(R_HEADS):
        s_scr[h] = state[h]

    @pl.when(t == pl.num_programs(1) - 1)
    def _():
        sfin_ref[0] = s_scr[...]


def _ret_short_step(cd_ref, q_ref, k_ref, v_ref, g_ref, s_ref, dmask_ref, qd_ref, kd_ref, gn_ref,
                    o_ref, snew_ref, *, bb, seq):
    k_all = k_ref[...].astype(F32)
    v_all = v_ref[...].astype(F32)
    for h in range(R_HEADS):
        qk = slice(h * R_QK_DIM, (h + 1) * R_QK_DIM)
        vv = slice(h * R_V_DIM, (h + 1) * R_V_DIM)
        q = q_ref[:, qk]
        sc = lax.dot_general(q, k_ref[:, qk], (((1,), (1,)), ((), ())),
                             preferred_element_type=F32) * dmask_ref[h]
        intra = jnp.dot(sc.astype(BF16), v_ref[:, vv], preferred_element_type=F32)
        q_scaled = q.astype(F32) * qd_ref[h]
        k_scaled = k_all[:, qk] * kd_ref[h]
        inter = []
        for b in range(bb):
            rows = slice(b * seq, (b + 1) * seq)
            s = s_ref[b, h]
            inter.append(jnp.dot(q_scaled[rows].astype(BF16), s.astype(BF16), preferred_element_type=F32))
            snew_ref[b, h] = s * cd_ref[h] + lax.dot_general(
                k_scaled[rows].astype(BF16), v_all[rows, vv].astype(BF16), (((0,), (0,)), ((), ())),
                preferred_element_type=F32)
        o = intra + jnp.concatenate(inter, axis=0)
        o_ref[:, vv] = _gated_groupnorm(o, gn_ref[:, vv], g_ref[:, vv])


def _retention_kernel(cdl_ref, cds_ref, ql_ref, kl_ref, vl_ref, gl_ref, dml_ref, qdl_ref, kdl_ref,
                      qs_ref, ks_ref, vs_ref, gs_ref, s_ref, dms_ref, qds_ref, kds_ref, gn_ref,
                      ol_ref, sfin_ref, os_ref, snew_ref, s_scr, *, n_chunks, bb, seq):
    _ret_long_step(cdl_ref, ql_ref, kl_ref, vl_ref, gl_ref, dml_ref, qdl_ref, kdl_ref, gn_ref,
                   ol_ref, sfin_ref, s_scr, n_chunks=n_chunks)
    _ret_short_step(cds_ref, qs_ref, ks_ref, vs_ref, gs_ref, s_ref, dms_ref, qds_ref, kds_ref, gn_ref,
                    os_ref, snew_ref, bb=bb, seq=seq)


def _retention(h_long, h_short, state, gn, *, batch_l, seq_l, batch_s, seq_s):
    tr = 256
    n_t = seq_l // tr
    n_steps = batch_l * n_t
    bb, rem = divmod(batch_s, n_steps)
    assert rem == 0 and (bb * seq_s) % 16 == 0
    ts = bb * seq_s
    tabs_l = _retention_tables(min(seq_l, R_CHUNK))
    tabs_s = _retention_tables(min(seq_s, R_CHUNK))
    lane_bcast = lambda t, rows: jnp.broadcast_to(t, (R_HEADS, rows, R_QK_DIM))
    blk = jnp.arange(ts) // seq_s
    dmask_s = jnp.where((blk[:, None] == blk[None, :])[None], jnp.tile(tabs_s["dmask"], (1, bb, bb)), 0.0)
    qd_s = lane_bcast(jnp.tile(tabs_s["q_decay"], (1, bb, 1)), ts)
    kd_s = lane_bcast(jnp.tile(tabs_s["k_decay"], (1, bb, 1)), ts)

    step = lambda b, t: b * n_t + t
    const3 = lambda b, t: (0, 0, 0)
    smem = pl.BlockSpec(memory_space=pltpu.SMEM)
    state_spec = pl.BlockSpec((bb, R_HEADS, R_QK_DIM, R_V_DIM), lambda b, t: (step(b, t), 0, 0, 0))
    return pl.pallas_call(
        functools.partial(_retention_kernel, n_chunks=tr // R_CHUNK, bb=bb, seq=seq_s),
        grid=(batch_l, n_t),
        in_specs=[
            smem, smem,
            pl.BlockSpec((tr, R_QK_WIDTH), lambda b, t: (step(b, t), OFF_RQ // R_QK_WIDTH)),
            pl.BlockSpec((tr, R_QK_WIDTH), lambda b, t: (step(b, t), OFF_RK // R_QK_WIDTH)),
            pl.BlockSpec((tr, R_V_WIDTH), lambda b, t: (step(b, t), OFF_RV // R_V_WIDTH)),
            pl.BlockSpec((tr, R_V_WIDTH), lambda b, t: (step(b, t), OFF_RG // R_V_WIDTH)),
            pl.BlockSpec((R_HEADS, R_CHUNK, R_CHUNK), const3),
            pl.BlockSpec((R_HEADS, R_CHUNK, R_QK_DIM), const3),
            pl.BlockSpec((R_HEADS, R_CHUNK, R_QK_DIM), const3),
            pl.BlockSpec((ts, R_QK_WIDTH), lambda b, t: (step(b, t), OFF_RQ // R_QK_WIDTH)),
            pl.BlockSpec((ts, R_QK_WIDTH), lambda b, t: (step(b, t), OFF_RK // R_QK_WIDTH)),
            pl.BlockSpec((ts, R_V_WIDTH), lambda b, t: (step(b, t), OFF_RV // R_V_WIDTH)),
            pl.BlockSpec((ts, R_V_WIDTH), lambda b, t: (step(b, t), OFF_RG // R_V_WIDTH)),
            state_spec,
            pl.BlockSpec((R_HEADS, ts, ts), const3),
            pl.BlockSpec((R_HEADS, ts, R_QK_DIM), const3),
            pl.BlockSpec((R_HEADS, ts, R_QK_DIM), const3),
            pl.BlockSpec((1, R_V_WIDTH), lambda b, t: (0, 0)),
        ],
        out_specs=[
            pl.BlockSpec((tr, R_V_WIDTH), lambda b, t: (step(b, t), 0)),
            pl.BlockSpec((1, R_HEADS, R_QK_DIM, R_V_DIM), lambda b, t: (b, 0, 0, 0)),
            pl.BlockSpec((ts, R_V_WIDTH), lambda b, t: (step(b, t), 0)),
            state_spec,
        ],
        out_shape=[
            jax.ShapeDtypeStruct((batch_l * seq_l, R_V_WIDTH), BF16),
            jax.ShapeDtypeStruct((batch_l, R_HEADS, R_QK_DIM, R_V_DIM), F32),
            jax.ShapeDtypeStruct((batch_s * seq_s, R_V_WIDTH), BF16),
            jax.ShapeDtypeStruct((batch_s, R_HEADS, R_QK_DIM, R_V_DIM), F32),
        ],
        scratch_shapes=[pltpu.VMEM((R_HEADS, R_QK_DIM, R_V_DIM), F32)],
        compiler_params=_params("arbitrary", "arbitrary"),
        name="retention",
    )(tabs_l["chunk_decay"], tabs_s["chunk_decay"], h_long, h_long, h_long, h_long, tabs_l["dmask"],
      lane_bcast(tabs_l["q_decay"], R_CHUNK), lane_bcast(tabs_l["k_decay"], R_CHUNK),
      h_short, h_short, h_short, h_short, state, dmask_s, qd_s, kd_s, gn)


def _retention_tables(c):
    log_g = jnp.log1p(-jnp.exp2(-5.0 - jnp.arange(R_HEADS, dtype=F32)))
    idx = jnp.arange(c, dtype=F32)
    diff = idx[:, None] - idx[None, :]
    causal = diff >= 0
    dmask = jnp.where(causal[None], jnp.exp(jnp.where(causal, diff, 0.0)[None] * log_g[:, None, None]), 0.0)
    q_decay = jnp.exp((idx[:, None] + 1.0) * log_g[None, :])
    k_decay = jnp.exp((c - 1.0 - idx)[:, None] * log_g[None, :])
    chunk_decay = jnp.exp(c * log_g)
    return {
        "dmask": dmask,
        "q_decay": q_decay.T[:, :, None],
        "k_decay": k_decay.T[:, :, None],
        "chunk_decay": chunk_decay,
    }


def _softmax_rows(s):
    e = jnp.exp(s - jnp.max(s, axis=-1, keepdims=True))
    return e, jnp.sum(e, axis=-1, keepdims=True)


def _mem_prompt_kernel(q_ref, mk_ref, mv_ref, o_ref):
    for h in range(M_HEADS):
        cols = slice(h * M_HEAD_DIM, (h + 1) * M_HEAD_DIM)
        s = lax.dot_general(q_ref[:, cols], mk_ref[:, cols], (((1,), (1,)), ((), ())),
                            preferred_element_type=F32)
        e, l = _softmax_rows(s)
        o = jnp.dot(e.astype(BF16), mv_ref[:, cols], preferred_element_type=F32) / l
        o_ref[:, cols] = o.astype(BF16)


def _mem_attn_prompt(h_act, mk, mv, *, batch, seq):
    tq = 512
    n_t = seq // tq
    return pl.pallas_call(
        _mem_prompt_kernel,
        grid=(batch, n_t),
        in_specs=[
            pl.BlockSpec((tq, M_WIDTH), lambda b, t: (b * n_t + t, OFF_MQ // M_WIDTH)),
            pl.BlockSpec((N_MEM, M_WIDTH), lambda b, t: (b, 0)),
            pl.BlockSpec((N_MEM, M_WIDTH), lambda b, t: (b, 0)),
        ],
        out_specs=pl.BlockSpec((tq, M_WIDTH), lambda b, t: (b * n_t + t, 0)),
        out_shape=jax.ShapeDtypeStruct((batch * seq, M_WIDTH), BF16),
        compiler_params=_params("arbitrary", "arbitrary"),
        name="mem_attn_prompt",
    )(h_act, mk, mv)


MEM_SEQS_PER_STEP = 4


def _mem_attend_step(step, n_steps, q_ref, mk_hbm, mv_hbm, o_ref, kbuf, vbuf, sem, *, seq):
    bb = MEM_SEQS_PER_STEP
    pairs = [(b, h) for b in range(bb) for h in range(M_HEADS)]

    def copies(st, slot):
        out = []
        for b, h in pairs:
            out.append(pltpu.make_async_copy(mk_hbm.at[st * bb + b, :, h, :], kbuf.at[slot, b, h],
                                             sem.at[0, slot]))
            out.append(pltpu.make_async_copy(mv_hbm.at[st * bb + b, :, h, :], vbuf.at[slot, b, h],
                                             sem.at[1, slot]))
        return out

    slot = step % 2

    @pl.when(step == 0)
    def _():
        for c in copies(0, 0):
            c.start()

    @pl.when(step + 1 < n_steps)
    def _():
        for c in copies(step + 1, 1 - slot):
            c.start()

    for c in copies(step, slot):
        c.wait()
    q_all = q_ref[...].astype(F32)
    scores = []
    for b, h in pairs:
        q = q_all[b * seq:(b + 1) * seq, h * M_HEAD_DIM:(h + 1) * M_HEAD_DIM].astype(BF16)
        scores.append(lax.dot_general(q, kbuf[slot, b, h].astype(BF16), (((1,), (1,)), ((), ())),
                                      preferred_element_type=F32))
    e, l = _softmax_rows(jnp.concatenate(scores, axis=0))
    inv_l = 1.0 / l
    outs = []
    for n, (b, h) in enumerate(pairs):
        rows = slice(n * seq, (n + 1) * seq)
        o = jnp.dot(e[rows].astype(BF16), vbuf[slot, b, h].astype(BF16), preferred_element_type=F32)
        outs.append(o * inv_l[rows])
    out_rows = [jnp.concatenate(outs[b * M_HEADS:(b + 1) * M_HEADS], axis=1) for b in range(bb)]
    o_ref[...] = jnp.concatenate(out_rows, axis=0).astype(BF16)


MERGE_ROW_CHUNK = 256


def _merge_kernel(ga_ref, gb_ref, gc_ref, u_ref, va_ref, sw_ref, sb_ref, b_ref, c_ref,
                  wa_ref, wb_ref, wc_ref, o_ref, a_scr, *, tm):
    @pl.when(pl.program_id(1) == 0)
    def _():
        for r in range(tm // CHUNK):
            rows = slice(r * CHUNK, (r + 1) * CHUNK)
            for g in range(A_GROUPS):
                cols = slice(g * A_GROUP_DIM, (g + 1) * A_GROUP_DIM)
                z = jnp.dot(sw_ref[g], va_ref[rows, cols], preferred_element_type=F32) + sb_ref[g]
                a_scr[rows, cols] = (u_ref[rows, cols].astype(F32) * z).astype(BF16)

    for r in range(tm // MERGE_ROW_CHUNK):
        rows = slice(r * MERGE_ROW_CHUNK, (r + 1) * MERGE_ROW_CHUNK)
        ya = jnp.dot(a_scr[rows, :], wa_ref[...], preferred_element_type=F32)
        yb = jnp.dot(b_ref[rows, :], wb_ref[...], preferred_element_type=F32)
        yc = jnp.dot(c_ref[rows, :], wc_ref[...], preferred_element_type=F32)
        merged = (ga_ref[rows, :].astype(F32) * ya + gb_ref[rows, :].astype(F32) * yb
                  + gc_ref[rows, :].astype(F32) * yc)
        o_ref[rows, :] = merged.astype(BF16)


def _merge(h_act, va, b_act, c_act, sgu_w, sgu_b, wa, wb, wc, *, va_col_block):
    m = h_act.shape[0]
    tm, tn = 1024, 512
    n_j = D_MODEL // tn
    return pl.pallas_call(
        functools.partial(_merge_kernel, tm=tm),
        grid=(m // tm, n_j),
        in_specs=[
            pl.BlockSpec((tm, tn), lambda i, j: (i, j)),
            pl.BlockSpec((tm, tn), lambda i, j: (i, n_j + j)),
            pl.BlockSpec((tm, tn), lambda i, j: (i, 2 * n_j + j)),
            pl.BlockSpec((tm, A_WIDTH), lambda i, j: (i, OFF_AU // A_WIDTH)),
            pl.BlockSpec((tm, A_WIDTH), lambda i, j: (i, va_col_block)),
            pl.BlockSpec((A_GROUPS, CHUNK, CHUNK), lambda i, j: (0, 0, 0)),
            pl.BlockSpec((A_GROUPS, CHUNK, 1), lambda i, j: (0, 0, 0)),
            pl.BlockSpec((tm, R_V_WIDTH), lambda i, j: (i, 0)),
            pl.BlockSpec((tm, M_WIDTH), lambda i, j: (i, 0)),
            pl.BlockSpec((A_WIDTH, tn), lambda i, j: (0, j)),
            pl.BlockSpec((R_V_WIDTH, tn), lambda i, j: (0, j)),
            pl.BlockSpec((M_WIDTH, tn), lambda i, j: (0, j)),
        ],
        out_specs=pl.BlockSpec((tm, tn), lambda i, j: (i, j)),
        out_shape=jax.ShapeDtypeStruct((m, D_MODEL), BF16),
        scratch_shapes=[pltpu.VMEM((tm, A_WIDTH), BF16)],
        compiler_params=_params("arbitrary", "arbitrary"),
        name="merge",
    )(h_act, h_act, h_act, h_act, va, sgu_w, sgu_b, b_act, c_act, wa, wb, wc)


OUT_ROW_CHUNK = 256


def _out_ln_kernel(m_ref, w_ref, x_ref, g_ref, b_ref, o_ref):
    for r in range(m_ref.shape[0] // OUT_ROW_CHUNK):
        rows = slice(r * OUT_ROW_CHUNK, (r + 1) * OUT_ROW_CHUNK)
        y = jnp.dot(m_ref[rows, :], w_ref[...], preferred_element_type=F32)
        o_ref[rows, :] = _standardize(ALPHA * x_ref[rows, :] + y) * g_ref[...] + b_ref[...]


def _out_ln(merged, w_out, x, ln_g, ln_b):
    m = x.shape[0]
    tm = 1024
    return pl.pallas_call(
        _out_ln_kernel,
        grid=(m // tm,),
        in_specs=[
            pl.BlockSpec((tm, D_MODEL), lambda i: (i, 0)),
            pl.BlockSpec((D_MODEL, D_MODEL), lambda i: (0, 0), pipeline_mode=pl.Buffered(1)),
            pl.BlockSpec((tm, D_MODEL), lambda i: (i, 0)),
            pl.BlockSpec((1, D_MODEL), lambda i: (0, 0)),
            pl.BlockSpec((1, D_MODEL), lambda i: (0, 0)),
        ],
        out_specs=pl.BlockSpec((tm, D_MODEL), lambda i: (i, 0)),
        out_shape=jax.ShapeDtypeStruct((m, D_MODEL), F32),
        compiler_params=_params("arbitrary"),
        name="out_ln",
    )(merged, w_out, x, ln_g, ln_b)


def _ffn_kernel(x_ref, wg_ref, wu_ref, wd_ref, g_ref, b_ref, o_ref, wgb_ref, wub_ref, wdb_ref,
                xb_ref, acc_ref):
    wgb_ref[...] = wg_ref[...].astype(BF16)
    wub_ref[...] = wu_ref[...].astype(BF16)
    wdb_ref[...] = wd_ref[...].astype(BF16)
    f = pl.program_id(1)

    @pl.when(f == 0)
    def _():
        xb_ref[...] = x_ref[...].astype(BF16)
        acc_ref[...] = jnp.zeros_like(acc_ref)

    xb = xb_ref[...]
    gate = jnp.dot(xb, wgb_ref[...], preferred_element_type=F32)
    up = jnp.dot(xb, wub_ref[...], preferred_element_type=F32)
    act = (jax.nn.silu(gate) * up).astype(BF16)
    acc_ref[...] += jnp.dot(act, wdb_ref[...], preferred_element_type=F32)

    @pl.when(f == pl.num_programs(1) - 1)
    def _():
        o_ref[...] = _standardize(ALPHA * x_ref[...] + acc_ref[...]) * g_ref[...] + b_ref[...]


def _ffn(x1, wg, wu, wd, ln_g, ln_b):
    tm = x1.shape[0]
    tf = 256
    one_buffer = pl.Buffered(1)
    out_shape = [jax.ShapeDtypeStruct((tm, D_MODEL), F32)] + [jax.ShapeDtypeStruct(w.shape, BF16)
                                                             for w in (wg, wu, wd)]
    out_specs = [pl.BlockSpec((tm, D_MODEL), lambda i, f: (i, 0), pipeline_mode=one_buffer),
                 pl.BlockSpec((D_MODEL, tf), lambda i, f: (0, f)),
                 pl.BlockSpec((D_MODEL, tf), lambda i, f: (0, f)),
                 pl.BlockSpec((tf, D_MODEL), lambda i, f: (f, 0))]
    return pl.pallas_call(
        _ffn_kernel,
        grid=(1, D_FF // tf),
        in_specs=[
            pl.BlockSpec((tm, D_MODEL), lambda i, f: (i, 0), pipeline_mode=one_buffer),
            pl.BlockSpec((D_MODEL, tf), lambda i, f: (0, f)),
            pl.BlockSpec((D_MODEL, tf), lambda i, f: (0, f)),
            pl.BlockSpec((tf, D_MODEL), lambda i, f: (f, 0)),
            pl.BlockSpec((1, D_MODEL), lambda i, f: (0, 0)),
            pl.BlockSpec((1, D_MODEL), lambda i, f: (0, 0)),
        ],
        out_specs=out_specs,
        out_shape=out_shape,
        scratch_shapes=[pltpu.VMEM((tm, D_MODEL), BF16), pltpu.VMEM((tm, D_MODEL), F32)],
        compiler_params=_params("arbitrary", "arbitrary"),
        name="ffn",
    )(x1, wg, wu, wd, ln_g, ln_b)


TAIL_ROW_CHUNK = 256


def _tail_kernel(m_ref, wo_ref, x_ref, g1_ref, b1_ref, wg_ref, wu_ref, wd_ref, g2_ref, b2_ref,
                 o_ref, x1_scr, x1b_scr):
    f = pl.program_id(1)
    n_f = pl.num_programs(1)
    chunks = [slice(r * TAIL_ROW_CHUNK, (r + 1) * TAIL_ROW_CHUNK)
              for r in range(m_ref.shape[0] // TAIL_ROW_CHUNK)]

    def swiglu(rows):
        xb = x1b_scr[rows, :]
        gate = jnp.dot(xb, wg_ref[...], preferred_element_type=F32)
        up = jnp.dot(xb, wu_ref[...], preferred_element_type=F32)
        act = (jax.nn.silu(gate) * up).astype(BF16)
        return jnp.dot(act, wd_ref[...], preferred_element_type=F32)

    @pl.when(f == 0)
    def _():
        for rows in chunks:
            y = jnp.dot(m_ref[rows, :], wo_ref[...], preferred_element_type=F32)
            x1 = _standardize(ALPHA * x_ref[rows, :] + y) * g1_ref[...] + b1_ref[...]
            x1_scr[rows, :] = x1
            x1b_scr[rows, :] = x1.astype(BF16)
        for rows in chunks:
            o_ref[rows, :] = swiglu(rows)

    @pl.when((f > 0) & (f < n_f - 1))
    def _():
        o_ref[...] += swiglu(slice(None))

    @pl.when(f == n_f - 1)
    def _():
        for rows in chunks:
            ff = o_ref[rows, :] + swiglu(rows)
            o_ref[rows, :] = _standardize(ALPHA * x1_scr[rows, :] + ff) * g2_ref[...] + b2_ref[...]


def _tail(merged, w_out, x, ln1_g, ln1_b, wg, wu, wd, ln2_g, ln2_b):
    m = x.shape[0]
    tm, tf = 512, 512
    assert D_FF // tf >= 2
    row = lambda i, f: (i, 0)
    const = lambda i, f: (0, 0)
    return pl.pallas_call(
        _tail_kernel,
        grid=(m // tm, D_FF // tf),
        in_specs=[
            pl.BlockSpec((tm, D_MODEL), row),
            pl.BlockSpec((D_MODEL, D_MODEL), const, pipeline_mode=pl.Buffered(1)),
            pl.BlockSpec((tm, D_MODEL), row),
            pl.BlockSpec((1, D_MODEL), const),
            pl.BlockSpec((1, D_MODEL), const),
            pl.BlockSpec((D_MODEL, tf), lambda i, f: (0, f)),
            pl.BlockSpec((D_MODEL, tf), lambda i, f: (0, f)),
            pl.BlockSpec((tf, D_MODEL), lambda i, f: (f, 0)),
            pl.BlockSpec((1, D_MODEL), const),
            pl.BlockSpec((1, D_MODEL), const),
        ],
        out_specs=pl.BlockSpec((tm, D_MODEL), row),
        out_shape=jax.ShapeDtypeStruct((m, D_MODEL), F32),
        scratch_shapes=[pltpu.VMEM((tm, D_MODEL), F32), pltpu.VMEM((tm, D_MODEL), BF16)],
        compiler_params=_params("arbitrary", "arbitrary"),
        name="tail",
    )(merged, w_out, x, ln1_g, ln1_b, wg, wu, wd, ln2_g, ln2_b)


def _rotary_tables(pos):
    half = R_QK_DIM // 2
    inv = ROPE_BASE ** (-jnp.arange(half, dtype=F32) / half)
    ang = pos[:, None] * inv[None, :]
    cos = jnp.cos(ang)
    sin = jnp.sin(ang)
    cos2 = jnp.concatenate([cos, cos], axis=-1)
    sin2 = jnp.concatenate([-sin, sin], axis=-1)
    k_scale = R_QK_DIM ** -0.5
    return jnp.stack([cos2, cos2 * k_scale]), jnp.stack([sin2, sin2 * k_scale])


def _sgu_tables(sgu_w, sgu_b, c):
    w = jnp.tril(sgu_w[:, :c, :c])
    b = sgu_b[:, :c]
    reps = CHUNK // c
    if reps > 1:
        blk = jnp.arange(CHUNK) // c
        w = jnp.where((blk[:, None] == blk[None, :])[None], jnp.tile(w, (1, reps, reps)), 0.0)
        b = jnp.tile(b, (1, reps))
    return w.astype(BF16), b[:, :, None]


def kernel(x_prompt, x_sample, mem_prompt, state_ret, cache_mem_k, cache_mem_v, w_in, sgu_ln_g, sgu_ln_b, sgu_w, sgu_b, w_proj_a, ret_gn_g, w_proj_b, w_mem_k, w_mem_v, w_proj_c, w_out, ln1_g, ln1_b, w_ffn_gate, w_ffn_up, w_ffn_down, ln2_g, ln2_b):
    assert w_in.shape[0] == DEPTH == 1
    bp, seq, _ = x_prompt.shape
    bs, dseq, _ = x_sample.shape
    l = 0
    layer = lambda a: a.reshape(a.shape[1:])
    bf = lambda w: w[l].astype(BF16)
    wa_b, wb_b, wc_b, wo_b = bf(w_proj_a), bf(w_proj_b), bf(w_proj_c), bf(w_out)
    wmk_b, wmv_b = bf(w_mem_k), bf(w_mem_v)
    ln_g = sgu_ln_g[l].reshape(1, A_WIDTH)
    ln_b = sgu_ln_b[l].reshape(1, A_WIDTH)
    gn = ret_gn_g[l].reshape(1, R_V_WIDTH)
    ln1g, ln1b = ln1_g[l].reshape(1, D_MODEL), ln1_b[l].reshape(1, D_MODEL)
    ln2g, ln2b = ln2_g[l].reshape(1, D_MODEL), ln2_b[l].reshape(1, D_MODEL)

    def merged(h_act, b_act, c_act, sgu_tabs):
        return _merge(h_act, h_act, b_act, c_act, sgu_tabs[0], sgu_tabs[1], wa_b, wb_b, wc_b,
                      va_col_block=OFF_AV // A_WIDTH)

    m_s = bs * dseq
    xs = x_sample.reshape(m_s, D_MODEL)
    cos_s, sin_s = _rotary_tables(float(PAST_LEN) + jnp.arange(dseq, dtype=F32))
    cos_s, sin_s = jnp.tile(cos_s, (1, bs, 1)), jnp.tile(sin_s, (1, bs, 1))
    h_s, va_s, w_in_b = _in_proj(xs, layer(w_in), cos_s, sin_s, ln_g, ln_b, tm=m_s)

    xp = x_prompt.reshape(bp * seq, D_MODEL)
    cos_p, sin_p = _rotary_tables(0.0 + jnp.arange(seq, dtype=F32))
    h_p, c_s = _in_proj(xp, w_in_b, cos_p, sin_p, ln_g, ln_b, tm=1024,
                        side=(h_s, layer(cache_mem_k), layer(cache_mem_v), dseq))
    b_p, s_p, b_s, s_s = _retention(h_p, h_s, layer(state_ret), gn,
                                    batch_l=bp, seq_l=seq, batch_s=bs, seq_s=dseq)

    m_s_merged = merged(h_s, b_s, c_s, _sgu_tables(sgu_w[l], sgu_b[l], min(dseq, CHUNK)))
    x1_s = _out_ln(m_s_merged, wo_b, xs, ln1g, ln1b)
    y_s, wg_b, wu_b, wd_b = _ffn(x1_s, layer(w_ffn_gate), layer(w_ffn_up), layer(w_ffn_down), ln2g, ln2b)

    memp = mem_prompt.reshape(bp * N_MEM, D_MODEL)
    mk_p, mk_pb = _mem_kv(memp, wmk_b)
    mv_p, mv_pb = _mem_kv(memp, wmv_b)
    c_p = _mem_attn_prompt(h_p, mk_pb, mv_pb, batch=bp, seq=seq)
    m_p_merged = merged(h_p, b_p, c_p, _sgu_tables(sgu_w[l], sgu_b[l], min(seq, CHUNK)))
    y_p = _tail(m_p_merged, wo_b, xp, ln1g, ln1b, wg_b, wu_b, wd_b, ln2g, ln2b)

    return (
        y_p.reshape(bp, seq, D_MODEL),
        y_s.reshape(bs, dseq, D_MODEL),
        s_p[None],
        mk_p.reshape(1, bp, N_MEM, M_HEADS, M_HEAD_DIM),
        mv_p.reshape(1, bp, N_MEM, M_HEADS, M_HEAD_DIM),
        s_s[None],
        va_s.reshape(1, bs, dseq, A_GROUPS, A_GROUP_DIM),
    )
```

```python
import functools
import math

import jax
import jax.numpy as jnp
from jax import lax
from jax.experimental import pallas as pl
from jax.experimental.pallas import tpu as pltpu

F32 = jnp.float32
BF16 = jnp.bfloat16

D_MODEL = 2048
DEPTH = 1
PAST_LEN = 16384
CHUNK = 128
A_GROUPS = 4
A_GROUP_DIM = 256
A_WIDTH = A_GROUPS * A_GROUP_DIM
R_HEADS = 8
R_QK_DIM = 128
R_V_DIM = 256
R_QK_WIDTH = R_HEADS * R_QK_DIM
R_V_WIDTH = R_HEADS * R_V_DIM
R_CHUNK = 128
N_MEM = 256
M_HEADS = 4
M_HEAD_DIM = 256
M_WIDTH = M_HEADS * M_HEAD_DIM
N_BRANCH = 3
D_FF = -(-8 * D_MODEL // (3 * 256)) * 256
ALPHA = (2.0 * DEPTH) ** 0.25
ROPE_BASE = 10000.0
LN_EPS = 1e-5

OFF_AU = N_BRANCH * D_MODEL
OFF_AV = OFF_AU + A_WIDTH
OFF_RQ = OFF_AV + A_WIDTH
OFF_RK = OFF_RQ + R_QK_WIDTH
OFF_RV = OFF_RK + R_QK_WIDTH
OFF_RG = OFF_RV + R_V_WIDTH
OFF_MQ = OFF_RG + R_V_WIDTH
IN_WIDTH = OFF_MQ + M_WIDTH

VMEM_LIMIT_BYTES = 56 * 1024 * 1024
LANES = 128


def _params(*sem):
    return pltpu.CompilerParams(dimension_semantics=sem, vmem_limit_bytes=VMEM_LIMIT_BYTES)


def _standardize(x):
    mu = jnp.mean(x, axis=-1, keepdims=True)
    xc = x - mu
    var = jnp.mean(xc * xc, axis=-1, keepdims=True)
    return xc * lax.rsqrt(var + LN_EPS)


IN_TN = 1024
IN_ROW_CHUNK = 256
_J_AU = OFF_AU // IN_TN
_J_AV = OFF_AV // IN_TN
_J_RQ = OFF_RQ // IN_TN
_J_RK = OFF_RK // IN_TN
_J_RV = OFF_RV // IN_TN
_J_RG = OFF_RG // IN_TN
_J_MQ = OFF_MQ // IN_TN
_J_END = IN_WIDTH // IN_TN


def _in_proj_kernel(x_ref, w_ref, cos_ref, sin_ref, lng_ref, lnb_ref, *rest, first_pass, side_seq,
                    side_steps):
    tm = x_ref.shape[0]
    j = pl.program_id(1)
    if first_pass:
        o_ref, va_ref, wb_ref, xb_ref = rest
        wb_ref[...] = w_ref[...].astype(BF16)
    else:
        qs_ref, mk_hbm, mv_hbm, o_ref, cs_ref, xb_ref, kbuf, vbuf, sem = rest
        wb_ref = w_ref

    @pl.when(j == 0)
    def _():
        xb_ref[...] = x_ref[...].astype(BF16)

    def column_range(pred, epilogue, with_side_job=False):
        @pl.when(pred)
        def _():
            if with_side_job:
                _mem_attend_step(pl.program_id(0) * side_steps + j, pl.num_programs(0) * side_steps,
                                 qs_ref, mk_hbm, mv_hbm, cs_ref, kbuf, vbuf, sem, seq=side_seq)
            for r in range(tm // IN_ROW_CHUNK):
                rows = slice(r * IN_ROW_CHUNK, (r + 1) * IN_ROW_CHUNK)
                acc = jnp.dot(xb_ref[rows, :], wb_ref[...], preferred_element_type=F32)
                epilogue(rows, acc)

    def store(fn):
        def epilogue(rows, acc):
            o_ref[rows, :] = fn(acc).astype(BF16)
        return epilogue

    def gelu_layernorm(rows, acc):
        v = jax.nn.gelu(acc)
        for g in range(IN_TN // A_GROUP_DIM):
            cols = slice(g * A_GROUP_DIM, (g + 1) * A_GROUP_DIM)
            vn = _standardize(v[:, cols]) * lng_ref[:, cols] + lnb_ref[:, cols]
            o_ref[rows, cols] = vn.astype(BF16)
            if first_pass:
                va_ref[rows, cols] = vn

    def rotary(rows, acc):
        for hh in range(IN_TN // R_QK_DIM):
            cols = slice(hh * R_QK_DIM, (hh + 1) * R_QK_DIM)
            xh = acc[:, cols]
            r = xh * cos_ref[rows, :] + pltpu.roll(xh, R_QK_DIM // 2, 1) * sin_ref[rows, :]
            o_ref[rows, cols] = r.astype(BF16)

    is_mq = j >= _J_MQ
    raw_scale = jnp.where(is_mq, M_HEAD_DIM ** -0.5, 1.0).astype(F32)
    if first_pass:
        column_range(j < _J_AU, store(jax.nn.sigmoid))
    else:
        column_range(j < side_steps, store(jax.nn.sigmoid), with_side_job=True)
        column_range((j >= side_steps) & (j < _J_AU), store(jax.nn.sigmoid))
    column_range((j >= _J_AU) & (j < _J_AV), store(jax.nn.gelu))
    column_range((j >= _J_AV) & (j < _J_RQ), gelu_layernorm)
    column_range((j >= _J_RQ) & (j < _J_RV), rotary)
    column_range(((j >= _J_RV) & (j < _J_RG)) | is_mq, store(lambda acc: acc * raw_scale))
    column_range((j >= _J_RG) & (j < _J_MQ), store(jax.nn.silu))


def _in_proj(x, w_in, cos_t, sin_t, ln_g, ln_b, *, tm, side=None):
    m = x.shape[0]
    first_pass = side is None
    n_pos_blocks = cos_t.shape[1] // tm
    assert A_WIDTH == IN_TN and R_QK_WIDTH == IN_TN
    table_idx = lambda i, j: (jnp.where(j == _J_RK, 1, 0), i % n_pos_blocks, 0)
    grid = (m // tm, _J_END)

    operands = [x, w_in, cos_t, sin_t, ln_g, ln_b]
    in_specs = [
        pl.BlockSpec((tm, D_MODEL), lambda i, j: (i, 0),
                     pipeline_mode=pl.Buffered(1) if first_pass else None),
        pl.BlockSpec((D_MODEL, IN_TN), lambda i, j: (0, j)),
        pl.BlockSpec((None, tm, R_QK_DIM), table_idx),
        pl.BlockSpec((None, tm, R_QK_DIM), table_idx),
        pl.BlockSpec((1, A_WIDTH), lambda i, j: (0, 0)),
        pl.BlockSpec((1, A_WIDTH), lambda i, j: (0, 0)),
    ]
    out_shape = [jax.ShapeDtypeStruct((m, IN_WIDTH), BF16)]
    out_specs = [pl.BlockSpec((tm, IN_TN), lambda i, j: (i, j))]
    scratch_shapes = [pltpu.VMEM((tm, D_MODEL), BF16)]
    side_seq = side_steps = None
    if first_pass:
        assert m == tm
        out_shape += [jax.ShapeDtypeStruct((m, A_WIDTH), F32), jax.ShapeDtypeStruct(w_in.shape, BF16)]
        out_specs += [pl.BlockSpec((tm, A_WIDTH), lambda i, j: (i, 0)),
                      pl.BlockSpec((D_MODEL, IN_TN), lambda i, j: (0, j))]
    else:
        h_side, mem_k, mem_v, side_seq = side
        side_rows = MEM_SEQS_PER_STEP * side_seq
        side_steps, rem = divmod(mem_k.shape[0], MEM_SEQS_PER_STEP * grid[0])
        assert rem == 0 and 0 < side_steps <= _J_AU
        side_block = lambda i, j: i * side_steps + jnp.minimum(j, side_steps - 1)
        operands += [h_side, mem_k, mem_v]
        in_specs += [pl.BlockSpec((side_rows, M_WIDTH), lambda i, j: (side_block(i, j), OFF_MQ // M_WIDTH)),
                     pl.BlockSpec(memory_space=pl.ANY),
                     pl.BlockSpec(memory_space=pl.ANY)]
        out_shape.append(jax.ShapeDtypeStruct((h_side.shape[0], M_WIDTH), BF16))
        out_specs.append(pl.BlockSpec((side_rows, M_WIDTH), lambda i, j: (side_block(i, j), 0)))
        scratch_shapes += [pltpu.VMEM((2, MEM_SEQS_PER_STEP, M_HEADS, N_MEM, M_HEAD_DIM), F32),
                           pltpu.VMEM((2, MEM_SEQS_PER_STEP, M_HEADS, N_MEM, M_HEAD_DIM), F32),
                           pltpu.SemaphoreType.DMA((2, 2))]
    return pl.pallas_call(
        functools.partial(_in_proj_kernel, first_pass=first_pass, side_seq=side_seq, side_steps=side_steps),
        grid=grid,
        in_specs=in_specs,
        out_specs=out_specs,
        out_shape=out_shape,
        scratch_shapes=scratch_shapes,
        compiler_params=_params("arbitrary", "arbitrary"),
        name="in_proj",
    )(*operands)


def _mem_kv_kernel(x_ref, wk_ref, wv_ref, k_ref, kb_ref, v_ref, vb_ref):
    xb = x_ref[...].astype(BF16)
    for w_ref, o_ref, ob_ref in ((wk_ref, k_ref, kb_ref), (wv_ref, v_ref, vb_ref)):
        acc = jnp.dot(xb, w_ref[...].astype(BF16), preferred_element_type=F32)
        o_ref[...] = acc
        ob_ref[...] = acc.astype(BF16)


def _mem_kv(x, wk, wv):
    m, k = x.shape
    n = wk.shape[1]
    tn = 512
    w_spec = pl.BlockSpec((k, tn), lambda j: (0, j))
    o_spec = pl.BlockSpec((m, tn), lambda j: (0, j))
    return pl.pallas_call(
        _mem_kv_kernel,
        grid=(n // tn,),
        in_specs=[pl.BlockSpec((m, k), lambda j: (0, 0), pipeline_mode=pl.Buffered(1)), w_spec, w_spec],
        out_specs=[o_spec] * 4,
        out_shape=[jax.ShapeDtypeStruct((m, n), F32), jax.ShapeDtypeStruct((m, n), BF16)] * 2,
        compiler_params=_params("arbitrary"),
        name="mem_kv",
    )(x, wk, wv)


def _gated_groupnorm(o, gn, g):
    return (g.astype(F32) * (_standardize(o) * gn)).astype(BF16)


def _ret_long_step(cd_ref, q_ref, k_ref, v_ref, g_ref, dmask_ref, qd_ref, kd_ref, gn_ref,
                   o_ref, sfin_ref, s_scr, *, n_chunks):
    t = pl.program_id(1)

    @pl.when(t == 0)
    def _():
        s_scr[...] = jnp.zeros_like(s_scr)

    state = [s_scr[h] for h in range(R_HEADS)]
    for c in range(n_chunks):
        rows = slice(c * R_CHUNK, (c + 1) * R_CHUNK)
        for h in range(R_HEADS):
            qk = slice(h * R_QK_DIM, (h + 1) * R_QK_DIM)
            vv = slice(h * R_V_DIM, (h + 1) * R_V_DIM)
            q = q_ref[rows, qk]
            k = k_ref[rows, qk]
            v = v_ref[rows, vv]
            s = state[h]
            sc = lax.dot_general(q, k, (((1,), (1,)), ((), ())), preferred_element_type=F32) * dmask_ref[h]
            lhs = jnp.concatenate([sc.astype(BF16), (q.astype(F32) * qd_ref[h]).astype(BF16)], axis=1)
            rhs = jnp.concatenate([v, s.astype(BF16)], axis=0)
            o = jnp.dot(lhs, rhs, preferred_element_type=F32)
            kd = (k.astype(F32) * kd_ref[h]).astype(BF16)
            state[h] = s * cd_ref[h] + lax.dot_general(kd, v, (((0,), (0,)), ((), ())),
                                                       preferred_element_type=F32)
            o_ref[rows, vv] = _gated_groupnorm(o, gn_ref[:, vv], g_ref[rows, vv])
    for h in range(R_HEADS):
        s_scr[h] = state[h]

    @pl.when(t == pl.num_programs(1) - 1)
    def _():
        sfin_ref[0] = s_scr[...]


def _ret_short_step(cd_ref, q_ref, k_ref, v_ref, g_ref, s_ref, dmask_ref, qd_ref, kd_ref, gn_ref,
                    o_ref, snew_ref, *, bb, seq):
    k_all = k_ref[...].astype(F32)
    v_all = v_ref[...].astype(F32)
    for h in range(R_HEADS):
        qk = slice(h * R_QK_DIM, (h + 1) * R_QK_DIM)
        vv = slice(h * R_V_DIM, (h + 1) * R_V_DIM)
        q = q_ref[:, qk]
        sc = lax.dot_general(q, k_ref[:, qk], (((1,), (1,)), ((), ())),
                             preferred_element_type=F32) * dmask_ref[h]
        intra = jnp.dot(sc.astype(BF16), v_ref[:, vv], preferred_element_type=F32)
        q_scaled = q.astype(F32) * qd_ref[h]
        k_scaled = k_all[:, qk] * kd_ref[h]
        inter = []
        for b in range(bb):
            rows = slice(b * seq, (b + 1) * seq)
            s = s_ref[b, h]
            inter.append(jnp.dot(q_scaled[rows].astype(BF16), s.astype(BF16), preferred_element_type=F32))
            snew_ref[b, h] = s * cd_ref[h] + lax.dot_general(
                k_scaled[rows].astype(BF16), v_all[rows, vv].astype(BF16), (((0,), (0,)), ((), ())),
                preferred_element_type=F32)
        o = intra + jnp.concatenate(inter, axis=0)
        o_ref[:, vv] = _gated_groupnorm(o, gn_ref[:, vv], g_ref[:, vv])


def _retention_kernel(cdl_ref, cds_ref, ql_ref, kl_ref, vl_ref, gl_ref, dml_ref, qdl_ref, kdl_ref,
                      qs_ref, ks_ref, vs_ref, gs_ref, s_ref, dms_ref, qds_ref, kds_ref, gn_ref,
                      ol_ref, sfin_ref, os_ref, snew_ref, s_scr, *, n_chunks, bb, seq):
    _ret_long_step(cdl_ref, ql_ref, kl_ref, vl_ref, gl_ref, dml_ref, qdl_ref, kdl_ref, gn_ref,
                   ol_ref, sfin_ref, s_scr, n_chunks=n_chunks)
    _ret_short_step(cds_ref, qs_ref, ks_ref, vs_ref, gs_ref, s_ref, dms_ref, qds_ref, kds_ref, gn_ref,
                    os_ref, snew_ref, bb=bb, seq=seq)


def _retention(h_long, h_short, state, gn, *, batch_l, seq_l, batch_s, seq_s):
    tr = 256
    n_t = seq_l // tr
    n_steps = batch_l * n_t
    bb, rem = divmod(batch_s, n_steps)
    assert rem == 0 and (bb * seq_s) % 16 == 0
    ts = bb * seq_s
    tabs_l = _retention_tables(min(seq_l, R_CHUNK))
    tabs_s = _retention_tables(min(seq_s, R_CHUNK))
    lane_bcast = lambda t, rows: jnp.broadcast_to(t, (R_HEADS, rows, R_QK_DIM))
    blk = jnp.arange(ts) // seq_s
    dmask_s = jnp.where((blk[:, None] == blk[None, :])[None], jnp.tile(tabs_s["dmask"], (1, bb, bb)), 0.0)
    qd_s = lane_bcast(jnp.tile(tabs_s["q_decay"], (1, bb, 1)), ts)
    kd_s = lane_bcast(jnp.tile(tabs_s["k_decay"], (1, bb, 1)), ts)

    step = lambda b, t: b * n_t + t
    const3 = lambda b, t: (0, 0, 0)
    smem = pl.BlockSpec(memory_space=pltpu.SMEM)
    state_spec = pl.BlockSpec((bb, R_HEADS, R_QK_DIM, R_V_DIM), lambda b, t: (step(b, t), 0, 0, 0))
    return pl.pallas_call(
        functools.partial(_retention_kernel, n_chunks=tr // R_CHUNK, bb=bb, seq=seq_s),
        grid=(batch_l, n_t),
        in_specs=[
            smem, smem,
            pl.BlockSpec((tr, R_QK_WIDTH), lambda b, t: (step(b, t), OFF_RQ // R_QK_WIDTH)),
            pl.BlockSpec((tr, R_QK_WIDTH), lambda b, t: (step(b, t), OFF_RK // R_QK_WIDTH)),
            pl.BlockSpec((tr, R_V_WIDTH), lambda b, t: (step(b, t), OFF_RV // R_V_WIDTH)),
            pl.BlockSpec((tr, R_V_WIDTH), lambda b, t: (step(b, t), OFF_RG // R_V_WIDTH)),
            pl.BlockSpec((R_HEADS, R_CHUNK, R_CHUNK), const3),
            pl.BlockSpec((R_HEADS, R_CHUNK, R_QK_DIM), const3),
            pl.BlockSpec((R_HEADS, R_CHUNK, R_QK_DIM), const3),
            pl.BlockSpec((ts, R_QK_WIDTH), lambda b, t: (step(b, t), OFF_RQ // R_QK_WIDTH)),
            pl.BlockSpec((ts, R_QK_WIDTH), lambda b, t: (step(b, t), OFF_RK // R_QK_WIDTH)),
            pl.BlockSpec((ts, R_V_WIDTH), lambda b, t: (step(b, t), OFF_RV // R_V_WIDTH)),
            pl.BlockSpec((ts, R_V_WIDTH), lambda b, t: (step(b, t), OFF_RG // R_V_WIDTH)),
            state_spec,
            pl.BlockSpec((R_HEADS, ts, ts), const3),
            pl.BlockSpec((R_HEADS, ts, R_QK_DIM), const3),
            pl.BlockSpec((R_HEADS, ts, R_QK_DIM), const3),
            pl.BlockSpec((1, R_V_WIDTH), lambda b, t: (0, 0)),
        ],
        out_specs=[
            pl.BlockSpec((tr, R_V_WIDTH), lambda b, t: (step(b, t), 0)),
            pl.BlockSpec((1, R_HEADS, R_QK_DIM, R_V_DIM), lambda b, t: (b, 0, 0, 0)),
            pl.BlockSpec((ts, R_V_WIDTH), lambda b, t: (step(b, t), 0)),
            state_spec,
        ],
        out_shape=[
            jax.ShapeDtypeStruct((batch_l * seq_l, R_V_WIDTH), BF16),
            jax.ShapeDtypeStruct((batch_l, R_HEADS, R_QK_DIM, R_V_DIM), F32),
            jax.ShapeDtypeStruct((batch_s * seq_s, R_V_WIDTH), BF16),
            jax.ShapeDtypeStruct((batch_s, R_HEADS, R_QK_DIM, R_V_DIM), F32),
        ],
        scratch_shapes=[pltpu.VMEM((R_HEADS, R_QK_DIM, R_V_DIM), F32)],
        compiler_params=_params("arbitrary", "arbitrary"),
        name="retention",
    )(tabs_l["chunk_decay"], tabs_s["chunk_decay"], h_long, h_long, h_long, h_long, tabs_l["dmask"],
      lane_bcast(tabs_l["q_decay"], R_CHUNK), lane_bcast(tabs_l["k_decay"], R_CHUNK),
      h_short, h_short, h_short, h_short, state, dmask_s, qd_s, kd_s, gn)


def _retention_tables(c):
    log_g = jnp.log1p(-jnp.exp2(-5.0 - jnp.arange(R_HEADS, dtype=F32)))
    idx = jnp.arange(c, dtype=F32)
    diff = idx[:, None] - idx[None, :]
    causal = diff >= 0
    dmask = jnp.where(causal[None], jnp.exp(jnp.where(causal, diff, 0.0)[None] * log_g[:, None, None]), 0.0)
    q_decay = jnp.exp((idx[:, None] + 1.0) * log_g[None, :])
    k_decay = jnp.exp((c - 1.0 - idx)[:, None] * log_g[None, :])
    chunk_decay = jnp.exp(c * log_g)
    return {
        "dmask": dmask,
        "q_decay": q_decay.T[:, :, None],
        "k_decay": k_decay.T[:, :, None],
        "chunk_decay": chunk_decay,
    }


def _softmax_rows(s):
    e = jnp.exp(s - jnp.max(s, axis=-1, keepdims=True))
    return e, jnp.sum(e, axis=-1, keepdims=True)


def _mem_prompt_kernel(q_ref, mk_ref, mv_ref, o_ref):
    for h in range(M_HEADS):
        cols = slice(h * M_HEAD_DIM, (h + 1) * M_HEAD_DIM)
        s = lax.dot_general(q_ref[:, cols], mk_ref[:, cols], (((1,), (1,)), ((), ())),
                            preferred_element_type=F32)
        e, l = _softmax_rows(s)
        o = jnp.dot(e.astype(BF16), mv_ref[:, cols], preferred_element_type=F32) / l
        o_ref[:, cols] = o.astype(BF16)


def _mem_attn_prompt(h_act, mk, mv, *, batch, seq):
    tq = 512
    n_t = seq // tq
    return pl.pallas_call(
        _mem_prompt_kernel,
        grid=(batch, n_t),
        in_specs=[
            pl.BlockSpec((tq, M_WIDTH), lambda b, t: (b * n_t + t, OFF_MQ // M_WIDTH)),
            pl.BlockSpec((N_MEM, M_WIDTH), lambda b, t: (b, 0)),
            pl.BlockSpec((N_MEM, M_WIDTH), lambda b, t: (b, 0)),
        ],
        out_specs=pl.BlockSpec((tq, M_WIDTH), lambda b, t: (b * n_t + t, 0)),
        out_shape=jax.ShapeDtypeStruct((batch * seq, M_WIDTH), BF16),
        compiler_params=_params("arbitrary", "arbitrary"),
        name="mem_attn_prompt",
    )(h_act, mk, mv)


MEM_SEQS_PER_STEP = 4


def _mem_attend_step(step, n_steps, q_ref, mk_hbm, mv_hbm, o_ref, kbuf, vbuf, sem, *, seq):
    bb = MEM_SEQS_PER_STEP
    pairs = [(b, h) for b in range(bb) for h in range(M_HEADS)]

    def copies(st, slot):
        out = []
        for b, h in pairs:
            out.append(pltpu.make_async_copy(mk_hbm.at[st * bb + b, :, h, :], kbuf.at[slot, b, h],
                                             sem.at[0, slot]))
            out.append(pltpu.make_async_copy(mv_hbm.at[st * bb + b, :, h, :], vbuf.at[slot, b, h],
                                             sem.at[1, slot]))
        return out

    slot = step % 2

    @pl.when(step == 0)
    def _():
        for c in copies(0, 0):
            c.start()

    @pl.when(step + 1 < n_steps)
    def _():
        for c in copies(step + 1, 1 - slot):
            c.start()

    for c in copies(step, slot):
        c.wait()
    q_all = q_ref[...].astype(F32)
    scores = []
    for b, h in pairs:
        q = q_all[b * seq:(b + 1) * seq, h * M_HEAD_DIM:(h + 1) * M_HEAD_DIM].astype(BF16)
        scores.append(lax.dot_general(q, kbuf[slot, b, h].astype(BF16), (((1,), (1,)), ((), ())),
                                      preferred_element_type=F32))
    e, l = _softmax_rows(jnp.concatenate(scores, axis=0))
    inv_l = 1.0 / l
    outs = []
    for n, (b, h) in enumerate(pairs):
        rows = slice(n * seq, (n + 1) * seq)
        o = jnp.dot(e[rows].astype(BF16), vbuf[slot, b, h].astype(BF16), preferred_element_type=F32)
        outs.append(o * inv_l[rows])
    out_rows = [jnp.concatenate(outs[b * M_HEADS:(b + 1) * M_HEADS], axis=1) for b in range(bb)]
    o_ref[...] = jnp.concatenate(out_rows, axis=0).astype(BF16)


MERGE_ROW_CHUNK = 256


def _merge_kernel(ga_ref, gb_ref, gc_ref, u_ref, va_ref, sw_ref, sb_ref, b_ref, c_ref,
                  wa_ref, wb_ref, wc_ref, o_ref, *rest, tm, first_pass):
    if first_pass:
        wab_ref, wbb_ref, wcb_ref, a_scr = rest
        wab_ref[...] = wa_ref[...].astype(BF16)
        wbb_ref[...] = wb_ref[...].astype(BF16)
        wcb_ref[...] = wc_ref[...].astype(BF16)
        wa_ref, wb_ref, wc_ref = wab_ref, wbb_ref, wcb_ref
    else:
        (a_scr,) = rest

    @pl.when(pl.program_id(1) == 0)
    def _():
        for r in range(tm // CHUNK):
            rows = slice(r * CHUNK, (r + 1) * CHUNK)
            for g in range(A_GROUPS):
                cols = slice(g * A_GROUP_DIM, (g + 1) * A_GROUP_DIM)
                z = jnp.dot(sw_ref[g], va_ref[rows, cols], preferred_element_type=F32) + sb_ref[g]
                a_scr[rows, cols] = (u_ref[rows, cols].astype(F32) * z).astype(BF16)

    for r in range(tm // MERGE_ROW_CHUNK):
        rows = slice(r * MERGE_ROW_CHUNK, (r + 1) * MERGE_ROW_CHUNK)
        ya = jnp.dot(a_scr[rows, :], wa_ref[...], preferred_element_type=F32)
        yb = jnp.dot(b_ref[rows, :], wb_ref[...], preferred_element_type=F32)
        yc = jnp.dot(c_ref[rows, :], wc_ref[...], preferred_element_type=F32)
        merged = (ga_ref[rows, :].astype(F32) * ya + gb_ref[rows, :].astype(F32) * yb
                  + gc_ref[rows, :].astype(F32) * yc)
        o_ref[rows, :] = merged.astype(BF16)


def _merge(h_act, b_act, c_act, sgu_w, sgu_b, wa, wb, wc, *, first_pass):
    m = h_act.shape[0]
    tm, tn = 1024, 512
    n_j = D_MODEL // tn
    row_tile_mode = pl.Buffered(1) if first_pass else None
    w_specs = [pl.BlockSpec((A_WIDTH, tn), lambda i, j: (0, j)),
               pl.BlockSpec((R_V_WIDTH, tn), lambda i, j: (0, j)),
               pl.BlockSpec((M_WIDTH, tn), lambda i, j: (0, j))]
    out_specs = [pl.BlockSpec((tm, tn), lambda i, j: (i, j))]
    out_shape = [jax.ShapeDtypeStruct((m, D_MODEL), BF16)]
    if first_pass:
        assert m == tm
        out_specs += w_specs
        out_shape += [jax.ShapeDtypeStruct(w.shape, BF16) for w in (wa, wb, wc)]
    return pl.pallas_call(
        functools.partial(_merge_kernel, tm=tm, first_pass=first_pass),
        grid=(m // tm, n_j),
        in_specs=[
            pl.BlockSpec((tm, tn), lambda i, j: (i, j)),
            pl.BlockSpec((tm, tn), lambda i, j: (i, n_j + j)),
            pl.BlockSpec((tm, tn), lambda i, j: (i, 2 * n_j + j)),
            pl.BlockSpec((tm, A_WIDTH), lambda i, j: (i, OFF_AU // A_WIDTH), pipeline_mode=row_tile_mode),
            pl.BlockSpec((tm, A_WIDTH), lambda i, j: (i, OFF_AV // A_WIDTH), pipeline_mode=row_tile_mode),
            pl.BlockSpec((A_GROUPS, CHUNK, CHUNK), lambda i, j: (0, 0, 0)),
            pl.BlockSpec((A_GROUPS, CHUNK, 1), lambda i, j: (0, 0, 0)),
            pl.BlockSpec((tm, R_V_WIDTH), lambda i, j: (i, 0), pipeline_mode=row_tile_mode),
            pl.BlockSpec((tm, M_WIDTH), lambda i, j: (i, 0), pipeline_mode=row_tile_mode),
        ] + w_specs,
        out_specs=out_specs,
        out_shape=out_shape,
        scratch_shapes=[pltpu.VMEM((tm, A_WIDTH), BF16)],
        compiler_params=_params("arbitrary", "arbitrary"),
        name="merge",
    )(h_act, h_act, h_act, h_act, h_act, sgu_w, sgu_b, b_act, c_act, wa, wb, wc)


OUT_ROW_CHUNK = 256


def _out_ln_kernel(m_ref, w_ref, x_ref, g_ref, b_ref, o_ref, wb_ref):
    wb_ref[...] = w_ref[...].astype(BF16)
    for r in range(m_ref.shape[0] // OUT_ROW_CHUNK):
        rows = slice(r * OUT_ROW_CHUNK, (r + 1) * OUT_ROW_CHUNK)
        y = jnp.dot(m_ref[rows, :], wb_ref[...], preferred_element_type=F32)
        o_ref[rows, :] = _standardize(ALPHA * x_ref[rows, :] + y) * g_ref[...] + b_ref[...]


def _out_ln(merged, w_out, x, ln_g, ln_b):
    tm = x.shape[0]
    full = lambda i: (0, 0)
    return pl.pallas_call(
        _out_ln_kernel,
        grid=(1,),
        in_specs=[
            pl.BlockSpec((tm, D_MODEL), full),
            pl.BlockSpec((D_MODEL, D_MODEL), full),
            pl.BlockSpec((tm, D_MODEL), full),
            pl.BlockSpec((1, D_MODEL), full),
            pl.BlockSpec((1, D_MODEL), full),
        ],
        out_specs=[pl.BlockSpec((tm, D_MODEL), full), pl.BlockSpec((D_MODEL, D_MODEL), full)],
        out_shape=[jax.ShapeDtypeStruct((tm, D_MODEL), F32), jax.ShapeDtypeStruct(w_out.shape, BF16)],
        compiler_params=_params("arbitrary"),
        name="out_ln",
    )(merged, w_out, x, ln_g, ln_b)


def _ffn_kernel(x_ref, wg_ref, wu_ref, wd_ref, g_ref, b_ref, o_ref, wgb_ref, wub_ref, wdb_ref,
                xb_ref, acc_ref):
    wgb_ref[...] = wg_ref[...].astype(BF16)
    wub_ref[...] = wu_ref[...].astype(BF16)
    wdb_ref[...] = wd_ref[...].astype(BF16)
    f = pl.program_id(1)

    @pl.when(f == 0)
    def _():
        xb_ref[...] = x_ref[...].astype(BF16)
        acc_ref[...] = jnp.zeros_like(acc_ref)

    xb = xb_ref[...]
    gate = jnp.dot(xb, wgb_ref[...], preferred_element_type=F32)
    up = jnp.dot(xb, wub_ref[...], preferred_element_type=F32)
    act = (jax.nn.silu(gate) * up).astype(BF16)
    acc_ref[...] += jnp.dot(act, wdb_ref[...], preferred_element_type=F32)

    @pl.when(f == pl.num_programs(1) - 1)
    def _():
        o_ref[...] = _standardize(ALPHA * x_ref[...] + acc_ref[...]) * g_ref[...] + b_ref[...]


def _ffn(x1, wg, wu, wd, ln_g, ln_b):
    tm = x1.shape[0]
    tf = 256
    one_buffer = pl.Buffered(1)
    out_shape = [jax.ShapeDtypeStruct((tm, D_MODEL), F32)] + [jax.ShapeDtypeStruct(w.shape, BF16)
                                                             for w in (wg, wu, wd)]
    out_specs = [pl.BlockSpec((tm, D_MODEL), lambda i, f: (i, 0), pipeline_mode=one_buffer),
                 pl.BlockSpec((D_MODEL, tf), lambda i, f: (0, f)),
                 pl.BlockSpec((D_MODEL, tf), lambda i, f: (0, f)),
                 pl.BlockSpec((tf, D_MODEL), lambda i, f: (f, 0))]
    return pl.pallas_call(
        _ffn_kernel,
        grid=(1, D_FF // tf),
        in_specs=[
            pl.BlockSpec((tm, D_MODEL), lambda i, f: (i, 0), pipeline_mode=one_buffer),
            pl.BlockSpec((D_MODEL, tf), lambda i, f: (0, f)),
            pl.BlockSpec((D_MODEL, tf), lambda i, f: (0, f)),
            pl.BlockSpec((tf, D_MODEL), lambda i, f: (f, 0)),
            pl.BlockSpec((1, D_MODEL), lambda i, f: (0, 0)),
            pl.BlockSpec((1, D_MODEL), lambda i, f: (0, 0)),
        ],
        out_specs=out_specs,
        out_shape=out_shape,
        scratch_shapes=[pltpu.VMEM((tm, D_MODEL), BF16), pltpu.VMEM((tm, D_MODEL), F32)],
        compiler_params=_params("arbitrary", "arbitrary"),
        name="ffn",
    )(x1, wg, wu, wd, ln_g, ln_b)


TAIL_ROW_CHUNK = 256


def _tail_kernel(m_ref, wo_ref, x_ref, g1_ref, b1_ref, wg_ref, wu_ref, wd_ref, g2_ref, b2_ref,
                 o_ref, x1_scr, x1b_scr):
    f = pl.program_id(1)
    n_f = pl.num_programs(1)
    chunks = [slice(r * TAIL_ROW_CHUNK, (r + 1) * TAIL_ROW_CHUNK)
              for r in range(m_ref.shape[0] // TAIL_ROW_CHUNK)]

    def swiglu(rows):
        xb = x1b_scr[rows, :]
        gate = jnp.dot(xb, wg_ref[...], preferred_element_type=F32)
        up = jnp.dot(xb, wu_ref[...], preferred_element_type=F32)
        act = (jax.nn.silu(gate) * up).astype(BF16)
        return jnp.dot(act, wd_ref[...], preferred_element_type=F32)

    @pl.when(f == 0)
    def _():
        for rows in chunks:
            y = jnp.dot(m_ref[rows, :], wo_ref[...], preferred_element_type=F32)
            x1 = _standardize(ALPHA * x_ref[rows, :] + y) * g1_ref[...] + b1_ref[...]
            x1_scr[rows, :] = x1
            x1b_scr[rows, :] = x1.astype(BF16)
        for rows in chunks:
            o_ref[rows, :] = swiglu(rows)

    @pl.when((f > 0) & (f < n_f - 1))
    def _():
        o_ref[...] += swiglu(slice(None))

    @pl.when(f == n_f - 1)
    def _():
        for rows in chunks:
            ff = o_ref[rows, :] + swiglu(rows)
            o_ref[rows, :] = _standardize(ALPHA * x1_scr[rows, :] + ff) * g2_ref[...] + b2_ref[...]


def _tail(merged, w_out, x, ln1_g, ln1_b, wg, wu, wd, ln2_g, ln2_b):
    m = x.shape[0]
    tm, tf = 512, 512
    assert D_FF // tf >= 2
    row = lambda i, f: (i, 0)
    const = lambda i, f: (0, 0)
    return pl.pallas_call(
        _tail_kernel,
        grid=(m // tm, D_FF // tf),
        in_specs=[
            pl.BlockSpec((tm, D_MODEL), row),
            pl.BlockSpec((D_MODEL, D_MODEL), const, pipeline_mode=pl.Buffered(1)),
            pl.BlockSpec((tm, D_MODEL), row),
            pl.BlockSpec((1, D_MODEL), const),
            pl.BlockSpec((1, D_MODEL), const),
            pl.BlockSpec((D_MODEL, tf), lambda i, f: (0, f)),
            pl.BlockSpec((D_MODEL, tf), lambda i, f: (0, f)),
            pl.BlockSpec((tf, D_MODEL), lambda i, f: (f, 0)),
            pl.BlockSpec((1, D_MODEL), const),
            pl.BlockSpec((1, D_MODEL), const),
        ],
        out_specs=pl.BlockSpec((tm, D_MODEL), row),
        out_shape=jax.ShapeDtypeStruct((m, D_MODEL), F32),
        scratch_shapes=[pltpu.VMEM((tm, D_MODEL), F32), pltpu.VMEM((tm, D_MODEL), BF16)],
        compiler_params=_params("arbitrary", "arbitrary"),
        name="tail",
    )(merged, w_out, x, ln1_g, ln1_b, wg, wu, wd, ln2_g, ln2_b)


def _rotary_tables(pos):
    half = R_QK_DIM // 2
    inv = ROPE_BASE ** (-jnp.arange(half, dtype=F32) / half)
    ang = pos[:, None] * inv[None, :]
    cos = jnp.cos(ang)
    sin = jnp.sin(ang)
    cos2 = jnp.concatenate([cos, cos], axis=-1)
    sin2 = jnp.concatenate([-sin, sin], axis=-1)
    k_scale = R_QK_DIM ** -0.5
    return jnp.stack([cos2, cos2 * k_scale]), jnp.stack([sin2, sin2 * k_scale])


def _sgu_tables(sgu_w, sgu_b, c):
    w = jnp.tril(sgu_w[:, :c, :c])
    b = sgu_b[:, :c]
    reps = CHUNK // c
    if reps > 1:
        blk = jnp.arange(CHUNK) // c
        w = jnp.where((blk[:, None] == blk[None, :])[None], jnp.tile(w, (1, reps, reps)), 0.0)
        b = jnp.tile(b, (1, reps))
    return w.astype(BF16), b[:, :, None]


def kernel(x_prompt, x_sample, mem_prompt, state_ret, cache_mem_k, cache_mem_v, w_in, sgu_ln_g, sgu_ln_b, sgu_w, sgu_b, w_proj_a, ret_gn_g, w_proj_b, w_mem_k, w_mem_v, w_proj_c, w_out, ln1_g, ln1_b, w_ffn_gate, w_ffn_up, w_ffn_down, ln2_g, ln2_b):
    assert w_in.shape[0] == DEPTH == 1
    bp, seq, _ = x_prompt.shape
    bs, dseq, _ = x_sample.shape
    l = 0
    layer = lambda a: a.reshape(a.shape[1:])
    ln_g = sgu_ln_g[l].reshape(1, A_WIDTH)
    ln_b = sgu_ln_b[l].reshape(1, A_WIDTH)
    gn = ret_gn_g[l].reshape(1, R_V_WIDTH)
    ln1g, ln1b = ln1_g[l].reshape(1, D_MODEL), ln1_b[l].reshape(1, D_MODEL)
    ln2g, ln2b = ln2_g[l].reshape(1, D_MODEL), ln2_b[l].reshape(1, D_MODEL)

    m_s = bs * dseq
    xs = x_sample.reshape(m_s, D_MODEL)
    cos_s, sin_s = _rotary_tables(float(PAST_LEN) + jnp.arange(dseq, dtype=F32))
    cos_s, sin_s = jnp.tile(cos_s, (1, bs, 1)), jnp.tile(sin_s, (1, bs, 1))
    h_s, va_s, w_in_b = _in_proj(xs, layer(w_in), cos_s, sin_s, ln_g, ln_b, tm=m_s)

    xp = x_prompt.reshape(bp * seq, D_MODEL)
    cos_p, sin_p = _rotary_tables(0.0 + jnp.arange(seq, dtype=F32))
    h_p, c_s = _in_proj(xp, w_in_b, cos_p, sin_p, ln_g, ln_b, tm=1024,
                        side=(h_s, layer(cache_mem_k), layer(cache_mem_v), dseq))
    b_p, s_p, b_s, s_s = _retention(h_p, h_s, layer(state_ret), gn,
                                    batch_l=bp, seq_l=seq, batch_s=bs, seq_s=dseq)

    m_s_merged, wa_b, wb_b, wc_b = _merge(h_s, b_s, c_s, *_sgu_tables(sgu_w[l], sgu_b[l], min(dseq, CHUNK)),
                                          layer(w_proj_a), layer(w_proj_b), layer(w_proj_c), first_pass=True)
    x1_s, wo_b = _out_ln(m_s_merged, layer(w_out), xs, ln1g, ln1b)
    y_s, wg_b, wu_b, wd_b = _ffn(x1_s, layer(w_ffn_gate), layer(w_ffn_up), layer(w_ffn_down), ln2g, ln2b)

    memp = mem_prompt.reshape(bp * N_MEM, D_MODEL)
    mk_p, mk_pb, mv_p, mv_pb = _mem_kv(memp, layer(w_mem_k), layer(w_mem_v))
    c_p = _mem_attn_prompt(h_p, mk_pb, mv_pb, batch=bp, seq=seq)
    (m_p_merged,) = _merge(h_p, b_p, c_p, *_sgu_tables(sgu_w[l], sgu_b[l], min(seq, CHUNK)),
                           wa_b, wb_b, wc_b, first_pass=False)
    y_p = _tail(m_p_merged, wo_b, xp, ln1g, ln1b, wg_b, wu_b, wd_b, ln2g, ln2b)

    return (
        y_p.reshape(bp, seq, D_MODEL),
        y_s.reshape(bs, dseq, D_MODEL),
        s_p[None],
        mk_p.reshape(1, bp, N_MEM, M_HEADS, M_HEAD_DIM),
        mv_p.reshape(1, bp, N_MEM, M_HEADS, M_HEAD_DIM),
        s_s[None],
        va_s.reshape(1, bs, dseq, A_GROUPS, A_GROUP_DIM),
    )
```

```python
import functools
import math

import jax
import jax.numpy as jnp
import numpy as np
from jax import lax
from jax.experimental import pallas as pl
from jax.experimental.pallas import tpu as pltpu

F32 = jnp.float32
BF16 = jnp.bfloat16

D_MODEL = 2048
DEPTH = 1
PAST_LEN = 16384
CHUNK = 128
A_GROUPS = 4
A_GROUP_DIM = 256
A_WIDTH = A_GROUPS * A_GROUP_DIM
R_HEADS = 8
R_QK_DIM = 128
R_V_DIM = 256
R_QK_WIDTH = R_HEADS * R_QK_DIM
R_V_WIDTH = R_HEADS * R_V_DIM
R_CHUNK = 128
N_MEM = 256
M_HEADS = 4
M_HEAD_DIM = 256
M_WIDTH = M_HEADS * M_HEAD_DIM
N_BRANCH = 3
D_FF = -(-8 * D_MODEL // (3 * 256)) * 256
ALPHA = (2.0 * DEPTH) ** 0.25
ROPE_BASE = 10000.0
LN_EPS = 1e-5

OFF_AU = N_BRANCH * D_MODEL
OFF_AV = OFF_AU + A_WIDTH
OFF_RQ = OFF_AV + A_WIDTH
OFF_RK = OFF_RQ + R_QK_WIDTH
OFF_RV = OFF_RK + R_QK_WIDTH
OFF_RG = OFF_RV + R_V_WIDTH
OFF_MQ = OFF_RG + R_V_WIDTH
IN_WIDTH = OFF_MQ + M_WIDTH

VMEM_LIMIT_BYTES = 56 * 1024 * 1024
LANES = 128


def _params(*sem):
    return pltpu.CompilerParams(dimension_semantics=sem, vmem_limit_bytes=VMEM_LIMIT_BYTES)


def _standardize(x):
    mu = jnp.mean(x, axis=-1, keepdims=True)
    xc = x - mu
    var = jnp.mean(xc * xc, axis=-1, keepdims=True)
    return xc * lax.rsqrt(var + LN_EPS)


IN_TN = 1024
IN_ROW_CHUNK = 256
_J_AU = OFF_AU // IN_TN
_J_AV = OFF_AV // IN_TN
_J_RQ = OFF_RQ // IN_TN
_J_RK = OFF_RK // IN_TN
_J_RV = OFF_RV // IN_TN
_J_RG = OFF_RG // IN_TN
_J_MQ = OFF_MQ // IN_TN
_J_END = IN_WIDTH // IN_TN


def _in_proj_kernel(x_ref, w_ref, cos_ref, sin_ref, lng_ref, lnb_ref, *rest, first_pass, side_seq,
                    side_steps):
    tm = x_ref.shape[0]
    j = pl.program_id(1)
    if first_pass:
        o_ref, va_ref, wb_ref, xb_ref = rest
        wb_ref[...] = w_ref[...].astype(BF16)
    else:
        qs_ref, mk_hbm, mv_hbm, o_ref, cs_ref, xb_ref, kbuf, vbuf, sem = rest
        wb_ref = w_ref

    @pl.when(j == 0)
    def _():
        xb_ref[...] = x_ref[...].astype(BF16)

    def column_range(pred, epilogue, with_side_job=False):
        @pl.when(pred)
        def _():
            if with_side_job:
                _mem_attend_step(pl.program_id(0) * side_steps + j, pl.num_programs(0) * side_steps,
                                 qs_ref, mk_hbm, mv_hbm, cs_ref, kbuf, vbuf, sem, seq=side_seq)
            for r in range(tm // IN_ROW_CHUNK):
                rows = slice(r * IN_ROW_CHUNK, (r + 1) * IN_ROW_CHUNK)
                acc = jnp.dot(xb_ref[rows, :], wb_ref[...], preferred_element_type=F32)
                epilogue(rows, acc)

    def store(fn):
        def epilogue(rows, acc):
            o_ref[rows, :] = fn(acc).astype(BF16)
        return epilogue

    def gelu_layernorm(rows, acc):
        v = jax.nn.gelu(acc)
        for g in range(IN_TN // A_GROUP_DIM):
            cols = slice(g * A_GROUP_DIM, (g + 1) * A_GROUP_DIM)
            vn = _standardize(v[:, cols]) * lng_ref[:, cols] + lnb_ref[:, cols]
            o_ref[rows, cols] = vn.astype(BF16)
            if first_pass:
                va_ref[rows, cols] = vn

    def rotary(rows, acc):
        for hh in range(IN_TN // R_QK_DIM):
            cols = slice(hh * R_QK_DIM, (hh + 1) * R_QK_DIM)
            xh = acc[:, cols]
            r = xh * cos_ref[rows, :] + pltpu.roll(xh, R_QK_DIM // 2, 1) * sin_ref[rows, :]
            o_ref[rows, cols] = r.astype(BF16)

    is_mq = j >= _J_MQ
    raw_scale = jnp.where(is_mq, M_HEAD_DIM ** -0.5, 1.0).astype(F32)
    if first_pass:
        column_range(j < _J_AU, store(jax.nn.sigmoid))
    else:
        column_range(j < side_steps, store(jax.nn.sigmoid), with_side_job=True)
        column_range((j >= side_steps) & (j < _J_AU), store(jax.nn.sigmoid))
    column_range((j >= _J_AU) & (j < _J_AV), store(jax.nn.gelu))
    column_range((j >= _J_AV) & (j < _J_RQ), gelu_layernorm)
    column_range((j >= _J_RQ) & (j < _J_RV), rotary)
    column_range(((j >= _J_RV) & (j < _J_RG)) | is_mq, store(lambda acc: acc * raw_scale))
    column_range((j >= _J_RG) & (j < _J_MQ), store(jax.nn.silu))


def _in_proj(x, w_in, cos_t, sin_t, ln_g, ln_b, *, tm, side=None):
    m = x.shape[0]
    first_pass = side is None
    n_pos_blocks = cos_t.shape[1] // tm
    assert A_WIDTH == IN_TN and R_QK_WIDTH == IN_TN
    table_idx = lambda i, j: (jnp.where(j == _J_RK, 1, 0), i % n_pos_blocks, 0)
    grid = (m // tm, _J_END)

    operands = [x, w_in, cos_t, sin_t, ln_g, ln_b]
    in_specs = [
        pl.BlockSpec((tm, D_MODEL), lambda i, j: (i, 0),
                     pipeline_mode=pl.Buffered(1) if first_pass else None),
        pl.BlockSpec((D_MODEL, IN_TN), lambda i, j: (0, j)),
        pl.BlockSpec((None, tm, R_QK_DIM), table_idx),
        pl.BlockSpec((None, tm, R_QK_DIM), table_idx),
        pl.BlockSpec((1, A_WIDTH), lambda i, j: (0, 0)),
        pl.BlockSpec((1, A_WIDTH), lambda i, j: (0, 0)),
    ]
    out_shape = [jax.ShapeDtypeStruct((m, IN_WIDTH), BF16)]
    out_specs = [pl.BlockSpec((tm, IN_TN), lambda i, j: (i, j))]
    scratch_shapes = [pltpu.VMEM((tm, D_MODEL), BF16)]
    side_seq = side_steps = None
    if first_pass:
        assert m == tm
        out_shape += [jax.ShapeDtypeStruct((m, A_WIDTH), F32), jax.ShapeDtypeStruct(w_in.shape, BF16)]
        out_specs += [pl.BlockSpec((tm, A_WIDTH), lambda i, j: (i, 0)),
                      pl.BlockSpec((D_MODEL, IN_TN), lambda i, j: (0, j))]
    else:
        h_side, mem_k, mem_v, side_seq = side
        side_rows = MEM_SEQS_PER_STEP * side_seq
        side_steps, rem = divmod(mem_k.shape[0], MEM_SEQS_PER_STEP * grid[0])
        assert rem == 0 and 0 < side_steps <= _J_AU
        side_block = lambda i, j: i * side_steps + jnp.minimum(j, side_steps - 1)
        operands += [h_side, mem_k, mem_v]
        in_specs += [pl.BlockSpec((side_rows, M_WIDTH), lambda i, j: (side_block(i, j), OFF_MQ // M_WIDTH)),
                     pl.BlockSpec(memory_space=pl.ANY),
                     pl.BlockSpec(memory_space=pl.ANY)]
        out_shape.append(jax.ShapeDtypeStruct((h_side.shape[0], M_WIDTH), BF16))
        out_specs.append(pl.BlockSpec((side_rows, M_WIDTH), lambda i, j: (side_block(i, j), 0)))
        scratch_shapes += [pltpu.VMEM((2, MEM_SEQS_PER_STEP, M_HEADS, N_MEM, M_HEAD_DIM), F32),
                           pltpu.VMEM((2, MEM_SEQS_PER_STEP, M_HEADS, N_MEM, M_HEAD_DIM), F32),
                           pltpu.SemaphoreType.DMA((2, 2))]
    return pl.pallas_call(
        functools.partial(_in_proj_kernel, first_pass=first_pass, side_seq=side_seq, side_steps=side_steps),
        grid=grid,
        in_specs=in_specs,
        out_specs=out_specs,
        out_shape=out_shape,
        scratch_shapes=scratch_shapes,
        compiler_params=_params("arbitrary", "arbitrary"),
        name="in_proj",
    )(*operands)


def _mem_kv_kernel(x_ref, wk_ref, wv_ref, k_ref, kb_ref, v_ref, vb_ref):
    xb = x_ref[...].astype(BF16)
    for w_ref, o_ref, ob_ref in ((wk_ref, k_ref, kb_ref), (wv_ref, v_ref, vb_ref)):
        acc = jnp.dot(xb, w_ref[...].astype(BF16), preferred_element_type=F32)
        o_ref[...] = acc
        ob_ref[...] = acc.astype(BF16)


def _mem_kv(x, wk, wv):
    m, k = x.shape
    n = wk.shape[1]
    tn = 512
    w_spec = pl.BlockSpec((k, tn), lambda j: (0, j))
    o_spec = pl.BlockSpec((m, tn), lambda j: (0, j))
    return pl.pallas_call(
        _mem_kv_kernel,
        grid=(n // tn,),
        in_specs=[pl.BlockSpec((m, k), lambda j: (0, 0), pipeline_mode=pl.Buffered(1)), w_spec, w_spec],
        out_specs=[o_spec] * 4,
        out_shape=[jax.ShapeDtypeStruct((m, n), F32), jax.ShapeDtypeStruct((m, n), BF16)] * 2,
        compiler_params=_params("arbitrary"),
        name="mem_kv",
    )(x, wk, wv)


def _gated_groupnorm(o, gn, g):
    return (g.astype(F32) * (_standardize(o) * gn)).astype(BF16)


def _ret_long_step(cd_ref, q_ref, k_ref, v_ref, g_ref, dmask_ref, qd_ref, kd_ref, gn_ref,
                   o_ref, sfin_ref, s_scr, *, n_chunks):
    t = pl.program_id(1)

    @pl.when(t == 0)
    def _():
        s_scr[...] = jnp.zeros_like(s_scr)

    state = [s_scr[h] for h in range(R_HEADS)]
    for c in range(n_chunks):
        rows = slice(c * R_CHUNK, (c + 1) * R_CHUNK)
        for h in range(R_HEADS):
            qk = slice(h * R_QK_DIM, (h + 1) * R_QK_DIM)
            vv = slice(h * R_V_DIM, (h + 1) * R_V_DIM)
            q = q_ref[rows, qk]
            k = k_ref[rows, qk]
            v = v_ref[rows, vv]
            s = state[h]
            sc = lax.dot_general(q, k, (((1,), (1,)), ((), ())), preferred_element_type=F32) * dmask_ref[h]
            lhs = jnp.concatenate([sc.astype(BF16), (q.astype(F32) * qd_ref[h]).astype(BF16)], axis=1)
            rhs = jnp.concatenate([v, s.astype(BF16)], axis=0)
            o = jnp.dot(lhs, rhs, preferred_element_type=F32)
            kd = (k.astype(F32) * kd_ref[h]).astype(BF16)
            state[h] = s * cd_ref[h] + lax.dot_general(kd, v, (((0,), (0,)), ((), ())),
                                                       preferred_element_type=F32)
            o_ref[rows, vv] = _gated_groupnorm(o, gn_ref[:, vv], g_ref[rows, vv])
    for h in range(R_HEADS):
        s_scr[h] = state[h]

    @pl.when(t == pl.num_programs(1) - 1)
    def _():
        sfin_ref[0] = s_scr[...]


def _ret_short_step(cd_ref, q_ref, k_ref, v_ref, g_ref, s_ref, dmask_ref, qd_ref, kd_ref, gn_ref,
                    o_ref, snew_ref, *, bb, seq):
    k_all = k_ref[...].astype(F32)
    v_all = v_ref[...].astype(F32)
    for h in range(R_HEADS):
        qk = slice(h * R_QK_DIM, (h + 1) * R_QK_DIM)
        vv = slice(h * R_V_DIM, (h + 1) * R_V_DIM)
        q = q_ref[:, qk]
        sc = lax.dot_general(q, k_ref[:, qk], (((1,), (1,)), ((), ())),
                             preferred_element_type=F32) * dmask_ref[h]
        intra = jnp.dot(sc.astype(BF16), v_ref[:, vv], preferred_element_type=F32)
        q_scaled = q.astype(F32) * qd_ref[h]
        k_scaled = k_all[:, qk] * kd_ref[h]
        inter = []
        for b in range(bb):
            rows = slice(b * seq, (b + 1) * seq)
            s = s_ref[b, h]
            inter.append(jnp.dot(q_scaled[rows].astype(BF16), s.astype(BF16), preferred_element_type=F32))
            snew_ref[b, h] = s * cd_ref[h] + lax.dot_general(
                k_scaled[rows].astype(BF16), v_all[rows, vv].astype(BF16), (((0,), (0,)), ((), ())),
                preferred_element_type=F32)
        o = intra + jnp.concatenate(inter, axis=0)
        o_ref[:, vv] = _gated_groupnorm(o, gn_ref[:, vv], g_ref[:, vv])


def _retention_kernel(cdl_ref, cds_ref, ql_ref, kl_ref, vl_ref, gl_ref, dml_ref, qdl_ref, kdl_ref,
                      qs_ref, ks_ref, vs_ref, gs_ref, s_ref, dms_ref, qds_ref, kds_ref, gn_ref,
                      ol_ref, sfin_ref, os_ref, snew_ref, s_scr, *, n_chunks, bb, seq):
    _ret_long_step(cdl_ref, ql_ref, kl_ref, vl_ref, gl_ref, dml_ref, qdl_ref, kdl_ref, gn_ref,
                   ol_ref, sfin_ref, s_scr, n_chunks=n_chunks)
    _ret_short_step(cds_ref, qs_ref, ks_ref, vs_ref, gs_ref, s_ref, dms_ref, qds_ref, kds_ref, gn_ref,
                    os_ref, snew_ref, bb=bb, seq=seq)


def _retention(h_long, h_short, state, gn, *, batch_l, seq_l, batch_s, seq_s):
    tr = 256
    n_t = seq_l // tr
    n_steps = batch_l * n_t
    bb, rem = divmod(batch_s, n_steps)
    assert rem == 0 and (bb * seq_s) % 16 == 0
    ts = bb * seq_s
    tabs_l = _retention_tables(min(seq_l, R_CHUNK))
    tabs_s = _retention_tables(min(seq_s, R_CHUNK))
    f32 = lambda t: jnp.asarray(t, F32)
    lane_bcast = lambda t, rows: f32(np.broadcast_to(t, (R_HEADS, rows, R_QK_DIM)))
    blk = np.arange(ts) // seq_s
    dmask_s = np.where((blk[:, None] == blk[None, :])[None], np.tile(tabs_s["dmask"], (1, bb, bb)), 0.0)
    qd_s = lane_bcast(np.tile(tabs_s["q_decay"], (1, bb, 1)), ts)
    kd_s = lane_bcast(np.tile(tabs_s["k_decay"], (1, bb, 1)), ts)

    step = lambda b, t: b * n_t + t
    const3 = lambda b, t: (0, 0, 0)
    smem = pl.BlockSpec(memory_space=pltpu.SMEM)
    state_spec = pl.BlockSpec((bb, R_HEADS, R_QK_DIM, R_V_DIM), lambda b, t: (step(b, t), 0, 0, 0))
    return pl.pallas_call(
        functools.partial(_retention_kernel, n_chunks=tr // R_CHUNK, bb=bb, seq=seq_s),
        grid=(batch_l, n_t),
        in_specs=[
            smem, smem,
            pl.BlockSpec((tr, R_QK_WIDTH), lambda b, t: (step(b, t), OFF_RQ // R_QK_WIDTH)),
            pl.BlockSpec((tr, R_QK_WIDTH), lambda b, t: (step(b, t), OFF_RK // R_QK_WIDTH)),
            pl.BlockSpec((tr, R_V_WIDTH), lambda b, t: (step(b, t), OFF_RV // R_V_WIDTH)),
            pl.BlockSpec((tr, R_V_WIDTH), lambda b, t: (step(b, t), OFF_RG // R_V_WIDTH)),
            pl.BlockSpec((R_HEADS, R_CHUNK, R_CHUNK), const3),
            pl.BlockSpec((R_HEADS, R_CHUNK, R_QK_DIM), const3),
            pl.BlockSpec((R_HEADS, R_CHUNK, R_QK_DIM), const3),
            pl.BlockSpec((ts, R_QK_WIDTH), lambda b, t: (step(b, t), OFF_RQ // R_QK_WIDTH)),
            pl.BlockSpec((ts, R_QK_WIDTH), lambda b, t: (step(b, t), OFF_RK // R_QK_WIDTH)),
            pl.BlockSpec((ts, R_V_WIDTH), lambda b, t: (step(b, t), OFF_RV // R_V_WIDTH)),
            pl.BlockSpec((ts, R_V_WIDTH), lambda b, t: (step(b, t), OFF_RG // R_V_WIDTH)),
            state_spec,
            pl.BlockSpec((R_HEADS, ts, ts), const3),
            pl.BlockSpec((R_HEADS, ts, R_QK_DIM), const3),
            pl.BlockSpec((R_HEADS, ts, R_QK_DIM), const3),
            pl.BlockSpec((1, R_V_WIDTH), lambda b, t: (0, 0)),
        ],
        out_specs=[
            pl.BlockSpec((tr, R_V_WIDTH), lambda b, t: (step(b, t), 0)),
            pl.BlockSpec((1, R_HEADS, R_QK_DIM, R_V_DIM), lambda b, t: (b, 0, 0, 0)),
            pl.BlockSpec((ts, R_V_WIDTH), lambda b, t: (step(b, t), 0)),
            state_spec,
        ],
        out_shape=[
            jax.ShapeDtypeStruct((batch_l * seq_l, R_V_WIDTH), BF16),
            jax.ShapeDtypeStruct((batch_l, R_HEADS, R_QK_DIM, R_V_DIM), F32),
            jax.ShapeDtypeStruct((batch_s * seq_s, R_V_WIDTH), BF16),
            jax.ShapeDtypeStruct((batch_s, R_HEADS, R_QK_DIM, R_V_DIM), F32),
        ],
        scratch_shapes=[pltpu.VMEM((R_HEADS, R_QK_DIM, R_V_DIM), F32)],
        compiler_params=_params("arbitrary", "arbitrary"),
        name="retention",
    )(f32(tabs_l["chunk_decay"]), f32(tabs_s["chunk_decay"]), h_long, h_long, h_long, h_long,
      f32(tabs_l["dmask"]), lane_bcast(tabs_l["q_decay"], R_CHUNK), lane_bcast(tabs_l["k_decay"], R_CHUNK),
      h_short, h_short, h_short, h_short, state, f32(dmask_s), qd_s, kd_s, gn)


def _retention_tables(c):
    log_g = np.log1p(-np.exp2(-5.0 - np.arange(R_HEADS, dtype=np.float64)))
    idx = np.arange(c, dtype=np.float64)
    diff = idx[:, None] - idx[None, :]
    causal = diff >= 0
    dmask = np.where(causal[None], np.exp(np.where(causal, diff, 0.0)[None] * log_g[:, None, None]), 0.0)
    q_decay = np.exp((idx[:, None] + 1.0) * log_g[None, :])
    k_decay = np.exp((c - 1.0 - idx)[:, None] * log_g[None, :])
    chunk_decay = np.exp(c * log_g)
    return {
        "dmask": dmask,
        "q_decay": q_decay.T[:, :, None],
        "k_decay": k_decay.T[:, :, None],
        "chunk_decay": chunk_decay,
    }


def _softmax_rows(s):
    e = jnp.exp(s - jnp.max(s, axis=-1, keepdims=True))
    return e, jnp.sum(e, axis=-1, keepdims=True)


def _mem_prompt_kernel(q_ref, mk_ref, mv_ref, o_ref):
    for h in range(M_HEADS):
        cols = slice(h * M_HEAD_DIM, (h + 1) * M_HEAD_DIM)
        s = lax.dot_general(q_ref[:, cols], mk_ref[:, cols], (((1,), (1,)), ((), ())),
                            preferred_element_type=F32)
        e, l = _softmax_rows(s)
        o = jnp.dot(e.astype(BF16), mv_ref[:, cols], preferred_element_type=F32) / l
        o_ref[:, cols] = o.astype(BF16)


def _mem_attn_prompt(h_act, mk, mv, *, batch, seq):
    tq = 512
    n_t = seq // tq
    return pl.pallas_call(
        _mem_prompt_kernel,
        grid=(batch, n_t),
        in_specs=[
            pl.BlockSpec((tq, M_WIDTH), lambda b, t: (b * n_t + t, OFF_MQ // M_WIDTH)),
            pl.BlockSpec((N_MEM, M_WIDTH), lambda b, t: (b, 0)),
            pl.BlockSpec((N_MEM, M_WIDTH), lambda b, t: (b, 0)),
        ],
        out_specs=pl.BlockSpec((tq, M_WIDTH), lambda b, t: (b * n_t + t, 0)),
        out_shape=jax.ShapeDtypeStruct((batch * seq, M_WIDTH), BF16),
        compiler_params=_params("arbitrary", "arbitrary"),
        name="mem_attn_prompt",
    )(h_act, mk, mv)


MEM_SEQS_PER_STEP = 4


def _mem_attend_step(step, n_steps, q_ref, mk_hbm, mv_hbm, o_ref, kbuf, vbuf, sem, *, seq):
    bb = MEM_SEQS_PER_STEP
    pairs = [(b, h) for b in range(bb) for h in range(M_HEADS)]

    def copies(st, slot):
        out = []
        for b, h in pairs:
            out.append(pltpu.make_async_copy(mk_hbm.at[st * bb + b, :, h, :], kbuf.at[slot, b, h],
                                             sem.at[0, slot]))
            out.append(pltpu.make_async_copy(mv_hbm.at[st * bb + b, :, h, :], vbuf.at[slot, b, h],
                                             sem.at[1, slot]))
        return out

    slot = step % 2

    @pl.when(step == 0)
    def _():
        for c in copies(0, 0):
            c.start()

    @pl.when(step + 1 < n_steps)
    def _():
        for c in copies(step + 1, 1 - slot):
            c.start()

    for c in copies(step, slot):
        c.wait()
    q_all = q_ref[...].astype(F32)
    scores = []
    for b, h in pairs:
        q = q_all[b * seq:(b + 1) * seq, h * M_HEAD_DIM:(h + 1) * M_HEAD_DIM].astype(BF16)
        scores.append(lax.dot_general(q, kbuf[slot, b, h].astype(BF16), (((1,), (1,)), ((), ())),
                                      preferred_element_type=F32))
    e, l = _softmax_rows(jnp.concatenate(scores, axis=0))
    inv_l = 1.0 / l
    outs = []
    for n, (b, h) in enumerate(pairs):
        rows = slice(n * seq, (n + 1) * seq)
        o = jnp.dot(e[rows].astype(BF16), vbuf[slot, b, h].astype(BF16), preferred_element_type=F32)
        outs.append(o * inv_l[rows])
    out_rows = [jnp.concatenate(outs[b * M_HEADS:(b + 1) * M_HEADS], axis=1) for b in range(bb)]
    o_ref[...] = jnp.concatenate(out_rows, axis=0).astype(BF16)


MERGE_ROW_CHUNK = 256


def _merge_kernel(ga_ref, gb_ref, gc_ref, u_ref, va_ref, sw_ref, sb_ref, b_ref, c_ref,
                  wa_ref, wb_ref, wc_ref, o_ref, *rest, tm, first_pass):
    if first_pass:
        wab_ref, wbb_ref, wcb_ref, a_scr = rest
        wab_ref[...] = wa_ref[...].astype(BF16)
        wbb_ref[...] = wb_ref[...].astype(BF16)
        wcb_ref[...] = wc_ref[...].astype(BF16)
        wa_ref, wb_ref, wc_ref = wab_ref, wbb_ref, wcb_ref
    else:
        (a_scr,) = rest

    @pl.when(pl.program_id(1) == 0)
    def _():
        for r in range(tm // CHUNK):
            rows = slice(r * CHUNK, (r + 1) * CHUNK)
            for g in range(A_GROUPS):
                cols = slice(g * A_GROUP_DIM, (g + 1) * A_GROUP_DIM)
                z = jnp.dot(sw_ref[g], va_ref[rows, cols], preferred_element_type=F32) + sb_ref[g]
                a_scr[rows, cols] = (u_ref[rows, cols].astype(F32) * z).astype(BF16)

    for r in range(tm // MERGE_ROW_CHUNK):
        rows = slice(r * MERGE_ROW_CHUNK, (r + 1) * MERGE_ROW_CHUNK)
        ya = jnp.dot(a_scr[rows, :], wa_ref[...], preferred_element_type=F32)
        yb = jnp.dot(b_ref[rows, :], wb_ref[...], preferred_element_type=F32)
        yc = jnp.dot(c_ref[rows, :], wc_ref[...], preferred_element_type=F32)
        merged = (ga_ref[rows, :].astype(F32) * ya + gb_ref[rows, :].astype(F32) * yb
                  + gc_ref[rows, :].astype(F32) * yc)
        o_ref[rows, :] = merged.astype(BF16)


def _merge(h_act, b_act, c_act, sgu_w, sgu_b, wa, wb, wc, *, first_pass):
    m = h_act.shape[0]
    tm, tn = 1024, 512
    n_j = D_MODEL // tn
    row_tile_mode = pl.Buffered(1) if first_pass else None
    w_specs = [pl.BlockSpec((A_WIDTH, tn), lambda i, j: (0, j)),
               pl.BlockSpec((R_V_WIDTH, tn), lambda i, j: (0, j)),
               pl.BlockSpec((M_WIDTH, tn), lambda i, j: (0, j))]
    out_specs = [pl.BlockSpec((tm, tn), lambda i, j: (i, j))]
    out_shape = [jax.ShapeDtypeStruct((m, D_MODEL), BF16)]
    if first_pass:
        assert m == tm
        out_specs += w_specs
        out_shape += [jax.ShapeDtypeStruct(w.shape, BF16) for w in (wa, wb, wc)]
    return pl.pallas_call(
        functools.partial(_merge_kernel, tm=tm, first_pass=first_pass),
        grid=(m // tm, n_j),
        in_specs=[
            pl.BlockSpec((tm, tn), lambda i, j: (i, j)),
            pl.BlockSpec((tm, tn), lambda i, j: (i, n_j + j)),
            pl.BlockSpec((tm, tn), lambda i, j: (i, 2 * n_j + j)),
            pl.BlockSpec((tm, A_WIDTH), lambda i, j: (i, OFF_AU // A_WIDTH), pipeline_mode=row_tile_mode),
            pl.BlockSpec((tm, A_WIDTH), lambda i, j: (i, OFF_AV // A_WIDTH), pipeline_mode=row_tile_mode),
            pl.BlockSpec((A_GROUPS, CHUNK, CHUNK), lambda i, j: (0, 0, 0)),
            pl.BlockSpec((A_GROUPS, CHUNK, 1), lambda i, j: (0, 0, 0)),
            pl.BlockSpec((tm, R_V_WIDTH), lambda i, j: (i, 0), pipeline_mode=row_tile_mode),
            pl.BlockSpec((tm, M_WIDTH), lambda i, j: (i, 0), pipeline_mode=row_tile_mode),
        ] + w_specs,
        out_specs=out_specs,
        out_shape=out_shape,
        scratch_shapes=[pltpu.VMEM((tm, A_WIDTH), BF16)],
        compiler_params=_params("arbitrary", "arbitrary"),
        name="merge",
    )(h_act, h_act, h_act, h_act, h_act, sgu_w, sgu_b, b_act, c_act, wa, wb, wc)


OUT_ROW_CHUNK = 256


def _out_ln_kernel(m_ref, w_ref, x_ref, g_ref, b_ref, o_ref, wb_ref):
    wb_ref[...] = w_ref[...].astype(BF16)
    for r in range(m_ref.shape[0] // OUT_ROW_CHUNK):
        rows = slice(r * OUT_ROW_CHUNK, (r + 1) * OUT_ROW_CHUNK)
        y = jnp.dot(m_ref[rows, :], wb_ref[...], preferred_element_type=F32)
        o_ref[rows, :] = _standardize(ALPHA * x_ref[rows, :] + y) * g_ref[...] + b_ref[...]


def _out_ln(merged, w_out, x, ln_g, ln_b):
    tm = x.shape[0]
    full = lambda i: (0, 0)
    return pl.pallas_call(
        _out_ln_kernel,
        grid=(1,),
        in_specs=[
            pl.BlockSpec((tm, D_MODEL), full),
            pl.BlockSpec((D_MODEL, D_MODEL), full),
            pl.BlockSpec((tm, D_MODEL), full),
            pl.BlockSpec((1, D_MODEL), full),
            pl.BlockSpec((1, D_MODEL), full),
        ],
        out_specs=[pl.BlockSpec((tm, D_MODEL), full), pl.BlockSpec((D_MODEL, D_MODEL), full)],
        out_shape=[jax.ShapeDtypeStruct((tm, D_MODEL), F32), jax.ShapeDtypeStruct(w_out.shape, BF16)],
        compiler_params=_params("arbitrary"),
        name="out_ln",
    )(merged, w_out, x, ln_g, ln_b)


def _ffn_kernel(x_ref, wg_ref, wu_ref, wd_ref, g_ref, b_ref, o_ref, wgb_ref, wub_ref, wdb_ref,
                xb_ref, acc_ref):
    wgb_ref[...] = wg_ref[...].astype(BF16)
    wub_ref[...] = wu_ref[...].astype(BF16)
    wdb_ref[...] = wd_ref[...].astype(BF16)
    f = pl.program_id(1)

    @pl.when(f == 0)
    def _():
        xb_ref[...] = x_ref[...].astype(BF16)
        acc_ref[...] = jnp.zeros_like(acc_ref)

    xb = xb_ref[...]
    gate = jnp.dot(xb, wgb_ref[...], preferred_element_type=F32)
    up = jnp.dot(xb, wub_ref[...], preferred_element_type=F32)
    act = (jax.nn.silu(gate) * up).astype(BF16)
    acc_ref[...] += jnp.dot(act, wdb_ref[...], preferred_element_type=F32)

    @pl.when(f == pl.num_programs(1) - 1)
    def _():
        o_ref[...] = _standardize(ALPHA * x_ref[...] + acc_ref[...]) * g_ref[...] + b_ref[...]


def _ffn(x1, wg, wu, wd, ln_g, ln_b):
    tm = x1.shape[0]
    tf = 256
    one_buffer = pl.Buffered(1)
    out_shape = [jax.ShapeDtypeStruct((tm, D_MODEL), F32)] + [jax.ShapeDtypeStruct(w.shape, BF16)
                                                             for w in (wg, wu, wd)]
    out_specs = [pl.BlockSpec((tm, D_MODEL), lambda i, f: (i, 0), pipeline_mode=one_buffer),
                 pl.BlockSpec((D_MODEL, tf), lambda i, f: (0, f)),
                 pl.BlockSpec((D_MODEL, tf), lambda i, f: (0, f)),
                 pl.BlockSpec((tf, D_MODEL), lambda i, f: (f, 0))]
    return pl.pallas_call(
        _ffn_kernel,
        grid=(1, D_FF // tf),
        in_specs=[
            pl.BlockSpec((tm, D_MODEL), lambda i, f: (i, 0), pipeline_mode=one_buffer),
            pl.BlockSpec((D_MODEL, tf), lambda i, f: (0, f)),
            pl.BlockSpec((D_MODEL, tf), lambda i, f: (0, f)),
            pl.BlockSpec((tf, D_MODEL), lambda i, f: (f, 0)),
            pl.BlockSpec((1, D_MODEL), lambda i, f: (0, 0)),
            pl.BlockSpec((1, D_MODEL), lambda i, f: (0, 0)),
        ],
        out_specs=out_specs,
        out_shape=out_shape,
        scratch_shapes=[pltpu.VMEM((tm, D_MODEL), BF16), pltpu.VMEM((tm, D_MODEL), F32)],
        compiler_params=_params("arbitrary", "arbitrary"),
        name="ffn",
    )(x1, wg, wu, wd, ln_g, ln_b)


TAIL_ROW_CHUNK = 256


def _tail_kernel(m_ref, wo_ref, x_ref, g1_ref, b1_ref, wg_ref, wu_ref, wd_ref, g2_ref, b2_ref,
                 o_ref, x1_scr, x1b_scr):
    f = pl.program_id(1)
    n_f = pl.num_programs(1)
    chunks = [slice(r * TAIL_ROW_CHUNK, (r + 1) * TAIL_ROW_CHUNK)
              for r in range(m_ref.shape[0] // TAIL_ROW_CHUNK)]

    def swiglu(rows):
        xb = x1b_scr[rows, :]
        gate = jnp.dot(xb, wg_ref[...], preferred_element_type=F32)
        up = jnp.dot(xb, wu_ref[...], preferred_element_type=F32)
        act = (jax.nn.silu(gate) * up).astype(BF16)
        return jnp.dot(act, wd_ref[...], preferred_element_type=F32)

    @pl.when(f == 0)
    def _():
        for rows in chunks:
            y = jnp.dot(m_ref[rows, :], wo_ref[...], preferred_element_type=F32)
            x1 = _standardize(ALPHA * x_ref[rows, :] + y) * g1_ref[...] + b1_ref[...]
            x1_scr[rows, :] = x1
            x1b_scr[rows, :] = x1.astype(BF16)
        for rows in chunks:
            o_ref[rows, :] = swiglu(rows)

    @pl.when((f > 0) & (f < n_f - 1))
    def _():
        o_ref[...] += swiglu(slice(None))

    @pl.when(f == n_f - 1)
    def _():
        for rows in chunks:
            ff = o_ref[rows, :] + swiglu(rows)
            o_ref[rows, :] = _standardize(ALPHA * x1_scr[rows, :] + ff) * g2_ref[...] + b2_ref[...]


def _tail(merged, w_out, x, ln1_g, ln1_b, wg, wu, wd, ln2_g, ln2_b):
    m = x.shape[0]
    tm, tf = 512, 512
    assert D_FF // tf >= 2
    row = lambda i, f: (i, 0)
    const = lambda i, f: (0, 0)
    return pl.pallas_call(
        _tail_kernel,
        grid=(m // tm, D_FF // tf),
        in_specs=[
            pl.BlockSpec((tm, D_MODEL), row),
            pl.BlockSpec((D_MODEL, D_MODEL), const, pipeline_mode=pl.Buffered(1)),
            pl.BlockSpec((tm, D_MODEL), row),
            pl.BlockSpec((1, D_MODEL), const),
            pl.BlockSpec((1, D_MODEL), const),
            pl.BlockSpec((D_MODEL, tf), lambda i, f: (0, f)),
            pl.BlockSpec((D_MODEL, tf), lambda i, f: (0, f)),
            pl.BlockSpec((tf, D_MODEL), lambda i, f: (f, 0)),
            pl.BlockSpec((1, D_MODEL), const),
            pl.BlockSpec((1, D_MODEL), const),
        ],
        out_specs=pl.BlockSpec((tm, D_MODEL), row),
        out_shape=jax.ShapeDtypeStruct((m, D_MODEL), F32),
        scratch_shapes=[pltpu.VMEM((tm, D_MODEL), F32), pltpu.VMEM((tm, D_MODEL), BF16)],
        compiler_params=_params("arbitrary", "arbitrary"),
        name="tail",
    )(merged, w_out, x, ln1_g, ln1_b, wg, wu, wd, ln2_g, ln2_b)


def _rotary_tables(pos_start, n_pos, reps):
    half = R_QK_DIM // 2
    inv = ROPE_BASE ** (-np.arange(half, dtype=np.float64) / half)
    ang = (pos_start + np.arange(n_pos, dtype=np.float64))[:, None] * inv[None, :]
    cos = np.cos(ang)
    sin = np.sin(ang)
    cos2 = np.concatenate([cos, cos], axis=-1)
    sin2 = np.concatenate([-sin, sin], axis=-1)
    k_scale = R_QK_DIM ** -0.5
    table = lambda t: jnp.asarray(np.tile(np.stack([t, t * k_scale]), (1, reps, 1)), F32)
    return table(cos2), table(sin2)


def _sgu_tables(sgu_w, sgu_b, c):
    w = jnp.tril(sgu_w[:, :c, :c])
    b = sgu_b[:, :c]
    reps = CHUNK // c
    if reps > 1:
        blk = jnp.arange(CHUNK) // c
        w = jnp.where((blk[:, None] == blk[None, :])[None], jnp.tile(w, (1, reps, reps)), 0.0)
        b = jnp.tile(b, (1, reps))
    return w.astype(BF16), b[:, :, None]


def kernel(x_prompt, x_sample, mem_prompt, state_ret, cache_mem_k, cache_mem_v, w_in, sgu_ln_g, sgu_ln_b, sgu_w, sgu_b, w_proj_a, ret_gn_g, w_proj_b, w_mem_k, w_mem_v, w_proj_c, w_out, ln1_g, ln1_b, w_ffn_gate, w_ffn_up, w_ffn_down, ln2_g, ln2_b):
    assert w_in.shape[0] == DEPTH == 1
    bp, seq, _ = x_prompt.shape
    bs, dseq, _ = x_sample.shape
    l = 0
    layer = lambda a: a.reshape(a.shape[1:])
    ln_g = sgu_ln_g[l].reshape(1, A_WIDTH)
    ln_b = sgu_ln_b[l].reshape(1, A_WIDTH)
    gn = ret_gn_g[l].reshape(1, R_V_WIDTH)
    ln1g, ln1b = ln1_g[l].reshape(1, D_MODEL), ln1_b[l].reshape(1, D_MODEL)
    ln2g, ln2b = ln2_g[l].reshape(1, D_MODEL), ln2_b[l].reshape(1, D_MODEL)

    m_s = bs * dseq
    xs = x_sample.reshape(m_s, D_MODEL)
    cos_s, sin_s = _rotary_tables(float(PAST_LEN), dseq, reps=bs)
    h_s, va_s, w_in_b = _in_proj(xs, layer(w_in), cos_s, sin_s, ln_g, ln_b, tm=m_s)

    xp = x_prompt.reshape(bp * seq, D_MODEL)
    cos_p, sin_p = _rotary_tables(0.0, seq, reps=1)
    h_p, c_s = _in_proj(xp, w_in_b, cos_p, sin_p, ln_g, ln_b, tm=1024,
                        side=(h_s, layer(cache_mem_k), layer(cache_mem_v), dseq))
    b_p, s_p, b_s, s_s = _retention(h_p, h_s, layer(state_ret), gn,
                                    batch_l=bp, seq_l=seq, batch_s=bs, seq_s=dseq)

    m_s_merged, wa_b, wb_b, wc_b = _merge(h_s, b_s, c_s, *_sgu_tables(sgu_w[l], sgu_b[l], min(dseq, CHUNK)),
                                          layer(w_proj_a), layer(w_proj_b), layer(w_proj_c), first_pass=True)
    x1_s, wo_b = _out_ln(m_s_merged, layer(w_out), xs, ln1g, ln1b)
    y_s, wg_b, wu_b, wd_b = _ffn(x1_s, layer(w_ffn_gate), layer(w_ffn_up), layer(w_ffn_down), ln2g, ln2b)

    memp = mem_prompt.reshape(bp * N_MEM, D_MODEL)
    mk_p, mk_pb, mv_p, mv_pb = _mem_kv(memp, layer(w_mem_k), layer(w_mem_v))
    c_p = _mem_attn_prompt(h_p, mk_pb, mv_pb, batch=bp, seq=seq)
    (m_p_merged,) = _merge(h_p, b_p, c_p, *_sgu_tables(sgu_w[l], sgu_b[l], min(seq, CHUNK)),
                           wa_b, wb_b, wc_b, first_pass=False)
    y_p = _tail(m_p_merged, wo_b, xp, ln1g, ln1b, wg_b, wu_b, wd_b, ln2g, ln2b)

    return (
        y_p.reshape(bp, seq, D_MODEL),
        y_s.reshape(bs, dseq, D_MODEL),
        s_p[None],
        mk_p.reshape(1, bp, N_MEM, M_HEADS, M_HEAD_DIM),
        mv_p.reshape(1, bp, N_MEM, M_HEADS, M_HEAD_DIM),
        s_s[None],
        va_s.reshape(1, bs, dseq, A_GROUPS, A_GROUP_DIM),
    )
```

```python
import functools

import jax
import jax.numpy as jnp
import numpy as np
from jax import lax
from jax.experimental import pallas as pl
from jax.experimental.pallas import tpu as pltpu

F32 = jnp.float32
BF16 = jnp.bfloat16

D_MODEL = 2048
DEPTH = 1
PAST_LEN = 16384
CHUNK = 128
A_GROUPS = 4
A_GROUP_DIM = 256
A_WIDTH = A_GROUPS * A_GROUP_DIM
R_HEADS = 8
R_QK_DIM = 128
R_V_DIM = 256
R_QK_WIDTH = R_HEADS * R_QK_DIM
R_V_WIDTH = R_HEADS * R_V_DIM
R_CHUNK = 128
N_MEM = 256
M_HEADS = 4
M_HEAD_DIM = 256
M_WIDTH = M_HEADS * M_HEAD_DIM
N_BRANCH = 3
D_FF = -(-8 * D_MODEL // (3 * 256)) * 256
ALPHA = (2.0 * DEPTH) ** 0.25
ROPE_BASE = 10000.0
LN_EPS = 1e-5

OFF_AU = N_BRANCH * D_MODEL
OFF_AV = OFF_AU + A_WIDTH
OFF_RQ = OFF_AV + A_WIDTH
OFF_RK = OFF_RQ + R_QK_WIDTH
OFF_RV = OFF_RK + R_QK_WIDTH
OFF_RG = OFF_RV + R_V_WIDTH
OFF_MQ = OFF_RG + R_V_WIDTH
IN_WIDTH = OFF_MQ + M_WIDTH

VMEM_LIMIT_BYTES = 56 * 1024 * 1024
RETENTION_VMEM_LIMIT_BYTES = 60 * 1024 * 1024
MERGE_VMEM_LIMIT_BYTES = 62 * 1024 * 1024


def _params(*sem, vmem_limit_bytes=VMEM_LIMIT_BYTES):
    return pltpu.CompilerParams(dimension_semantics=sem, vmem_limit_bytes=vmem_limit_bytes)


def _standardize(x):
    mu = jnp.mean(x, axis=-1, keepdims=True)
    xc = x - mu
    var = jnp.mean(xc * xc, axis=-1, keepdims=True)
    return xc * lax.rsqrt(var + LN_EPS)


IN_TN = 1024
IN_ROW_CHUNK = 256
_J_AU = OFF_AU // IN_TN
_J_AV = OFF_AV // IN_TN
_J_RQ = OFF_RQ // IN_TN
_J_RK = OFF_RK // IN_TN
_J_RV = OFF_RV // IN_TN
_J_RG = OFF_RG // IN_TN
_J_MQ = OFF_MQ // IN_TN
_J_END = IN_WIDTH // IN_TN


def _in_proj_kernel(x_ref, w_ref, cos_ref, sin_ref, lng_ref, lnb_ref, *rest, first_pass, side_seq,
                    side_steps):
    tm = x_ref.shape[0]
    j = pl.program_id(1)
    if first_pass:
        o_ref, va_ref, wb_ref, xb_ref = rest
        wb_ref[...] = w_ref[...].astype(BF16)
    else:
        qs_ref, mk_hbm, mv_hbm, o_ref, cs_ref, xb_ref, kbuf, vbuf, sem = rest
        wb_ref = w_ref

    @pl.when(j == 0)
    def _():
        xb_ref[...] = x_ref[...].astype(BF16)

    def column_range(pred, epilogue, with_side_job=False):
        @pl.when(pred)
        def _():
            if with_side_job:
                _mem_attend_step(pl.program_id(0) * side_steps + j, pl.num_programs(0) * side_steps,
                                 qs_ref, mk_hbm, mv_hbm, cs_ref, kbuf, vbuf, sem, seq=side_seq)
            for r in range(tm // IN_ROW_CHUNK):
                rows = slice(r * IN_ROW_CHUNK, (r + 1) * IN_ROW_CHUNK)
                acc = jnp.dot(xb_ref[rows, :], wb_ref[...], preferred_element_type=F32)
                epilogue(rows, acc)

    def store(fn):
        def epilogue(rows, acc):
            o_ref[rows, :] = fn(acc).astype(BF16)
        return epilogue

    def gelu_layernorm(rows, acc):
        v = jax.nn.gelu(acc)
        for g in range(IN_TN // A_GROUP_DIM):
            cols = slice(g * A_GROUP_DIM, (g + 1) * A_GROUP_DIM)
            vn = _standardize(v[:, cols]) * lng_ref[:, cols] + lnb_ref[:, cols]
            o_ref[rows, cols] = vn.astype(BF16)
            if first_pass:
                va_ref[rows, cols] = vn

    def rotary(rows, acc):
        for hh in range(IN_TN // R_QK_DIM):
            cols = slice(hh * R_QK_DIM, (hh + 1) * R_QK_DIM)
            xh = acc[:, cols]
            r = xh * cos_ref[rows, :] + pltpu.roll(xh, R_QK_DIM // 2, 1) * sin_ref[rows, :]
            o_ref[rows, cols] = r.astype(BF16)

    is_mq = j >= _J_MQ
    raw_scale = jnp.where(is_mq, M_HEAD_DIM ** -0.5, 1.0).astype(F32)
    if first_pass:
        column_range(j < _J_AU, store(jax.nn.sigmoid))
    else:
        column_range(j < side_steps, store(jax.nn.sigmoid), with_side_job=True)
        column_range((j >= side_steps) & (j < _J_AU), store(jax.nn.sigmoid))
    column_range((j >= _J_AU) & (j < _J_AV), store(jax.nn.gelu))
    column_range((j >= _J_AV) & (j < _J_RQ), gelu_layernorm)
    column_range((j >= _J_RQ) & (j < _J_RV), rotary)
    column_range(((j >= _J_RV) & (j < _J_RG)) | is_mq, store(lambda acc: acc * raw_scale))
    column_range((j >= _J_RG) & (j < _J_MQ), store(jax.nn.silu))


def _in_proj(x, w_in, cos_t, sin_t, ln_g, ln_b, *, tm, side=None):
    m = x.shape[0]
    first_pass = side is None
    n_pos_blocks = cos_t.shape[1] // tm
    assert A_WIDTH == IN_TN and R_QK_WIDTH == IN_TN
    table_idx = lambda i, j: (jnp.where(j == _J_RK, 1, 0), i % n_pos_blocks, 0)
    grid = (m // tm, _J_END)

    operands = [x, w_in, cos_t, sin_t, ln_g, ln_b]
    in_specs = [
        pl.BlockSpec((tm, D_MODEL), lambda i, j: (i, 0),
                     pipeline_mode=pl.Buffered(1) if first_pass else None),
        pl.BlockSpec((D_MODEL, IN_TN), lambda i, j: (0, j)),
        pl.BlockSpec((None, tm, R_QK_DIM), table_idx),
        pl.BlockSpec((None, tm, R_QK_DIM), table_idx),
        pl.BlockSpec((1, A_WIDTH), lambda i, j: (0, 0)),
        pl.BlockSpec((1, A_WIDTH), lambda i, j: (0, 0)),
    ]
    out_shape = [jax.ShapeDtypeStruct((m, IN_WIDTH), BF16)]
    out_specs = [pl.BlockSpec((tm, IN_TN), lambda i, j: (i, j))]
    scratch_shapes = [pltpu.VMEM((tm, D_MODEL), BF16)]
    side_seq = side_steps = None
    if first_pass:
        assert m == tm
        out_shape += [jax.ShapeDtypeStruct((m, A_WIDTH), F32), jax.ShapeDtypeStruct(w_in.shape, BF16)]
        out_specs += [pl.BlockSpec((tm, A_WIDTH), lambda i, j: (i, 0)),
                      pl.BlockSpec((D_MODEL, IN_TN), lambda i, j: (0, j))]
    else:
        h_side, mem_k, mem_v, side_seq = side
        side_rows = MEM_SEQS_PER_STEP * side_seq
        side_steps, rem = divmod(mem_k.shape[0], MEM_SEQS_PER_STEP * grid[0])
        assert rem == 0 and 0 < side_steps <= _J_AU
        side_block = lambda i, j: i * side_steps + jnp.minimum(j, side_steps - 1)
        operands += [h_side, mem_k, mem_v]
        in_specs += [pl.BlockSpec((side_rows, M_WIDTH), lambda i, j: (side_block(i, j), OFF_MQ // M_WIDTH)),
                     pl.BlockSpec(memory_space=pl.ANY),
                     pl.BlockSpec(memory_space=pl.ANY)]
        out_shape.append(jax.ShapeDtypeStruct((h_side.shape[0], M_WIDTH), BF16))
        out_specs.append(pl.BlockSpec((side_rows, M_WIDTH), lambda i, j: (side_block(i, j), 0)))
        scratch_shapes += [pltpu.VMEM((2, MEM_SEQS_PER_STEP, M_HEADS, N_MEM, M_HEAD_DIM), F32),
                           pltpu.VMEM((2, MEM_SEQS_PER_STEP, M_HEADS, N_MEM, M_HEAD_DIM), F32),
                           pltpu.SemaphoreType.DMA((2, 2))]
    return pl.pallas_call(
        functools.partial(_in_proj_kernel, first_pass=first_pass, side_seq=side_seq, side_steps=side_steps),
        grid=grid,
        in_specs=in_specs,
        out_specs=out_specs,
        out_shape=out_shape,
        scratch_shapes=scratch_shapes,
        compiler_params=_params("arbitrary", "arbitrary"),
        name="in_proj",
    )(*operands)


def _mem_kv_kernel(x_ref, wk_ref, wv_ref, k_ref, kb_ref, v_ref, vb_ref):
    xb = x_ref[...].astype(BF16)
    for w_ref, o_ref, ob_ref in ((wk_ref, k_ref, kb_ref), (wv_ref, v_ref, vb_ref)):
        acc = jnp.dot(xb, w_ref[...].astype(BF16), preferred_element_type=F32)
        o_ref[...] = acc
        ob_ref[...] = acc.astype(BF16)


def _mem_kv(x, wk, wv):
    m, k = x.shape
    n = wk.shape[1]
    tn = 512
    w_spec = pl.BlockSpec((k, tn), lambda j: (0, j))
    o_spec = pl.BlockSpec((m, tn), lambda j: (0, j))
    return pl.pallas_call(
        _mem_kv_kernel,
        grid=(n // tn,),
        in_specs=[pl.BlockSpec((m, k), lambda j: (0, 0), pipeline_mode=pl.Buffered(1)), w_spec, w_spec],
        out_specs=[o_spec] * 4,
        out_shape=[jax.ShapeDtypeStruct((m, n), F32), jax.ShapeDtypeStruct((m, n), BF16)] * 2,
        compiler_params=_params("arbitrary"),
        name="mem_kv",
    )(x, wk, wv)


def _gated_groupnorm(o, gn, g):
    return (g.astype(F32) * (_standardize(o) * gn)).astype(BF16)


def _ret_long_step(cd_ref, q_ref, k_ref, v_ref, g_ref, dmask_ref, qd_ref, kd_ref, gn_ref,
                   o_ref, sfin_ref, s_scr, *, n_chunks):
    t = pl.program_id(1)

    @pl.when(t == 0)
    def _():
        s_scr[...] = jnp.zeros_like(s_scr)

    state = [s_scr[h] for h in range(R_HEADS)]
    for c in range(n_chunks):
        rows = slice(c * R_CHUNK, (c + 1) * R_CHUNK)
        for h in range(R_HEADS):
            qk = slice(h * R_QK_DIM, (h + 1) * R_QK_DIM)
            vv = slice(h * R_V_DIM, (h + 1) * R_V_DIM)
            q = q_ref[rows, qk]
            k = k_ref[rows, qk]
            v = v_ref[rows, vv]
            s = state[h]
            sc = lax.dot_general(q, k, (((1,), (1,)), ((), ())), preferred_element_type=F32) * dmask_ref[h]
            lhs = jnp.concatenate([sc.astype(BF16), (q.astype(F32) * qd_ref[h]).astype(BF16)], axis=1)
            rhs = jnp.concatenate([v, s.astype(BF16)], axis=0)
            o = jnp.dot(lhs, rhs, preferred_element_type=F32)
            kd = (k.astype(F32) * kd_ref[h]).astype(BF16)
            state[h] = s * cd_ref[h] + lax.dot_general(kd, v, (((0,), (0,)), ((), ())),
                                                       preferred_element_type=F32)
            o_ref[rows, vv] = _gated_groupnorm(o, gn_ref[:, vv], g_ref[rows, vv])
    for h in range(R_HEADS):
        s_scr[h] = state[h]

    @pl.when(t == pl.num_programs(1) - 1)
    def _():
        sfin_ref[0] = s_scr[...]


def _ret_short_step(cd_ref, q_ref, k_ref, v_ref, g_ref, s_ref, dmask_ref, qd_ref, kd_ref, gn_ref,
                    o_ref, snew_ref, *, bb, seq):
    k_all = k_ref[...].astype(F32)
    v_all = v_ref[...].astype(F32)
    for h in range(R_HEADS):
        qk = slice(h * R_QK_DIM, (h + 1) * R_QK_DIM)
        vv = slice(h * R_V_DIM, (h + 1) * R_V_DIM)
        q = q_ref[:, qk]
        sc = lax.dot_general(q, k_ref[:, qk], (((1,), (1,)), ((), ())),
                             preferred_element_type=F32) * dmask_ref[h]
        intra = jnp.dot(sc.astype(BF16), v_ref[:, vv], preferred_element_type=F32)
        q_scaled = q.astype(F32) * qd_ref[h]
        k_scaled = k_all[:, qk] * kd_ref[h]
        inter = []
        for b in range(bb):
            rows = slice(b * seq, (b + 1) * seq)
            s = s_ref[b, h]
            inter.append(jnp.dot(q_scaled[rows].astype(BF16), s.astype(BF16), preferred_element_type=F32))
            snew_ref[b, h] = s * cd_ref[h] + lax.dot_general(
                k_scaled[rows].astype(BF16), v_all[rows, vv].astype(BF16), (((0,), (0,)), ((), ())),
                preferred_element_type=F32)
        o = intra + jnp.concatenate(inter, axis=0)
        o_ref[:, vv] = _gated_groupnorm(o, gn_ref[:, vv], g_ref[:, vv])


def _retention_kernel(cdl_ref, cds_ref, ql_ref, kl_ref, vl_ref, gl_ref, dml_ref, qdl_ref, kdl_ref,
                      qs_ref, ks_ref, vs_ref, gs_ref, s_ref, dms_ref, qds_ref, kds_ref, gn_ref,
                      ol_ref, sfin_ref, os_ref, snew_ref, s_scr, *, n_chunks, bb, seq):
    _ret_long_step(cdl_ref, ql_ref, kl_ref, vl_ref, gl_ref, dml_ref, qdl_ref, kdl_ref, gn_ref,
                   ol_ref, sfin_ref, s_scr, n_chunks=n_chunks)
    _ret_short_step(cds_ref, qs_ref, ks_ref, vs_ref, gs_ref, s_ref, dms_ref, qds_ref, kds_ref, gn_ref,
                    os_ref, snew_ref, bb=bb, seq=seq)


def _retention(h_long, h_short, state, gn, *, batch_l, seq_l, batch_s, seq_s):
    tr = 512
    n_t = seq_l // tr
    n_steps = batch_l * n_t
    bb, rem = divmod(batch_s, n_steps)
    assert rem == 0 and (bb * seq_s) % 16 == 0
    ts = bb * seq_s
    tabs_l = _retention_tables(min(seq_l, R_CHUNK))
    tabs_s = _retention_tables(min(seq_s, R_CHUNK))
    f32 = lambda t: jnp.asarray(t, F32)
    lane_bcast = lambda t, rows: f32(np.broadcast_to(t, (R_HEADS, rows, R_QK_DIM)))
    blk = np.arange(ts) // seq_s
    dmask_s = np.where((blk[:, None] == blk[None, :])[None], np.tile(tabs_s["dmask"], (1, bb, bb)), 0.0)
    qd_s = lane_bcast(np.tile(tabs_s["q_decay"], (1, bb, 1)), ts)
    kd_s = lane_bcast(np.tile(tabs_s["k_decay"], (1, bb, 1)), ts)

    step = lambda b, t: b * n_t + t
    const3 = lambda b, t: (0, 0, 0)
    smem = pl.BlockSpec(memory_space=pltpu.SMEM)
    state_spec = pl.BlockSpec((bb, R_HEADS, R_QK_DIM, R_V_DIM), lambda b, t: (step(b, t), 0, 0, 0))
    return pl.pallas_call(
        functools.partial(_retention_kernel, n_chunks=tr // R_CHUNK, bb=bb, seq=seq_s),
        grid=(batch_l, n_t),
        in_specs=[
            smem, smem,
            pl.BlockSpec((tr, R_QK_WIDTH), lambda b, t: (step(b, t), OFF_RQ // R_QK_WIDTH)),
            pl.BlockSpec((tr, R_QK_WIDTH), lambda b, t: (step(b, t), OFF_RK // R_QK_WIDTH)),
            pl.BlockSpec((tr, R_V_WIDTH), lambda b, t: (step(b, t), OFF_RV // R_V_WIDTH)),
            pl.BlockSpec((tr, R_V_WIDTH), lambda b, t: (step(b, t), OFF_RG // R_V_WIDTH)),
            pl.BlockSpec((R_HEADS, R_CHUNK, R_CHUNK), const3),
            pl.BlockSpec((R_HEADS, R_CHUNK, R_QK_DIM), const3),
            pl.BlockSpec((R_HEADS, R_CHUNK, R_QK_DIM), const3),
            pl.BlockSpec((ts, R_QK_WIDTH), lambda b, t: (step(b, t), OFF_RQ // R_QK_WIDTH)),
            pl.BlockSpec((ts, R_QK_WIDTH), lambda b, t: (step(b, t), OFF_RK // R_QK_WIDTH)),
            pl.BlockSpec((ts, R_V_WIDTH), lambda b, t: (step(b, t), OFF_RV // R_V_WIDTH)),
            pl.BlockSpec((ts, R_V_WIDTH), lambda b, t: (step(b, t), OFF_RG // R_V_WIDTH)),
            state_spec,
            pl.BlockSpec((R_HEADS, ts, ts), const3),
            pl.BlockSpec((R_HEADS, ts, R_QK_DIM), const3),
            pl.BlockSpec((R_HEADS, ts, R_QK_DIM), const3),
            pl.BlockSpec((1, R_V_WIDTH), lambda b, t: (0, 0)),
        ],
        out_specs=[
            pl.BlockSpec((tr, R_V_WIDTH), lambda b, t: (step(b, t), 0)),
            pl.BlockSpec((1, R_HEADS, R_QK_DIM, R_V_DIM), lambda b, t: (b, 0, 0, 0)),
            pl.BlockSpec((ts, R_V_WIDTH), lambda b, t: (step(b, t), 0)),
            state_spec,
        ],
        out_shape=[
            jax.ShapeDtypeStruct((batch_l * seq_l, R_V_WIDTH), BF16),
            jax.ShapeDtypeStruct((batch_l, R_HEADS, R_QK_DIM, R_V_DIM), F32),
            jax.ShapeDtypeStruct((batch_s * seq_s, R_V_WIDTH), BF16),
            jax.ShapeDtypeStruct((batch_s, R_HEADS, R_QK_DIM, R_V_DIM), F32),
        ],
        scratch_shapes=[pltpu.VMEM((R_HEADS, R_QK_DIM, R_V_DIM), F32)],
        compiler_params=_params("arbitrary", "arbitrary", vmem_limit_bytes=RETENTION_VMEM_LIMIT_BYTES),
        name="retention",
    )(f32(tabs_l["chunk_decay"]), f32(tabs_s["chunk_decay"]), h_long, h_long, h_long, h_long,
      f32(tabs_l["dmask"]), lane_bcast(tabs_l["q_decay"], R_CHUNK), lane_bcast(tabs_l["k_decay"], R_CHUNK),
      h_short, h_short, h_short, h_short, state, f32(dmask_s), qd_s, kd_s, gn)


def _retention_tables(c):
    log_g = np.log1p(-np.exp2(-5.0 - np.arange(R_HEADS, dtype=np.float64)))
    idx = np.arange(c, dtype=np.float64)
    diff = idx[:, None] - idx[None, :]
    causal = diff >= 0
    dmask = np.where(causal[None], np.exp(np.where(causal, diff, 0.0)[None] * log_g[:, None, None]), 0.0)
    q_decay = np.exp((idx[:, None] + 1.0) * log_g[None, :])
    k_decay = np.exp((c - 1.0 - idx)[:, None] * log_g[None, :])
    chunk_decay = np.exp(c * log_g)
    return {
        "dmask": dmask,
        "q_decay": q_decay.T[:, :, None],
        "k_decay": k_decay.T[:, :, None],
        "chunk_decay": chunk_decay,
    }


def _softmax_rows(s):
    e = jnp.exp(s - jnp.max(s, axis=-1, keepdims=True))
    return e, jnp.sum(e, axis=-1, keepdims=True)


def _mem_prompt_kernel(q_ref, mk_ref, mv_ref, o_ref):
    for h in range(M_HEADS):
        cols = slice(h * M_HEAD_DIM, (h + 1) * M_HEAD_DIM)
        s = lax.dot_general(q_ref[:, cols], mk_ref[:, cols], (((1,), (1,)), ((), ())),
                            preferred_element_type=F32)
        e, l = _softmax_rows(s)
        o = jnp.dot(e.astype(BF16), mv_ref[:, cols], preferred_element_type=F32) / l
        o_ref[:, cols] = o.astype(BF16)


def _mem_attn_prompt(h_act, mk, mv, *, batch, seq):
    tq = 512
    n_t = seq // tq
    return pl.pallas_call(
        _mem_prompt_kernel,
        grid=(batch, n_t),
        in_specs=[
            pl.BlockSpec((tq, M_WIDTH), lambda b, t: (b * n_t + t, OFF_MQ // M_WIDTH)),
            pl.BlockSpec((N_MEM, M_WIDTH), lambda b, t: (b, 0)),
            pl.BlockSpec((N_MEM, M_WIDTH), lambda b, t: (b, 0)),
        ],
        out_specs=pl.BlockSpec((tq, M_WIDTH), lambda b, t: (b * n_t + t, 0)),
        out_shape=jax.ShapeDtypeStruct((batch * seq, M_WIDTH), BF16),
        compiler_params=_params("arbitrary", "arbitrary"),
        name="mem_attn_prompt",
    )(h_act, mk, mv)


MEM_SEQS_PER_STEP = 4


def _mem_attend_step(step, n_steps, q_ref, mk_hbm, mv_hbm, o_ref, kbuf, vbuf, sem, *, seq):
    bb = MEM_SEQS_PER_STEP
    pairs = [(b, h) for b in range(bb) for h in range(M_HEADS)]

    def copies(st, slot):
        out = []
        for b, h in pairs:
            out.append(pltpu.make_async_copy(mk_hbm.at[st * bb + b, :, h, :], kbuf.at[slot, b, h],
                                             sem.at[0, slot]))
            out.append(pltpu.make_async_copy(mv_hbm.at[st * bb + b, :, h, :], vbuf.at[slot, b, h],
                                             sem.at[1, slot]))
        return out

    slot = step % 2

    @pl.when(step == 0)
    def _():
        for c in copies(0, 0):
            c.start()

    @pl.when(step + 1 < n_steps)
    def _():
        for c in copies(step + 1, 1 - slot):
            c.start()

    for c in copies(step, slot):
        c.wait()
    q_all = q_ref[...].astype(F32)
    scores = []
    for b, h in pairs:
        q = q_all[b * seq:(b + 1) * seq, h * M_HEAD_DIM:(h + 1) * M_HEAD_DIM].astype(BF16)
        scores.append(lax.dot_general(q, kbuf[slot, b, h].astype(BF16), (((1,), (1,)), ((), ())),
                                      preferred_element_type=F32))
    e, l = _softmax_rows(jnp.concatenate(scores, axis=0))
    inv_l = 1.0 / l
    outs = []
    for n, (b, h) in enumerate(pairs):
        rows = slice(n * seq, (n + 1) * seq)
        o = jnp.dot(e[rows].astype(BF16), vbuf[slot, b, h].astype(BF16), preferred_element_type=F32)
        outs.append(o * inv_l[rows])
    out_rows = [jnp.concatenate(outs[b * M_HEADS:(b + 1) * M_HEADS], axis=1) for b in range(bb)]
    o_ref[...] = jnp.concatenate(out_rows, axis=0).astype(BF16)


MERGE_ROW_CHUNK = 256


def _merge_kernel(ga_ref, gb_ref, gc_ref, u_ref, va_ref, sw_ref, sb_ref, b_ref, c_ref,
                  wa_ref, wb_ref, wc_ref, o_ref, *rest, tm, first_pass):
    if first_pass:
        wab_ref, wbb_ref, wcb_ref, a_scr = rest
        wab_ref[...] = wa_ref[...].astype(BF16)
        wbb_ref[...] = wb_ref[...].astype(BF16)
        wcb_ref[...] = wc_ref[...].astype(BF16)
        wa_ref, wb_ref, wc_ref = wab_ref, wbb_ref, wcb_ref
    else:
        (a_scr,) = rest

    @pl.when(pl.program_id(1) == 0)
    def _():
        for r in range(tm // CHUNK):
            rows = slice(r * CHUNK, (r + 1) * CHUNK)
            for g in range(A_GROUPS):
                cols = slice(g * A_GROUP_DIM, (g + 1) * A_GROUP_DIM)
                z = jnp.dot(sw_ref[g], va_ref[rows, cols], preferred_element_type=F32) + sb_ref[g]
                a_scr[rows, cols] = (u_ref[rows, cols].astype(F32) * z).astype(BF16)

    for r in range(tm // MERGE_ROW_CHUNK):
        rows = slice(r * MERGE_ROW_CHUNK, (r + 1) * MERGE_ROW_CHUNK)
        ya = jnp.dot(a_scr[rows, :], wa_ref[...], preferred_element_type=F32)
        yb = jnp.dot(b_ref[rows, :], wb_ref[...], preferred_element_type=F32)
        yc = jnp.dot(c_ref[rows, :], wc_ref[...], preferred_element_type=F32)
        merged = (ga_ref[rows, :].astype(F32) * ya + gb_ref[rows, :].astype(F32) * yb
                  + gc_ref[rows, :].astype(F32) * yc)
        o_ref[rows, :] = merged.astype(BF16)


def _merge(h_act, b_act, c_act, sgu_w, sgu_b, wa, wb, wc, *, first_pass):
    m = h_act.shape[0]
    tm, tn = 1024, (512 if first_pass else 1024)
    n_j = D_MODEL // tn
    row_tile_mode = pl.Buffered(1) if first_pass else None
    w_specs = [pl.BlockSpec((A_WIDTH, tn), lambda i, j: (0, j)),
               pl.BlockSpec((R_V_WIDTH, tn), lambda i, j: (0, j)),
               pl.BlockSpec((M_WIDTH, tn), lambda i, j: (0, j))]
    out_specs = [pl.BlockSpec((tm, tn), lambda i, j: (i, j))]
    out_shape = [jax.ShapeDtypeStruct((m, D_MODEL), BF16)]
    if first_pass:
        assert m == tm
        out_specs += w_specs
        out_shape += [jax.ShapeDtypeStruct(w.shape, BF16) for w in (wa, wb, wc)]
    return pl.pallas_call(
        functools.partial(_merge_kernel, tm=tm, first_pass=first_pass),
        grid=(m // tm, n_j),
        in_specs=[
            pl.BlockSpec((tm, tn), lambda i, j: (i, j)),
            pl.BlockSpec((tm, tn), lambda i, j: (i, n_j + j)),
            pl.BlockSpec((tm, tn), lambda i, j: (i, 2 * n_j + j)),
            pl.BlockSpec((tm, A_WIDTH), lambda i, j: (i, OFF_AU // A_WIDTH), pipeline_mode=row_tile_mode),
            pl.BlockSpec((tm, A_WIDTH), lambda i, j: (i, OFF_AV // A_WIDTH), pipeline_mode=row_tile_mode),
            pl.BlockSpec((A_GROUPS, CHUNK, CHUNK), lambda i, j: (0, 0, 0)),
            pl.BlockSpec((A_GROUPS, CHUNK, 1), lambda i, j: (0, 0, 0)),
            pl.BlockSpec((tm, R_V_WIDTH), lambda i, j: (i, 0), pipeline_mode=row_tile_mode),
            pl.BlockSpec((tm, M_WIDTH), lambda i, j: (i, 0), pipeline_mode=row_tile_mode),
        ] + w_specs,
        out_specs=out_specs,
        out_shape=out_shape,
        scratch_shapes=[pltpu.VMEM((tm, A_WIDTH), BF16)],
        compiler_params=_params("arbitrary", "arbitrary", vmem_limit_bytes=MERGE_VMEM_LIMIT_BYTES),
        name="merge",
    )(h_act, h_act, h_act, h_act, h_act, sgu_w, sgu_b, b_act, c_act, wa, wb, wc)


OUT_ROW_CHUNK = 256


def _out_ln_kernel(m_ref, w_ref, x_ref, g_ref, b_ref, o_ref, wb_ref):
    wb_ref[...] = w_ref[...].astype(BF16)
    for r in range(m_ref.shape[0] // OUT_ROW_CHUNK):
        rows = slice(r * OUT_ROW_CHUNK, (r + 1) * OUT_ROW_CHUNK)
        y = jnp.dot(m_ref[rows, :], wb_ref[...], preferred_element_type=F32)
        o_ref[rows, :] = _standardize(ALPHA * x_ref[rows, :] + y) * g_ref[...] + b_ref[...]


def _out_ln(merged, w_out, x, ln_g, ln_b):
    tm = x.shape[0]
    full = lambda i: (0, 0)
    return pl.pallas_call(
        _out_ln_kernel,
        grid=(1,),
        in_specs=[
            pl.BlockSpec((tm, D_MODEL), full),
            pl.BlockSpec((D_MODEL, D_MODEL), full),
            pl.BlockSpec((tm, D_MODEL), full),
            pl.BlockSpec((1, D_MODEL), full),
            pl.BlockSpec((1, D_MODEL), full),
        ],
        out_specs=[pl.BlockSpec((tm, D_MODEL), full), pl.BlockSpec((D_MODEL, D_MODEL), full)],
        out_shape=[jax.ShapeDtypeStruct((tm, D_MODEL), F32), jax.ShapeDtypeStruct(w_out.shape, BF16)],
        compiler_params=_params("arbitrary"),
        name="out_ln",
    )(merged, w_out, x, ln_g, ln_b)


def _ffn_kernel(x_ref, wg_ref, wu_ref, wd_ref, g_ref, b_ref, o_ref, wgb_ref, wub_ref, wdb_ref,
                xb_ref, acc_ref):
    wgb_ref[...] = wg_ref[...].astype(BF16)
    wub_ref[...] = wu_ref[...].astype(BF16)
    wdb_ref[...] = wd_ref[...].astype(BF16)
    f = pl.program_id(1)

    @pl.when(f == 0)
    def _():
        xb_ref[...] = x_ref[...].astype(BF16)
        acc_ref[...] = jnp.zeros_like(acc_ref)

    xb = xb_ref[...]
    gate = jnp.dot(xb, wgb_ref[...], preferred_element_type=F32)
    up = jnp.dot(xb, wub_ref[...], preferred_element_type=F32)
    act = (jax.nn.silu(gate) * up).astype(BF16)
    acc_ref[...] += jnp.dot(act, wdb_ref[...], preferred_element_type=F32)

    @pl.when(f == pl.num_programs(1) - 1)
    def _():
        o_ref[...] = _standardize(ALPHA * x_ref[...] + acc_ref[...]) * g_ref[...] + b_ref[...]


def _ffn(x1, wg, wu, wd, ln_g, ln_b):
    tm = x1.shape[0]
    tf = 256
    one_buffer = pl.Buffered(1)
    out_shape = [jax.ShapeDtypeStruct((tm, D_MODEL), F32)] + [jax.ShapeDtypeStruct(w.shape, BF16)
                                                             for w in (wg, wu, wd)]
    out_specs = [pl.BlockSpec((tm, D_MODEL), lambda i, f: (i, 0), pipeline_mode=one_buffer),
                 pl.BlockSpec((D_MODEL, tf), lambda i, f: (0, f)),
                 pl.BlockSpec((D_MODEL, tf), lambda i, f: (0, f)),
                 pl.BlockSpec((tf, D_MODEL), lambda i, f: (f, 0))]
    return pl.pallas_call(
        _ffn_kernel,
        grid=(1, D_FF // tf),
        in_specs=[
            pl.BlockSpec((tm, D_MODEL), lambda i, f: (i, 0), pipeline_mode=one_buffer),
            pl.BlockSpec((D_MODEL, tf), lambda i, f: (0, f)),
            pl.BlockSpec((D_MODEL, tf), lambda i, f: (0, f)),
            pl.BlockSpec((tf, D_MODEL), lambda i, f: (f, 0)),
            pl.BlockSpec((1, D_MODEL), lambda i, f: (0, 0)),
            pl.BlockSpec((1, D_MODEL), lambda i, f: (0, 0)),
        ],
        out_specs=out_specs,
        out_shape=out_shape,
        scratch_shapes=[pltpu.VMEM((tm, D_MODEL), BF16), pltpu.VMEM((tm, D_MODEL), F32)],
        compiler_params=_params("arbitrary", "arbitrary"),
        name="ffn",
    )(x1, wg, wu, wd, ln_g, ln_b)


TAIL_ROW_CHUNK = 256


def _tail_kernel(m_ref, wo_ref, x_ref, g1_ref, b1_ref, wg_ref, wu_ref, wd_ref, g2_ref, b2_ref,
                 o_ref, x1_scr, x1b_scr):
    f = pl.program_id(1)
    n_f = pl.num_programs(1)
    chunks = [slice(r * TAIL_ROW_CHUNK, (r + 1) * TAIL_ROW_CHUNK)
              for r in range(m_ref.shape[0] // TAIL_ROW_CHUNK)]

    def swiglu(rows):
        xb = x1b_scr[rows, :]
        gate = jnp.dot(xb, wg_ref[...], preferred_element_type=F32)
        up = jnp.dot(xb, wu_ref[...], preferred_element_type=F32)
        act = (jax.nn.silu(gate) * up).astype(BF16)
        return jnp.dot(act, wd_ref[...], preferred_element_type=F32)

    @pl.when(f == 0)
    def _():
        for rows in chunks:
            y = jnp.dot(m_ref[rows, :], wo_ref[...], preferred_element_type=F32)
            x1 = _standardize(ALPHA * x_ref[rows, :] + y) * g1_ref[...] + b1_ref[...]
            x1_scr[rows, :] = x1
            x1b_scr[rows, :] = x1.astype(BF16)
        for rows in chunks:
            o_ref[rows, :] = swiglu(rows)

    @pl.when((f > 0) & (f < n_f - 1))
    def _():
        o_ref[...] += swiglu(slice(None))

    @pl.when(f == n_f - 1)
    def _():
        for rows in chunks:
            ff = o_ref[rows, :] + swiglu(rows)
            o_ref[rows, :] = _standardize(ALPHA * x1_scr[rows, :] + ff) * g2_ref[...] + b2_ref[...]


def _tail(merged, w_out, x, ln1_g, ln1_b, wg, wu, wd, ln2_g, ln2_b):
    m = x.shape[0]
    tm, tf = 512, 512
    assert D_FF // tf >= 2
    row = lambda i, f: (i, 0)
    const = lambda i, f: (0, 0)
    return pl.pallas_call(
        _tail_kernel,
        grid=(m // tm, D_FF // tf),
        in_specs=[
            pl.BlockSpec((tm, D_MODEL), row),
            pl.BlockSpec((D_MODEL, D_MODEL), const, pipeline_mode=pl.Buffered(1)),
            pl.BlockSpec((tm, D_MODEL), row),
            pl.BlockSpec((1, D_MODEL), const),
            pl.BlockSpec((1, D_MODEL), const),
            pl.BlockSpec((D_MODEL, tf), lambda i, f: (0, f)),
            pl.BlockSpec((D_MODEL, tf), lambda i, f: (0, f)),
            pl.BlockSpec((tf, D_MODEL), lambda i, f: (f, 0)),
            pl.BlockSpec((1, D_MODEL), const),
            pl.BlockSpec((1, D_MODEL), const),
        ],
        out_specs=pl.BlockSpec((tm, D_MODEL), row),
        out_shape=jax.ShapeDtypeStruct((m, D_MODEL), F32),
        scratch_shapes=[pltpu.VMEM((tm, D_MODEL), F32), pltpu.VMEM((tm, D_MODEL), BF16)],
        compiler_params=_params("arbitrary", "arbitrary"),
        name="tail",
    )(merged, w_out, x, ln1_g, ln1_b, wg, wu, wd, ln2_g, ln2_b)


def _rotary_tables(pos_start, n_pos, reps):
    half = R_QK_DIM // 2
    inv = ROPE_BASE ** (-np.arange(half, dtype=np.float64) / half)
    ang = (pos_start + np.arange(n_pos, dtype=np.float64))[:, None] * inv[None, :]
    cos = np.cos(ang)
    sin = np.sin(ang)
    cos2 = np.concatenate([cos, cos], axis=-1)
    sin2 = np.concatenate([-sin, sin], axis=-1)
    k_scale = R_QK_DIM ** -0.5
    table = lambda t: jnp.asarray(np.tile(np.stack([t, t * k_scale]), (1, reps, 1)), F32)
    return table(cos2), table(sin2)


def _sgu_tables(sgu_w, sgu_b, c):
    w = jnp.tril(sgu_w[:, :c, :c])
    b = sgu_b[:, :c]
    reps = CHUNK // c
    if reps > 1:
        blk = jnp.arange(CHUNK) // c
        w = jnp.where((blk[:, None] == blk[None, :])[None], jnp.tile(w, (1, reps, reps)), 0.0)
        b = jnp.tile(b, (1, reps))
    return w.astype(BF16), b[:, :, None]


def kernel(x_prompt, x_sample, mem_prompt, state_ret, cache_mem_k, cache_mem_v, w_in, sgu_ln_g, sgu_ln_b, sgu_w, sgu_b, w_proj_a, ret_gn_g, w_proj_b, w_mem_k, w_mem_v, w_proj_c, w_out, ln1_g, ln1_b, w_ffn_gate, w_ffn_up, w_ffn_down, ln2_g, ln2_b):
    assert w_in.shape[0] == DEPTH == 1
    bp, seq, _ = x_prompt.shape
    bs, dseq, _ = x_sample.shape
    l = 0
    layer = lambda a: a.reshape(a.shape[1:])
    ln_g = sgu_ln_g[l].reshape(1, A_WIDTH)
    ln_b = sgu_ln_b[l].reshape(1, A_WIDTH)
    gn = ret_gn_g[l].reshape(1, R_V_WIDTH)
    ln1g, ln1b = ln1_g[l].reshape(1, D_MODEL), ln1_b[l].reshape(1, D_MODEL)
    ln2g, ln2b = ln2_g[l].reshape(1, D_MODEL), ln2_b[l].reshape(1, D_MODEL)

    m_s = bs * dseq
    xs = x_sample.reshape(m_s, D_MODEL)
    cos_s, sin_s = _rotary_tables(float(PAST_LEN), dseq, reps=bs)
    h_s, va_s, w_in_b = _in_proj(xs, layer(w_in), cos_s, sin_s, ln_g, ln_b, tm=m_s)

    xp = x_prompt.reshape(bp * seq, D_MODEL)
    cos_p, sin_p = _rotary_tables(0.0, seq, reps=1)
    h_p, c_s = _in_proj(xp, w_in_b, cos_p, sin_p, ln_g, ln_b, tm=1024,
                        side=(h_s, layer(cache_mem_k), layer(cache_mem_v), dseq))
    b_p, s_p, b_s, s_s = _retention(h_p, h_s, layer(state_ret), gn,
                                    batch_l=bp, seq_l=seq, batch_s=bs, seq_s=dseq)

    m_s_merged, wa_b, wb_b, wc_b = _merge(h_s, b_s, c_s, *_sgu_tables(sgu_w[l], sgu_b[l], min(dseq, CHUNK)),
                                          layer(w_proj_a), layer(w_proj_b), layer(w_proj_c), first_pass=True)
    x1_s, wo_b = _out_ln(m_s_merged, layer(w_out), xs, ln1g, ln1b)
    y_s, wg_b, wu_b, wd_b = _ffn(x1_s, layer(w_ffn_gate), layer(w_ffn_up), layer(w_ffn_down), ln2g, ln2b)

    memp = mem_prompt.reshape(bp * N_MEM, D_MODEL)
    mk_p, mk_pb, mv_p, mv_pb = _mem_kv(memp, layer(w_mem_k), layer(w_mem_v))
    c_p = _mem_attn_prompt(h_p, mk_pb, mv_pb, batch=bp, seq=seq)
    (m_p_merged,) = _merge(h_p, b_p, c_p, *_sgu_tables(sgu_w[l], sgu_b[l], min(seq, CHUNK)),
                           wa_b, wb_b, wc_b, first_pass=False)
    y_p = _tail(m_p_merged, wo_b, xp, ln1g, ln1b, wg_b, wu_b, wd_b, ln2g, ln2b)

    return (
        y_p.reshape(bp, seq, D_MODEL),
        y_s.reshape(bs, dseq, D_MODEL),
        s_p[None],
        mk_p.reshape(1, bp, N_MEM, M_HEADS, M_HEAD_DIM),
        mv_p.reshape(1, bp, N_MEM, M_HEADS, M_HEAD_DIM),
        s_s[None],
        va_s.reshape(1, bs, dseq, A_GROUPS, A_GROUP_DIM),
    )
```

```python
import functools

import jax
import jax.numpy as jnp
import numpy as np
from jax import lax
from jax.experimental import pallas as pl
from jax.experimental.pallas import tpu as pltpu

F32 = jnp.float32
BF16 = jnp.bfloat16

D_MODEL = 2048
DEPTH = 1
PAST_LEN = 16384
CHUNK = 128
A_GROUPS = 4
A_GROUP_DIM = 256
A_WIDTH = A_GROUPS * A_GROUP_DIM
R_HEADS = 8
R_QK_DIM = 128
R_V_DIM = 256
R_QK_WIDTH = R_HEADS * R_QK_DIM
R_V_WIDTH = R_HEADS * R_V_DIM
R_CHUNK = 128
N_MEM = 256
M_HEADS = 4
M_HEAD_DIM = 256
M_WIDTH = M_HEADS * M_HEAD_DIM
N_BRANCH = 3
D_FF = -(-8 * D_MODEL // (3 * 256)) * 256
ALPHA = (2.0 * DEPTH) ** 0.25
ROPE_BASE = 10000.0
LN_EPS = 1e-5

OFF_AU = N_BRANCH * D_MODEL
OFF_AV = OFF_AU + A_WIDTH
OFF_RQ = OFF_AV + A_WIDTH
OFF_RK = OFF_RQ + R_QK_WIDTH
OFF_RV = OFF_RK + R_QK_WIDTH
OFF_RG = OFF_RV + R_V_WIDTH
OFF_MQ = OFF_RG + R_V_WIDTH
IN_WIDTH = OFF_MQ + M_WIDTH

VMEM_LIMIT_BYTES = 56 * 1024 * 1024
RETENTION_VMEM_LIMIT_BYTES = 60 * 1024 * 1024
MERGE_VMEM_LIMIT_BYTES = 62 * 1024 * 1024


def _params(*sem, vmem_limit_bytes=VMEM_LIMIT_BYTES):
    return pltpu.CompilerParams(dimension_semantics=sem, vmem_limit_bytes=vmem_limit_bytes)


def _standardize(x):
    mu = jnp.mean(x, axis=-1, keepdims=True)
    xc = x - mu
    var = jnp.mean(xc * xc, axis=-1, keepdims=True)
    return xc * lax.rsqrt(var + LN_EPS)


IN_TN = 1024
IN_ROW_CHUNK = 256
_J_AU = OFF_AU // IN_TN
_J_AV = OFF_AV // IN_TN
_J_RQ = OFF_RQ // IN_TN
_J_RK = OFF_RK // IN_TN
_J_RV = OFF_RV // IN_TN
_J_RG = OFF_RG // IN_TN
_J_MQ = OFF_MQ // IN_TN
_J_END = IN_WIDTH // IN_TN


def _in_proj_kernel(x_ref, w_ref, cos_ref, sin_ref, lng_ref, lnb_ref, *rest, first_pass, side_seq,
                    side_steps):
    tm = x_ref.shape[0]
    j = pl.program_id(1)
    if first_pass:
        o_ref, va_ref, wb_ref, xb_ref = rest
        wb_ref[...] = w_ref[...].astype(BF16)
    else:
        qs_ref, mk_hbm, mv_hbm, o_ref, cs_ref, xb_ref, kbuf, vbuf, sem = rest
        wb_ref = w_ref

    @pl.when(j == 0)
    def _():
        xb_ref[...] = x_ref[...].astype(BF16)

    def column_range(pred, epilogue, with_side_job=False):
        @pl.when(pred)
        def _():
            if with_side_job:
                _mem_attend_step(pl.program_id(0) * side_steps + j, pl.num_programs(0) * side_steps,
                                 qs_ref, mk_hbm, mv_hbm, cs_ref, kbuf, vbuf, sem, seq=side_seq)
            for r in range(tm // IN_ROW_CHUNK):
                rows = slice(r * IN_ROW_CHUNK, (r + 1) * IN_ROW_CHUNK)
                acc = jnp.dot(xb_ref[rows, :], wb_ref[...], preferred_element_type=F32)
                epilogue(rows, acc)

    def store(fn):
        def epilogue(rows, acc):
            o_ref[rows, :] = fn(acc).astype(BF16)
        return epilogue

    def gelu_layernorm(rows, acc):
        v = jax.nn.gelu(acc)
        for g in range(IN_TN // A_GROUP_DIM):
            cols = slice(g * A_GROUP_DIM, (g + 1) * A_GROUP_DIM)
            vn = _standardize(v[:, cols]) * lng_ref[:, cols] + lnb_ref[:, cols]
            o_ref[rows, cols] = vn.astype(BF16)
            if first_pass:
                va_ref[rows, cols] = vn

    def rotary(rows, acc):
        for hh in range(IN_TN // R_QK_DIM):
            cols = slice(hh * R_QK_DIM, (hh + 1) * R_QK_DIM)
            xh = acc[:, cols]
            r = xh * cos_ref[rows, :] + pltpu.roll(xh, R_QK_DIM // 2, 1) * sin_ref[rows, :]
            o_ref[rows, cols] = r.astype(BF16)

    is_mq = j >= _J_MQ
    raw_scale = jnp.where(is_mq, M_HEAD_DIM ** -0.5, 1.0).astype(F32)
    if first_pass:
        column_range(j < _J_AU, store(jax.nn.sigmoid))
    else:
        column_range(j < side_steps, store(jax.nn.sigmoid), with_side_job=True)
        column_range((j >= side_steps) & (j < _J_AU), store(jax.nn.sigmoid))
    column_range((j >= _J_AU) & (j < _J_AV), store(jax.nn.gelu))
    column_range((j >= _J_AV) & (j < _J_RQ), gelu_layernorm)
    column_range((j >= _J_RQ) & (j < _J_RV), rotary)
    column_range(((j >= _J_RV) & (j < _J_RG)) | is_mq, store(lambda acc: acc * raw_scale))
    column_range((j >= _J_RG) & (j < _J_MQ), store(jax.nn.silu))


def _in_proj(x, w_in, cos_t, sin_t, ln_g, ln_b, *, tm, side=None):
    m = x.shape[0]
    first_pass = side is None
    n_pos_blocks = cos_t.shape[1] // tm
    assert A_WIDTH == IN_TN and R_QK_WIDTH == IN_TN
    table_idx = lambda i, j: (jnp.where(j == _J_RK, 1, 0), i % n_pos_blocks, 0)
    grid = (m // tm, _J_END)

    operands = [x, w_in, cos_t, sin_t, ln_g, ln_b]
    in_specs = [
        pl.BlockSpec((tm, D_MODEL), lambda i, j: (i, 0),
                     pipeline_mode=pl.Buffered(1) if first_pass else None),
        pl.BlockSpec((D_MODEL, IN_TN), lambda i, j: (0, j)),
        pl.BlockSpec((None, tm, R_QK_DIM), table_idx),
        pl.BlockSpec((None, tm, R_QK_DIM), table_idx),
        pl.BlockSpec((1, A_WIDTH), lambda i, j: (0, 0)),
        pl.BlockSpec((1, A_WIDTH), lambda i, j: (0, 0)),
    ]
    out_shape = [jax.ShapeDtypeStruct((m, IN_WIDTH), BF16)]
    out_specs = [pl.BlockSpec((tm, IN_TN), lambda i, j: (i, j))]
    scratch_shapes = [pltpu.VMEM((tm, D_MODEL), BF16)]
    side_seq = side_steps = None
    if first_pass:
        assert m == tm
        out_shape += [jax.ShapeDtypeStruct((m, A_WIDTH), F32), jax.ShapeDtypeStruct(w_in.shape, BF16)]
        out_specs += [pl.BlockSpec((tm, A_WIDTH), lambda i, j: (i, 0)),
                      pl.BlockSpec((D_MODEL, IN_TN), lambda i, j: (0, j))]
    else:
        h_side, mem_k, mem_v, side_seq = side
        side_rows = MEM_SEQS_PER_STEP * side_seq
        side_steps, rem = divmod(mem_k.shape[0], MEM_SEQS_PER_STEP * grid[0])
        assert rem == 0 and 0 < side_steps <= _J_AU
        side_block = lambda i, j: i * side_steps + jnp.minimum(j, side_steps - 1)
        operands += [h_side, mem_k, mem_v]
        in_specs += [pl.BlockSpec((side_rows, M_WIDTH), lambda i, j: (side_block(i, j), OFF_MQ // M_WIDTH)),
                     pl.BlockSpec(memory_space=pl.ANY),
                     pl.BlockSpec(memory_space=pl.ANY)]
        out_shape.append(jax.ShapeDtypeStruct((h_side.shape[0], M_WIDTH), BF16))
        out_specs.append(pl.BlockSpec((side_rows, M_WIDTH), lambda i, j: (side_block(i, j), 0)))
        scratch_shapes += [pltpu.VMEM((2, MEM_SEQS_PER_STEP, M_HEADS, N_MEM, M_HEAD_DIM), F32),
                           pltpu.VMEM((2, MEM_SEQS_PER_STEP, M_HEADS, N_MEM, M_HEAD_DIM), F32),
                           pltpu.SemaphoreType.DMA((2, 2))]
    return pl.pallas_call(
        functools.partial(_in_proj_kernel, first_pass=first_pass, side_seq=side_seq, side_steps=side_steps),
        grid=grid,
        in_specs=in_specs,
        out_specs=out_specs,
        out_shape=out_shape,
        scratch_shapes=scratch_shapes,
        compiler_params=_params("arbitrary", "arbitrary"),
        name="in_proj",
    )(*operands)


def _mem_kv_kernel(x_ref, wk_ref, wv_ref, k_ref, kb_ref, v_ref, vb_ref):
    xb = x_ref[...].astype(BF16)
    for w_ref, o_ref, ob_ref in ((wk_ref, k_ref, kb_ref), (wv_ref, v_ref, vb_ref)):
        acc = jnp.dot(xb, w_ref[...].astype(BF16), preferred_element_type=F32)
        o_ref[...] = acc
        ob_ref[...] = acc.astype(BF16)


def _mem_kv(x, wk, wv):
    m, k = x.shape
    n = wk.shape[1]
    tn = 512
    w_spec = pl.BlockSpec((k, tn), lambda j: (0, j))
    o_spec = pl.BlockSpec((m, tn), lambda j: (0, j))
    return pl.pallas_call(
        _mem_kv_kernel,
        grid=(n // tn,),
        in_specs=[pl.BlockSpec((m, k), lambda j: (0, 0), pipeline_mode=pl.Buffered(1)), w_spec, w_spec],
        out_specs=[o_spec] * 4,
        out_shape=[jax.ShapeDtypeStruct((m, n), F32), jax.ShapeDtypeStruct((m, n), BF16)] * 2,
        compiler_params=_params("arbitrary"),
        name="mem_kv",
    )(x, wk, wv)


def _gated_groupnorm(o, gn, g):
    return (g.astype(F32) * (_standardize(o) * gn)).astype(BF16)


def _ret_long_step(cd_ref, q_ref, k_ref, v_ref, g_ref, dmask_ref, qd_ref, kd_ref, gn_ref,
                   o_ref, sfin_ref, s_scr, *, n_chunks):
    t = pl.program_id(1)

    @pl.when(t == 0)
    def _():
        s_scr[...] = jnp.zeros_like(s_scr)

    state = [s_scr[h] for h in range(R_HEADS)]
    for c in range(n_chunks):
        rows = slice(c * R_CHUNK, (c + 1) * R_CHUNK)
        for h in range(R_HEADS):
            qk = slice(h * R_QK_DIM, (h + 1) * R_QK_DIM)
            vv = slice(h * R_V_DIM, (h + 1) * R_V_DIM)
            q = q_ref[rows, qk]
            k = k_ref[rows, qk]
            v = v_ref[rows, vv]
            s = state[h]
            sc = lax.dot_general(q, k, (((1,), (1,)), ((), ())), preferred_element_type=F32) * dmask_ref[h]
            lhs = jnp.concatenate([sc.astype(BF16), (q.astype(F32) * qd_ref[h]).astype(BF16)], axis=1)
            rhs = jnp.concatenate([v, s.astype(BF16)], axis=0)
            o = jnp.dot(lhs, rhs, preferred_element_type=F32)
            kd = (k.astype(F32) * kd_ref[h]).astype(BF16)
            state[h] = s * cd_ref[h] + lax.dot_general(kd, v, (((0,), (0,)), ((), ())),
                                                       preferred_element_type=F32)
            o_ref[rows, vv] = _gated_groupnorm(o, gn_ref[:, vv], g_ref[rows, vv])
    for h in range(R_HEADS):
        s_scr[h] = state[h]

    @pl.when(t == pl.num_programs(1) - 1)
    def _():
        sfin_ref[0] = s_scr[...]


def _ret_short_step(cd_ref, q_ref, k_ref, v_ref, g_ref, s_ref, dmask_ref, qd_ref, kd_ref, gn_ref,
                    o_ref, snew_ref, *, bb, seq):
    k_all = k_ref[...].astype(F32)
    v_all = v_ref[...].astype(F32)
    for h in range(R_HEADS):
        qk = slice(h * R_QK_DIM, (h + 1) * R_QK_DIM)
        vv = slice(h * R_V_DIM, (h + 1) * R_V_DIM)
        q = q_ref[:, qk]
        sc = lax.dot_general(q, k_ref[:, qk], (((1,), (1,)), ((), ())),
                             preferred_element_type=F32) * dmask_ref[h]
        intra = jnp.dot(sc.astype(BF16), v_ref[:, vv], preferred_element_type=F32)
        q_scaled = q.astype(F32) * qd_ref[h]
        k_scaled = k_all[:, qk] * kd_ref[h]
        inter = []
        for b in range(bb):
            rows = slice(b * seq, (b + 1) * seq)
            s = s_ref[b, h]
            inter.append(jnp.dot(q_scaled[rows].astype(BF16), s.astype(BF16), preferred_element_type=F32))
            snew_ref[b, h] = s * cd_ref[h] + lax.dot_general(
                k_scaled[rows].astype(BF16), v_all[rows, vv].astype(BF16), (((0,), (0,)), ((), ())),
                preferred_element_type=F32)
        o = intra + jnp.concatenate(inter, axis=0)
        o_ref[:, vv] = _gated_groupnorm(o, gn_ref[:, vv], g_ref[:, vv])


def _retention_kernel(cdl_ref, cds_ref, ql_ref, kl_ref, vl_ref, gl_ref, dml_ref, qdl_ref, kdl_ref,
                      qs_ref, ks_ref, vs_ref, gs_ref, s_ref, dms_ref, qds_ref, kds_ref, gn_ref,
                      ol_ref, sfin_ref, os_ref, snew_ref, s_scr, *, n_chunks, bb, seq):
    _ret_long_step(cdl_ref, ql_ref, kl_ref, vl_ref, gl_ref, dml_ref, qdl_ref, kdl_ref, gn_ref,
                   ol_ref, sfin_ref, s_scr, n_chunks=n_chunks)
    _ret_short_step(cds_ref, qs_ref, ks_ref, vs_ref, gs_ref, s_ref, dms_ref, qds_ref, kds_ref, gn_ref,
                    os_ref, snew_ref, bb=bb, seq=seq)


def _retention(h_long, h_short, state, gn, *, batch_l, seq_l, batch_s, seq_s):
    tr = 512
    n_t = seq_l // tr
    n_steps = batch_l * n_t
    bb, rem = divmod(batch_s, n_steps)
    assert rem == 0 and (bb * seq_s) % 16 == 0
    ts = bb * seq_s
    tabs_l = _retention_tables(min(seq_l, R_CHUNK))
    tabs_s = _retention_tables(min(seq_s, R_CHUNK))
    f32 = lambda t: jnp.asarray(t, F32)
    lane_bcast = lambda t, rows: f32(np.broadcast_to(t, (R_HEADS, rows, R_QK_DIM)))
    blk = np.arange(ts) // seq_s
    dmask_s = np.where((blk[:, None] == blk[None, :])[None], np.tile(tabs_s["dmask"], (1, bb, bb)), 0.0)
    qd_s = lane_bcast(np.tile(tabs_s["q_decay"], (1, bb, 1)), ts)
    kd_s = lane_bcast(np.tile(tabs_s["k_decay"], (1, bb, 1)), ts)

    step = lambda b, t: b * n_t + t
    const3 = lambda b, t: (0, 0, 0)
    smem = pl.BlockSpec(memory_space=pltpu.SMEM)
    state_spec = pl.BlockSpec((bb, R_HEADS, R_QK_DIM, R_V_DIM), lambda b, t: (step(b, t), 0, 0, 0))
    return pl.pallas_call(
        functools.partial(_retention_kernel, n_chunks=tr // R_CHUNK, bb=bb, seq=seq_s),
        grid=(batch_l, n_t),
        in_specs=[
            smem, smem,
            pl.BlockSpec((tr, R_QK_WIDTH), lambda b, t: (step(b, t), OFF_RQ // R_QK_WIDTH)),
            pl.BlockSpec((tr, R_QK_WIDTH), lambda b, t: (step(b, t), OFF_RK // R_QK_WIDTH)),
            pl.BlockSpec((tr, R_V_WIDTH), lambda b, t: (step(b, t), OFF_RV // R_V_WIDTH)),
            pl.BlockSpec((tr, R_V_WIDTH), lambda b, t: (step(b, t), OFF_RG // R_V_WIDTH)),
            pl.BlockSpec((R_HEADS, R_CHUNK, R_CHUNK), const3),
            pl.BlockSpec((R_HEADS, R_CHUNK, R_QK_DIM), const3),
            pl.BlockSpec((R_HEADS, R_CHUNK, R_QK_DIM), const3),
            pl.BlockSpec((ts, R_QK_WIDTH), lambda b, t: (step(b, t), OFF_RQ // R_QK_WIDTH)),
            pl.BlockSpec((ts, R_QK_WIDTH), lambda b, t: (step(b, t), OFF_RK // R_QK_WIDTH)),
            pl.BlockSpec((ts, R_V_WIDTH), lambda b, t: (step(b, t), OFF_RV // R_V_WIDTH)),
            pl.BlockSpec((ts, R_V_WIDTH), lambda b, t: (step(b, t), OFF_RG // R_V_WIDTH)),
            state_spec,
            pl.BlockSpec((R_HEADS, ts, ts), const3),
            pl.BlockSpec((R_HEADS, ts, R_QK_DIM), const3),
            pl.BlockSpec((R_HEADS, ts, R_QK_DIM), const3),
            pl.BlockSpec((1, R_V_WIDTH), lambda b, t: (0, 0)),
        ],
        out_specs=[
            pl.BlockSpec((tr, R_V_WIDTH), lambda b, t: (step(b, t), 0)),
            pl.BlockSpec((1, R_HEADS, R_QK_DIM, R_V_DIM), lambda b, t: (b, 0, 0, 0)),
            pl.BlockSpec((ts, R_V_WIDTH), lambda b, t: (step(b, t), 0)),
            state_spec,
        ],
        out_shape=[
            jax.ShapeDtypeStruct((batch_l * seq_l, R_V_WIDTH), BF16),
            jax.ShapeDtypeStruct((batch_l, R_HEADS, R_QK_DIM, R_V_DIM), F32),
            jax.ShapeDtypeStruct((batch_s * seq_s, R_V_WIDTH), BF16),
            jax.ShapeDtypeStruct((batch_s, R_HEADS, R_QK_DIM, R_V_DIM), F32),
        ],
        scratch_shapes=[pltpu.VMEM((R_HEADS, R_QK_DIM, R_V_DIM), F32)],
        compiler_params=_params("arbitrary", "arbitrary", vmem_limit_bytes=RETENTION_VMEM_LIMIT_BYTES),
        name="retention",
    )(f32(tabs_l["chunk_decay"]), f32(tabs_s["chunk_decay"]), h_long, h_long, h_long, h_long,
      f32(tabs_l["dmask"]), lane_bcast(tabs_l["q_decay"], R_CHUNK), lane_bcast(tabs_l["k_decay"], R_CHUNK),
      h_short, h_short, h_short, h_short, state, f32(dmask_s), qd_s, kd_s, gn)


def _retention_tables(c):
    log_g = np.log1p(-np.exp2(-5.0 - np.arange(R_HEADS, dtype=np.float64)))
    idx = np.arange(c, dtype=np.float64)
    diff = idx[:, None] - idx[None, :]
    causal = diff >= 0
    dmask = np.where(causal[None], np.exp(np.where(causal, diff, 0.0)[None] * log_g[:, None, None]), 0.0)
    q_decay = np.exp((idx[:, None] + 1.0) * log_g[None, :])
    k_decay = np.exp((c - 1.0 - idx)[:, None] * log_g[None, :])
    chunk_decay = np.exp(c * log_g)
    return {
        "dmask": dmask,
        "q_decay": q_decay.T[:, :, None],
        "k_decay": k_decay.T[:, :, None],
        "chunk_decay": chunk_decay,
    }


def _softmax_rows(s):
    e = jnp.exp(s - jnp.max(s, axis=-1, keepdims=True))
    return e, jnp.sum(e, axis=-1, keepdims=True)


def _mem_prompt_kernel(q_ref, mk_ref, mv_ref, o_ref):
    for h in range(M_HEADS):
        cols = slice(h * M_HEAD_DIM, (h + 1) * M_HEAD_DIM)
        s = lax.dot_general(q_ref[:, cols], mk_ref[:, cols], (((1,), (1,)), ((), ())),
                            preferred_element_type=F32)
        e, l = _softmax_rows(s)
        o = jnp.dot(e.astype(BF16), mv_ref[:, cols], preferred_element_type=F32) / l
        o_ref[:, cols] = o.astype(BF16)


def _mem_attn_prompt(h_act, mk, mv, *, batch, seq):
    tq = 2048
    n_t = seq // tq
    return pl.pallas_call(
        _mem_prompt_kernel,
        grid=(batch, n_t),
        in_specs=[
            pl.BlockSpec((tq, M_WIDTH), lambda b, t: (b * n_t + t, OFF_MQ // M_WIDTH)),
            pl.BlockSpec((N_MEM, M_WIDTH), lambda b, t: (b, 0)),
            pl.BlockSpec((N_MEM, M_WIDTH), lambda b, t: (b, 0)),
        ],
        out_specs=pl.BlockSpec((tq, M_WIDTH), lambda b, t: (b * n_t + t, 0)),
        out_shape=jax.ShapeDtypeStruct((batch * seq, M_WIDTH), BF16),
        compiler_params=_params("arbitrary", "arbitrary"),
        name="mem_attn_prompt",
    )(h_act, mk, mv)


MEM_SEQS_PER_STEP = 4


def _mem_attend_step(step, n_steps, q_ref, mk_hbm, mv_hbm, o_ref, kbuf, vbuf, sem, *, seq):
    bb = MEM_SEQS_PER_STEP
    pairs = [(b, h) for b in range(bb) for h in range(M_HEADS)]

    def copies(st, slot):
        out = []
        for b, h in pairs:
            out.append(pltpu.make_async_copy(mk_hbm.at[st * bb + b, :, h, :], kbuf.at[slot, b, h],
                                             sem.at[0, slot]))
            out.append(pltpu.make_async_copy(mv_hbm.at[st * bb + b, :, h, :], vbuf.at[slot, b, h],
                                             sem.at[1, slot]))
        return out

    slot = step % 2

    @pl.when(step == 0)
    def _():
        for n, c in enumerate(copies(0, 0)):
            c.start(priority=n % 2)

    @pl.when(step + 1 < n_steps)
    def _():
        for n, c in enumerate(copies(step + 1, 1 - slot)):
            c.start(priority=n % 2)

    for c in copies(step, slot):
        c.wait()
    q_all = q_ref[...].astype(F32)
    scores = []
    for b, h in pairs:
        q = q_all[b * seq:(b + 1) * seq, h * M_HEAD_DIM:(h + 1) * M_HEAD_DIM].astype(BF16)
        scores.append(lax.dot_general(q, kbuf[slot, b, h].astype(BF16), (((1,), (1,)), ((), ())),
                                      preferred_element_type=F32))
    e, l = _softmax_rows(jnp.concatenate(scores, axis=0))
    inv_l = 1.0 / l
    outs = []
    for n, (b, h) in enumerate(pairs):
        rows = slice(n * seq, (n + 1) * seq)
        o = jnp.dot(e[rows].astype(BF16), vbuf[slot, b, h].astype(BF16), preferred_element_type=F32)
        outs.append(o * inv_l[rows])
    out_rows = [jnp.concatenate(outs[b * M_HEADS:(b + 1) * M_HEADS], axis=1) for b in range(bb)]
    o_ref[...] = jnp.concatenate(out_rows, axis=0).astype(BF16)


MERGE_ROW_CHUNK = 512


def _merge_kernel(ga_ref, gb_ref, gc_ref, u_ref, va_ref, sw_ref, sb_ref, b_ref, c_ref,
                  wa_ref, wb_ref, wc_ref, o_ref, *rest, tm, first_pass):
    if first_pass:
        wab_ref, wbb_ref, wcb_ref, a_scr = rest
        wab_ref[...] = wa_ref[...].astype(BF16)
        wbb_ref[...] = wb_ref[...].astype(BF16)
        wcb_ref[...] = wc_ref[...].astype(BF16)
        wa_ref, wb_ref, wc_ref = wab_ref, wbb_ref, wcb_ref
    else:
        (a_scr,) = rest

    @pl.when(pl.program_id(1) == 0)
    def _():
        for r in range(tm // CHUNK):
            rows = slice(r * CHUNK, (r + 1) * CHUNK)
            for g in range(A_GROUPS):
                cols = slice(g * A_GROUP_DIM, (g + 1) * A_GROUP_DIM)
                z = jnp.dot(sw_ref[g], va_ref[rows, cols], preferred_element_type=F32) + sb_ref[g]
                a_scr[rows, cols] = (u_ref[rows, cols].astype(F32) * z).astype(BF16)

    for r in range(tm // MERGE_ROW_CHUNK):
        rows = slice(r * MERGE_ROW_CHUNK, (r + 1) * MERGE_ROW_CHUNK)
        ya = jnp.dot(a_scr[rows, :], wa_ref[...], preferred_element_type=F32)
        yb = jnp.dot(b_ref[rows, :], wb_ref[...], preferred_element_type=F32)
        yc = jnp.dot(c_ref[rows, :], wc_ref[...], preferred_element_type=F32)
        merged = (ga_ref[rows, :].astype(F32) * ya + gb_ref[rows, :].astype(F32) * yb
                  + gc_ref[rows, :].astype(F32) * yc)
        o_ref[rows, :] = merged.astype(BF16)


def _merge(h_act, b_act, c_act, sgu_w, sgu_b, wa, wb, wc, *, first_pass):
    m = h_act.shape[0]
    tm, tn = 1024, (512 if first_pass else 1024)
    n_j = D_MODEL // tn
    row_tile_mode = pl.Buffered(1) if first_pass else None
    w_specs = [pl.BlockSpec((A_WIDTH, tn), lambda i, j: (0, j)),
               pl.BlockSpec((R_V_WIDTH, tn), lambda i, j: (0, j)),
               pl.BlockSpec((M_WIDTH, tn), lambda i, j: (0, j))]
    out_specs = [pl.BlockSpec((tm, tn), lambda i, j: (i, j))]
    out_shape = [jax.ShapeDtypeStruct((m, D_MODEL), BF16)]
    if first_pass:
        assert m == tm
        out_specs += w_specs
        out_shape += [jax.ShapeDtypeStruct(w.shape, BF16) for w in (wa, wb, wc)]
    return pl.pallas_call(
        functools.partial(_merge_kernel, tm=tm, first_pass=first_pass),
        grid=(m // tm, n_j),
        in_specs=[
            pl.BlockSpec((tm, tn), lambda i, j: (i, j)),
            pl.BlockSpec((tm, tn), lambda i, j: (i, n_j + j)),
            pl.BlockSpec((tm, tn), lambda i, j: (i, 2 * n_j + j)),
            pl.BlockSpec((tm, A_WIDTH), lambda i, j: (i, OFF_AU // A_WIDTH), pipeline_mode=row_tile_mode),
            pl.BlockSpec((tm, A_WIDTH), lambda i, j: (i, OFF_AV // A_WIDTH), pipeline_mode=row_tile_mode),
            pl.BlockSpec((A_GROUPS, CHUNK, CHUNK), lambda i, j: (0, 0, 0)),
            pl.BlockSpec((A_GROUPS, CHUNK, 1), lambda i, j: (0, 0, 0)),
            pl.BlockSpec((tm, R_V_WIDTH), lambda i, j: (i, 0), pipeline_mode=row_tile_mode),
            pl.BlockSpec((tm, M_WIDTH), lambda i, j: (i, 0), pipeline_mode=row_tile_mode),
        ] + w_specs,
        out_specs=out_specs,
        out_shape=out_shape,
        scratch_shapes=[pltpu.VMEM((tm, A_WIDTH), BF16)],
        compiler_params=_params("arbitrary", "arbitrary", vmem_limit_bytes=MERGE_VMEM_LIMIT_BYTES),
        name="merge",
    )(h_act, h_act, h_act, h_act, h_act, sgu_w, sgu_b, b_act, c_act, wa, wb, wc)


OUT_ROW_CHUNK = 256


def _out_ln_kernel(m_ref, w_ref, x_ref, g_ref, b_ref, o_ref, wb_ref):
    wb_ref[...] = w_ref[...].astype(BF16)
    for r in range(m_ref.shape[0] // OUT_ROW_CHUNK):
        rows = slice(r * OUT_ROW_CHUNK, (r + 1) * OUT_ROW_CHUNK)
        y = jnp.dot(m_ref[rows, :], wb_ref[...], preferred_element_type=F32)
        o_ref[rows, :] = _standardize(ALPHA * x_ref[rows, :] + y) * g_ref[...] + b_ref[...]


def _out_ln(merged, w_out, x, ln_g, ln_b):
    tm = x.shape[0]
    full = lambda i: (0, 0)
    return pl.pallas_call(
        _out_ln_kernel,
        grid=(1,),
        in_specs=[
            pl.BlockSpec((tm, D_MODEL), full),
            pl.BlockSpec((D_MODEL, D_MODEL), full),
            pl.BlockSpec((tm, D_MODEL), full),
            pl.BlockSpec((1, D_MODEL), full),
            pl.BlockSpec((1, D_MODEL), full),
        ],
        out_specs=[pl.BlockSpec((tm, D_MODEL), full), pl.BlockSpec((D_MODEL, D_MODEL), full)],
        out_shape=[jax.ShapeDtypeStruct((tm, D_MODEL), F32), jax.ShapeDtypeStruct(w_out.shape, BF16)],
        compiler_params=_params("arbitrary"),
        name="out_ln",
    )(merged, w_out, x, ln_g, ln_b)


def _ffn_kernel(x_ref, wg_ref, wu_ref, wd_ref, g_ref, b_ref, o_ref, wgb_ref, wub_ref, wdb_ref,
                xb_ref, acc_ref):
    wgb_ref[...] = wg_ref[...].astype(BF16)
    wub_ref[...] = wu_ref[...].astype(BF16)
    wdb_ref[...] = wd_ref[...].astype(BF16)
    f = pl.program_id(1)

    @pl.when(f == 0)
    def _():
        xb_ref[...] = x_ref[...].astype(BF16)
        acc_ref[...] = jnp.zeros_like(acc_ref)

    xb = xb_ref[...]
    gate = jnp.dot(xb, wgb_ref[...], preferred_element_type=F32)
    up = jnp.dot(xb, wub_ref[...], preferred_element_type=F32)
    act = (jax.nn.silu(gate) * up).astype(BF16)
    acc_ref[...] += jnp.dot(act, wdb_ref[...], preferred_element_type=F32)

    @pl.when(f == pl.num_programs(1) - 1)
    def _():
        o_ref[...] = _standardize(ALPHA * x_ref[...] + acc_ref[...]) * g_ref[...] + b_ref[...]


def _ffn(x1, wg, wu, wd, ln_g, ln_b):
    tm = x1.shape[0]
    tf = 256
    one_buffer = pl.Buffered(1)
    out_shape = [jax.ShapeDtypeStruct((tm, D_MODEL), F32)] + [jax.ShapeDtypeStruct(w.shape, BF16)
                                                             for w in (wg, wu, wd)]
    out_specs = [pl.BlockSpec((tm, D_MODEL), lambda i, f: (i, 0), pipeline_mode=one_buffer),
                 pl.BlockSpec((D_MODEL, tf), lambda i, f: (0, f)),
                 pl.BlockSpec((D_MODEL, tf), lambda i, f: (0, f)),
                 pl.BlockSpec((tf, D_MODEL), lambda i, f: (f, 0))]
    return pl.pallas_call(
        _ffn_kernel,
        grid=(1, D_FF // tf),
        in_specs=[
            pl.BlockSpec((tm, D_MODEL), lambda i, f: (i, 0), pipeline_mode=one_buffer),
            pl.BlockSpec((D_MODEL, tf), lambda i, f: (0, f)),
            pl.BlockSpec((D_MODEL, tf), lambda i, f: (0, f)),
            pl.BlockSpec((tf, D_MODEL), lambda i, f: (f, 0)),
            pl.BlockSpec((1, D_MODEL), lambda i, f: (0, 0)),
            pl.BlockSpec((1, D_MODEL), lambda i, f: (0, 0)),
        ],
        out_specs=out_specs,
        out_shape=out_shape,
        scratch_shapes=[pltpu.VMEM((tm, D_MODEL), BF16), pltpu.VMEM((tm, D_MODEL), F32)],
        compiler_params=_params("arbitrary", "arbitrary"),
        name="ffn",
    )(x1, wg, wu, wd, ln_g, ln_b)


TAIL_ROW_CHUNK = 256


def _tail_kernel(m_ref, wo_ref, x_ref, g1_ref, b1_ref, wg_ref, wu_ref, wd_ref, g2_ref, b2_ref,
                 o_ref, x1_scr, x1b_scr):
    f = pl.program_id(1)
    n_f = pl.num_programs(1)
    chunks = [slice(r * TAIL_ROW_CHUNK, (r + 1) * TAIL_ROW_CHUNK)
              for r in range(m_ref.shape[0] // TAIL_ROW_CHUNK)]

    def swiglu(rows):
        xb = x1b_scr[rows, :]
        gate = jnp.dot(xb, wg_ref[...], preferred_element_type=F32)
        up = jnp.dot(xb, wu_ref[...], preferred_element_type=F32)
        act = (jax.nn.silu(gate) * up).astype(BF16)
        return jnp.dot(act, wd_ref[...], preferred_element_type=F32)

    @pl.when(f == 0)
    def _():
        for rows in chunks:
            y = jnp.dot(m_ref[rows, :], wo_ref[...], preferred_element_type=F32)
            x1 = _standardize(ALPHA * x_ref[rows, :] + y) * g1_ref[...] + b1_ref[...]
            x1_scr[rows, :] = x1
            x1b_scr[rows, :] = x1.astype(BF16)
        for rows in chunks:
            o_ref[rows, :] = swiglu(rows)

    @pl.when((f > 0) & (f < n_f - 1))
    def _():
        o_ref[...] += swiglu(slice(None))

    @pl.when(f == n_f - 1)
    def _():
        for rows in chunks:
            ff = o_ref[rows, :] + swiglu(rows)
            o_ref[rows, :] = _standardize(ALPHA * x1_scr[rows, :] + ff) * g2_ref[...] + b2_ref[...]


def _tail(merged, w_out, x, ln1_g, ln1_b, wg, wu, wd, ln2_g, ln2_b):
    m = x.shape[0]
    tm, tf = 512, 512
    assert D_FF // tf >= 2
    row = lambda i, f: (i, 0)
    const = lambda i, f: (0, 0)
    return pl.pallas_call(
        _tail_kernel,
        grid=(m // tm, D_FF // tf),
        in_specs=[
            pl.BlockSpec((tm, D_MODEL), row),
            pl.BlockSpec((D_MODEL, D_MODEL), const, pipeline_mode=pl.Buffered(1)),
            pl.BlockSpec((tm, D_MODEL), row),
            pl.BlockSpec((1, D_MODEL), const),
            pl.BlockSpec((1, D_MODEL), const),
            pl.BlockSpec((D_MODEL, tf), lambda i, f: (0, f)),
            pl.BlockSpec((D_MODEL, tf), lambda i, f: (0, f)),
            pl.BlockSpec((tf, D_MODEL), lambda i, f: (f, 0)),
            pl.BlockSpec((1, D_MODEL), const),
            pl.BlockSpec((1, D_MODEL), const),
        ],
        out_specs=pl.BlockSpec((tm, D_MODEL), row),
        out_shape=jax.ShapeDtypeStruct((m, D_MODEL), F32),
        scratch_shapes=[pltpu.VMEM((tm, D_MODEL), F32), pltpu.VMEM((tm, D_MODEL), BF16)],
        compiler_params=_params("arbitrary", "arbitrary"),
        name="tail",
    )(merged, w_out, x, ln1_g, ln1_b, wg, wu, wd, ln2_g, ln2_b)


def _rotary_tables(pos_start, n_pos, reps):
    half = R_QK_DIM // 2
    inv = ROPE_BASE ** (-np.arange(half, dtype=np.float64) / half)
    ang = (pos_start + np.arange(n_pos, dtype=np.float64))[:, None] * inv[None, :]
    cos = np.cos(ang)
    sin = np.sin(ang)
    cos2 = np.concatenate([cos, cos], axis=-1)
    sin2 = np.concatenate([-sin, sin], axis=-1)
    k_scale = R_QK_DIM ** -0.5
    table = lambda t: jnp.asarray(np.tile(np.stack([t, t * k_scale]), (1, reps, 1)), F32)
    return table(cos2), table(sin2)


def _sgu_tables(sgu_w, sgu_b, c):
    w = jnp.tril(sgu_w[:, :c, :c])
    b = sgu_b[:, :c]
    reps = CHUNK // c
    if reps > 1:
        blk = jnp.arange(CHUNK) // c
        w = jnp.where((blk[:, None] == blk[None, :])[None], jnp.tile(w, (1, reps, reps)), 0.0)
        b = jnp.tile(b, (1, reps))
    return w.astype(BF16), b[:, :, None]


def kernel(x_prompt, x_sample, mem_prompt, state_ret, cache_mem_k, cache_mem_v, w_in, sgu_ln_g, sgu_ln_b, sgu_w, sgu_b, w_proj_a, ret_gn_g, w_proj_b, w_mem_k, w_mem_v, w_proj_c, w_out, ln1_g, ln1_b, w_ffn_gate, w_ffn_up, w_ffn_down, ln2_g, ln2_b):
    assert w_in.shape[0] == DEPTH == 1
    bp, seq, _ = x_prompt.shape
    bs, dseq, _ = x_sample.shape
    l = 0
    layer = lambda a: a.reshape(a.shape[1:])
    ln_g = sgu_ln_g[l].reshape(1, A_WIDTH)
    ln_b = sgu_ln_b[l].reshape(1, A_WIDTH)
    gn = ret_gn_g[l].reshape(1, R_V_WIDTH)
    ln1g, ln1b = ln1_g[l].reshape(1, D_MODEL), ln1_b[l].reshape(1, D_MODEL)
    ln2g, ln2b = ln2_g[l].reshape(1, D_MODEL), ln2_b[l].reshape(1, D_MODEL)

    m_s = bs * dseq
    xs = x_sample.reshape(m_s, D_MODEL)
    cos_s, sin_s = _rotary_tables(float(PAST_LEN), dseq, reps=bs)
    h_s, va_s, w_in_b = _in_proj(xs, layer(w_in), cos_s, sin_s, ln_g, ln_b, tm=m_s)

    xp = x_prompt.reshape(bp * seq, D_MODEL)
    cos_p, sin_p = _rotary_tables(0.0, seq, reps=1)
    h_p, c_s = _in_proj(xp, w_in_b, cos_p, sin_p, ln_g, ln_b, tm=1024,
                        side=(h_s, layer(cache_mem_k), layer(cache_mem_v), dseq))
    b_p, s_p, b_s, s_s = _retention(h_p, h_s, layer(state_ret), gn,
                                    batch_l=bp, seq_l=seq, batch_s=bs, seq_s=dseq)

    m_s_merged, wa_b, wb_b, wc_b = _merge(h_s, b_s, c_s, *_sgu_tables(sgu_w[l], sgu_b[l], min(dseq, CHUNK)),
                                          layer(w_proj_a), layer(w_proj_b), layer(w_proj_c), first_pass=True)
    x1_s, wo_b = _out_ln(m_s_merged, layer(w_out), xs, ln1g, ln1b)
    y_s, wg_b, wu_b, wd_b = _ffn(x1_s, layer(w_ffn_gate), layer(w_ffn_up), layer(w_ffn_down), ln2g, ln2b)

    memp = mem_prompt.reshape(bp * N_MEM, D_MODEL)
    mk_p, mk_pb, mv_p, mv_pb = _mem_kv(memp, layer(w_mem_k), layer(w_mem_v))
    c_p = _mem_attn_prompt(h_p, mk_pb, mv_pb, batch=bp, seq=seq)
    (m_p_merged,) = _merge(h_p, b_p, c_p, *_sgu_tables(sgu_w[l], sgu_b[l], min(seq, CHUNK)),
                           wa_b, wb_b, wc_b, first_pass=False)
    y_p = _tail(m_p_merged, wo_b, xp, ln1g, ln1b, wg_b, wu_b, wd_b, ln2g, ln2b)

    return (
        y_p.reshape(bp, seq, D_MODEL),
        y_s.reshape(bs, dseq, D_MODEL),
        s_p[None],
        mk_p.reshape(1, bp, N_MEM, M_HEADS, M_HEAD_DIM),
        mv_p.reshape(1, bp, N_MEM, M_HEADS, M_HEAD_DIM),
        s_s[None],
        va_s.reshape(1, bs, dseq, A_GROUPS, A_GROUP_DIM),
    )
```
